```python
import math
import jax
import jax.numpy as jnp
from jax import lax
import numpy as np

D_MODEL = 1024
BATCH = 16
SEQ = 2048
DEPTH = 4
DEC_BATCH = 128
DEC_SEQ = 8
PAST_LEN = 8192
PAGE_SIZE = 128

N_BRANCH = 4
BRANCH_W = D_MODEL // 2
HEAD_DIM = 64
SC_W = BRANCH_W
SC_K = 3
MLA_HEADS = BRANCH_W // HEAD_DIM
MLA_NOPE = HEAD_DIM
MLA_ROPE = HEAD_DIM // 2
MLA_V = HEAD_DIM
MLA_QLORA = D_MODEL // 4
MLA_KVLORA = D_MODEL // 8
ROPE_THETA = 10000.0
Q_BLOCK = 128
GDN_HEADS = BRANCH_W // HEAD_DIM
GDN_DK = HEAD_DIM
GDN_DV = HEAD_DIM
GDN_K = 4
GDN_CHUNK = 64
GDN_QKV = GDN_HEADS * (2 * GDN_DK + GDN_DV)
LRU_W = BRANCH_W
LRU_BLOCKS = BRANCH_W // HEAD_DIM
LRU_BD = LRU_W // LRU_BLOCKS
LRU_K = 4
LRU_C = 8.0
FFN_HIDDEN = -(-8 * D_MODEL // (3 * 256)) * 256
IN_SIZES = (3 * SC_W, MLA_QLORA, MLA_KVLORA + MLA_ROPE, GDN_QKV, GDN_HEADS * GDN_DV, GDN_HEADS, GDN_HEADS, LRU_W, LRU_W)
IN_WIDTH = sum(IN_SIZES)
EPS = 1e-6
STATE_KEYS = ('ckv', 'kpe', 'sconv', 'gdn_conv', 'gdn', 'lru_conv', 'lru')

kernel_name = 'hybrid_conditioned_decoder_step'


def rmsnorm(x, g):
    xf = x.astype(jnp.float32)
    y = xf * lax.rsqrt(jnp.mean(xf * xf, axis=-1, keepdims=True) + EPS)
    return (y * g.astype(jnp.float32)).astype(x.dtype)


def l2norm(x):
    return x * lax.rsqrt(jnp.sum(x * x, axis=-1, keepdims=True) + EPS)


def causal_dwconv(u, buf, w, b=None):
    k_w = w.shape[0]
    t = u.shape[1]
    ext = jnp.concatenate([buf.astype(u.dtype), u], axis=1)
    out = w[0] * ext[:, 0:t]
    for j in range(1, k_w):
        out = out + w[j] * ext[:, j:j + t]
    if b is not None:
        out = out + b
    return out, ext[:, t:]


def rope(x, pos):
    half = x.shape[-1] // 2
    inv = ROPE_THETA ** (-jnp.arange(half, dtype=jnp.float32) / half)
    ang = pos.astype(jnp.float32)[:, None] * inv[None, :]
    shape = (1, pos.shape[0]) + (1,) * (x.ndim - 3) + (half,)
    cos = jnp.cos(ang).reshape(shape)
    sin = jnp.sin(ang).reshape(shape)
    xf = x.astype(jnp.float32)
    x1, x2 = xf[..., :half], xf[..., half:]
    return jnp.concatenate([x1 * cos - x2 * sin, x1 * sin + x2 * cos], axis=-1).astype(x.dtype)


def mla_attend_prompt(q_abs, q_pe, ckv, kpe):
    b, t, h, c = q_abs.shape
    nb = t // Q_BLOCK
    scale = (MLA_NOPE + MLA_ROPE) ** -0.5
    qa = q_abs.reshape(b, nb, Q_BLOCK, h, c).swapaxes(0, 1)
    qp = q_pe.reshape(b, nb, Q_BLOCK, h, MLA_ROPE).swapaxes(0, 1)
    kpos = jnp.arange(t)

    def block(args):
        i, qa_i, qp_i = args
        s = jnp.einsum('bqhc,bkc->bhqk', qa_i, ckv) + jnp.einsum('bqhr,bkr->bhqk', qp_i, kpe)
        s = s.astype(jnp.float32) * scale
        qpos = i * Q_BLOCK + jnp.arange(Q_BLOCK)
        s = jnp.where(kpos[None, :] <= qpos[:, None], s, -jnp.inf)
        p = jax.nn.softmax(s, axis=-1).astype(ckv.dtype)
        return jnp.einsum('bhqk,bkc->bqhc', p, ckv)

    o = lax.map(block, (jnp.arange(nb), qa, qp))
    return o.swapaxes(0, 1).reshape(b, t, h, c)


def mla_attend_sample(q_abs, q_pe, ckv_past, kpe_past, ckv_new, kpe_new):
    t = q_abs.shape[1]
    p_len = ckv_past.shape[1]
    scale = (MLA_NOPE + MLA_ROPE) ** -0.5
    s_past = jnp.einsum('bqhc,bkc->bhqk', q_abs, ckv_past) + jnp.einsum('bqhr,bkr->bhqk', q_pe, kpe_past)
    s_new = jnp.einsum('bqhc,bkc->bhqk', q_abs, ckv_new) + jnp.einsum('bqhr,bkr->bhqk', q_pe, kpe_new)
    causal = jnp.tril(jnp.ones((t, t), dtype=bool))
    s_new = jnp.where(causal, s_new.astype(jnp.float32) * scale, -jnp.inf)
    s = jnp.concatenate([s_past.astype(jnp.float32) * scale, s_new], axis=-1)
    p = jax.nn.softmax(s, axis=-1).astype(ckv_new.dtype)
    return (jnp.einsum('bhqk,bkc->bqhc', p[..., :p_len], ckv_past)
            + jnp.einsum('bhqk,bkc->bqhc', p[..., p_len:], ckv_new))


def gated_delta_rule(q, k, v, g, beta, s0):
    b, t, h, dk = q.shape
    dv = v.shape[-1]
    c = GDN_CHUNK
    f32 = jnp.float32
    q = l2norm(q.astype(f32)) * (dk ** -0.5)
    k = l2norm(k.astype(f32))
    v = v.astype(f32)
    pad = (-t) % c
    n = (t + pad) // c

    def to_chunks(x):
        x = jnp.pad(x, [(0, 0), (0, pad)] + [(0, 0)] * (x.ndim - 2))
        x = x.reshape((b, n, c) + x.shape[2:])
        return jnp.moveaxis(x, 1, 0).swapaxes(2, 3)

    qc, kc, vc = to_chunks(q), to_chunks(k), to_chunks(v)
    gc = jnp.cumsum(to_chunks(g.astype(f32)), axis=-1)
    bc = to_chunks(beta.astype(f32))
    tril = jnp.tril(jnp.ones((c, c), dtype=bool))
    strict = jnp.tril(jnp.ones((c, c), dtype=bool), -1)
    diff = gc[..., :, None] - gc[..., None, :]
    decay = jnp.where(tril, jnp.exp(jnp.where(tril, diff, 0.0)), 0.0)
    kb = kc * bc[..., None]
    a_mat = jnp.where(strict, jnp.einsum('nbhid,nbhjd->nbhij', kb, kc) * decay, 0.0)
    rhs = jnp.concatenate([vc * bc[..., None], kb * jnp.exp(gc)[..., None]], axis=-1)
    sol = lax.linalg.triangular_solve(jnp.eye(c, dtype=f32) + a_mat, rhs,
                                      left_side=True, lower=True, unit_diagonal=True)
    u, w = sol[..., :dv], sol[..., dv:]
    intra = jnp.where(tril, jnp.einsum('nbhid,nbhjd->nbhij', qc, kc) * decay, 0.0)

    def step(s_mat, xs):
        q_i, k_i, u_i, w_i, g_i, att_i = xs
        v_new = u_i - jnp.einsum('bhcd,bhde->bhce', w_i, s_mat)
        o_i = (jnp.einsum('bhcd,bhde->bhce', q_i * jnp.exp(g_i)[..., None], s_mat)
               + jnp.einsum('bhij,bhje->bhie', att_i, v_new))
        g_last = g_i[..., -1:]
        s_mat = (s_mat * jnp.exp(g_last)[..., None]
                 + jnp.einsum('bhcd,bhce->bhde', k_i * jnp.exp(g_last - g_i)[..., None], v_new))
        return s_mat, o_i

    s_fin, o = lax.scan(step, s0.astype(f32), (qc, kc, u, w, gc, intra))
    o = jnp.moveaxis(o.swapaxes(2, 3), 0, 1).reshape(b, n * c, h, dv)[:, :t]
    return o, s_fin


def linear_recurrence(a, bx, h0):
    bx = bx.at[:, 0].add(a[:, 0] * h0)

    def comb(l, r):
        return (l[0] * r[0], r[0] * l[1] + r[1])

    _, hs = lax.associative_scan(comb, (a, bx), axis=1)
    return hs, hs[:, -1]


def mixer_block(h, pos, st, p, past):
    b, t, _ = h.shape
    f32 = jnp.float32
    points = [int(v) for v in np.cumsum(IN_SIZES)[:-1]]
    sc_in, q_a, kv_a, gdn_qkv, gdn_z, gdn_a, gdn_b, lru_x, lru_g = jnp.split(h @ p['w_in'], points, axis=-1)

    b_gate, c_gate, x_t = jnp.split(sc_in, 3, axis=-1)
    conv_a, sc_buf = causal_dwconv(c_gate * x_t, st['sconv'], p['w_sc_conv'])
    y_a = b_gate * conv_a

    cq = rmsnorm(q_a, p['g_q_norm'])
    q = (cq @ p['w_qb']).reshape(b, t, MLA_HEADS, MLA_NOPE + MLA_ROPE)
    q_nope = q[..., :MLA_NOPE]
    q_pe = rope(q[..., MLA_NOPE:], pos)
    ckv = rmsnorm(kv_a[..., :MLA_KVLORA], p['g_kv_norm'])
    kpe = rope(kv_a[..., MLA_KVLORA:], pos)
    w_kb = p['w_kvb'][..., :MLA_NOPE]
    w_vb = p['w_kvb'][..., MLA_NOPE:]
    q_abs = jnp.einsum('bthd,chd->bthc', q_nope, w_kb)
    if past is None:
        o_lat = mla_attend_prompt(q_abs, q_pe, ckv, kpe)
    else:
        o_lat = mla_attend_sample(q_abs, q_pe, past[0], past[1], ckv, kpe)
    y_b = jnp.einsum('bthc,chd->bthd', o_lat, w_vb).reshape(b, t, MLA_HEADS * MLA_V)

    qkv_c, gdn_buf = causal_dwconv(gdn_qkv, st['gdn_conv'], p['w_gdn_conv'])
    qkv_c = jax.nn.silu(qkv_c)
    q_g, k_g, v_g = jnp.split(qkv_c, [GDN_HEADS * GDN_DK, 2 * GDN_HEADS * GDN_DK], axis=-1)
    q_g = q_g.reshape(b, t, GDN_HEADS, GDN_DK)
    k_g = k_g.reshape(b, t, GDN_HEADS, GDN_DK)
    v_g = v_g.reshape(b, t, GDN_HEADS, GDN_DV)
    beta = jax.nn.sigmoid(gdn_b.astype(f32))
    g_log = -jnp.exp(p['gdn_a_log'].astype(f32)) * jax.nn.softplus(gdn_a.astype(f32) + p['gdn_dt_bias'].astype(f32))
    o_c, s_gdn = gated_delta_rule(q_g, k_g, v_g, g_log, beta, st['gdn'])
    o_c = rmsnorm(o_c.astype(h.dtype), p['g_gdn_norm']) * jax.nn.silu(gdn_z.reshape(b, t, GDN_HEADS, GDN_DV))
    y_c = o_c.reshape(b, t, GDN_HEADS * GDN_DV)

    u, lru_buf = causal_dwconv(lru_x, st['lru_conv'], p['w_lru_conv'], p['b_lru_conv'])
    ub = u.reshape(b, t, LRU_BLOCKS, LRU_BD)
    r = jax.nn.sigmoid(jnp.einsum('btni,nij->btnj', ub, p['w_lru_gate_a']).reshape(b, t, LRU_W) + p['b_lru_gate_a'])
    ig = jax.nn.sigmoid(jnp.einsum('btni,nij->btnj', ub, p['w_lru_gate_x']).reshape(b, t, LRU_W) + p['b_lru_gate_x'])
    log_a = -LRU_C * r.astype(f32) * jax.nn.softplus(-p['lru_lambda'].astype(f32))
    a = jnp.exp(log_a)
    mult = jnp.sqrt(1.0 - jnp.exp(2.0 * log_a))
    mult = jnp.where((pos == 0)[None, :, None], 1.0, mult)
    hs, h_last = linear_recurrence(a, mult * (ig * u).astype(f32), st['lru'].astype(f32))
    y_d = hs.astype(h.dtype) * jax.nn.gelu(lru_g)

    ys = jnp.stack([y_a, y_b, y_c, y_d], axis=2)
    proj = jnp.einsum('btnc,ncd->btnd', ys, p['w_branch_out'])
    gates = jax.nn.sigmoid((h @ p['w_merge_gate']).reshape(b, t, N_BRANCH, D_MODEL))
    mix = jnp.einsum('btnd,btnd->btd', gates, proj) @ p['w_mix_out']
    new_st = {'ckv': ckv, 'kpe': kpe, 'sconv': sc_buf, 'gdn_conv': gdn_buf,
              'gdn': s_gdn.astype(h.dtype), 'lru_conv': lru_buf, 'lru': h_last.astype(h.dtype)}
    return mix, new_st


def trunk_layer(x, c, pos, st, p, past):
    mod = (jax.nn.silu(c) @ p['w_ada'] + p['b_ada'])[:, None, :]
    sh1, sc1, gt1, sh2, sc2, gt2 = jnp.split(mod, 6, axis=-1)
    hmix = rmsnorm(x, p['g_norm_mix']) * (1.0 + sc1) + sh1
    mix, new_st = mixer_block(hmix, pos, st, p, past)
    x = x + gt1 * mix
    hffn = rmsnorm(x, p['g_norm_ffn']) * (1.0 + sc2) + sh2
    gate, up = jnp.split(hffn @ p['w_ffn_in'], 2, axis=-1)
    x = x + gt2 * ((jax.nn.silu(gate) * up) @ p['w_ffn_out'])
    return x, new_st


def setup_inputs(seed: int = 0) -> dict:
    key = jax.random.key(seed)
    keys = jax.random.split(key, 48)
    f32 = jnp.float32
    cnt = [0]

    def nxt():
        k = keys[cnt[0]]
        cnt[0] += 1
        return k

    def nrm(shape, scale):
        return jax.random.normal(nxt(), shape, f32) * scale

    def gain(shape):
        return 1.0 + nrm(shape, 0.02)

    def unif(shape, lo, hi):
        return jax.random.uniform(nxt(), shape, f32, lo, hi)

    n_pages = PAST_LEN // PAGE_SIZE
    n_used = DEC_BATCH * n_pages
    n_pool = (5 * n_used) // 4
    page_table = jax.random.permutation(nxt(), n_pool)[:n_used].reshape(DEC_BATCH, n_pages).astype(jnp.int32)
    a0 = unif((DEPTH, LRU_W), 0.9, 0.999)
    s0 = a0 ** (1.0 / LRU_C)
    dt0 = unif((DEPTH, GDN_HEADS), 0.001, 0.1)
    return {
        'x_prompt': nrm((BATCH, SEQ, D_MODEL), 1.0),
        'x_sample': nrm((DEC_BATCH, DEC_SEQ, D_MODEL), 1.0),
        'c_prompt': nrm((BATCH, D_MODEL), 1.0),
        'c_sample': nrm((DEC_BATCH, D_MODEL), 1.0),
        'cache_mla_ckv': nrm((DEPTH, n_pool, PAGE_SIZE, MLA_KVLORA), 1.0),
        'cache_mla_kpe': nrm((DEPTH, n_pool, PAGE_SIZE, MLA_ROPE), 1.0),
        'page_table': page_table,
        'state_sconv': nrm((DEPTH, DEC_BATCH, SC_K - 1, SC_W), 1.0),
        'state_gdn_conv': nrm((DEPTH, DEC_BATCH, GDN_K - 1, GDN_QKV), 1.0),
        'state_gdn': nrm((DEPTH, DEC_BATCH, GDN_HEADS, GDN_DK, GDN_DV), 0.1),
        'state_lru_conv': nrm((DEPTH, DEC_BATCH, LRU_K - 1, LRU_W), 1.0),
        'state_lru': nrm((DEPTH, DEC_BATCH, LRU_W), 0.5),
        'w_ada': nrm((DEPTH, D_MODEL, 6 * D_MODEL), D_MODEL ** -0.5),
        'b_ada': nrm((DEPTH, 6 * D_MODEL), 0.02),
        'g_norm_mix': gain((DEPTH, D_MODEL)),
        'g_norm_ffn': gain((DEPTH, D_MODEL)),
        'w_in': nrm((DEPTH, D_MODEL, IN_WIDTH), D_MODEL ** -0.5),
        'w_sc_conv': nrm((DEPTH, SC_K, SC_W), SC_K ** -0.5),
        'g_q_norm': gain((DEPTH, MLA_QLORA)),
        'w_qb': nrm((DEPTH, MLA_QLORA, MLA_HEADS * (MLA_NOPE + MLA_ROPE)), MLA_QLORA ** -0.5),
        'g_kv_norm': gain((DEPTH, MLA_KVLORA)),
        'w_kvb': nrm((DEPTH, MLA_KVLORA, MLA_HEADS, MLA_NOPE + MLA_V), MLA_KVLORA ** -0.5),
        'w_gdn_conv': nrm((DEPTH, GDN_K, GDN_QKV), GDN_K ** -0.5),
        'gdn_a_log': jnp.log(unif((DEPTH, GDN_HEADS), 1.0, 16.0)),
        'gdn_dt_bias': dt0 + jnp.log(-jnp.expm1(-dt0)),
        'g_gdn_norm': gain((DEPTH, GDN_DV)),
        'w_lru_conv': nrm((DEPTH, LRU_K, LRU_W), LRU_K ** -0.5),
        'b_lru_conv': nrm((DEPTH, LRU_W), 0.02),
        'w_lru_gate_a': nrm((DEPTH, LRU_BLOCKS, LRU_BD, LRU_BD), LRU_BD ** -0.5),
        'b_lru_gate_a': nrm((DEPTH, LRU_W), 0.02),
        'w_lru_gate_x': nrm((DEPTH, LRU_BLOCKS, LRU_BD, LRU_BD), LRU_BD ** -0.5),
        'b_lru_gate_x': nrm((DEPTH, LRU_W), 0.02),
        'lru_lambda': jnp.log(s0) - jnp.log1p(-s0),
        'w_branch_out': nrm((DEPTH, N_BRANCH, BRANCH_W, D_MODEL), BRANCH_W ** -0.5),
        'w_merge_gate': nrm((DEPTH, D_MODEL, N_BRANCH * D_MODEL), D_MODEL ** -0.5),
        'w_mix_out': nrm((DEPTH, D_MODEL, D_MODEL), D_MODEL ** -0.5),
        'w_ffn_in': nrm((DEPTH, D_MODEL, 2 * FFN_HIDDEN), D_MODEL ** -0.5),
        'w_ffn_out': nrm((DEPTH, FFN_HIDDEN, D_MODEL), FFN_HIDDEN ** -0.5),
        'g_final': gain((D_MODEL,)),
    }


def reference(x_prompt, x_sample, c_prompt, c_sample, cache_mla_ckv, cache_mla_kpe, page_table,
              state_sconv, state_gdn_conv, state_gdn, state_lru_conv, state_lru,
              w_ada, b_ada, g_norm_mix, g_norm_ffn, w_in, w_sc_conv, g_q_norm, w_qb, g_kv_norm, w_kvb,
              w_gdn_conv, gdn_a_log, gdn_dt_bias, g_gdn_norm, w_lru_conv, b_lru_conv,
              w_lru_gate_a, b_lru_gate_a, w_lru_gate_x, b_lru_gate_x, lru_lambda,
              w_branch_out, w_merge_gate, w_mix_out, w_ffn_in, w_ffn_out, g_final):
    params = {'w_ada': w_ada, 'b_ada': b_ada, 'g_norm_mix': g_norm_mix, 'g_norm_ffn': g_norm_ffn,
              'w_in': w_in, 'w_sc_conv': w_sc_conv, 'g_q_norm': g_q_norm, 'w_qb': w_qb,
              'g_kv_norm': g_kv_norm, 'w_kvb': w_kvb, 'w_gdn_conv': w_gdn_conv, 'gdn_a_log': gdn_a_log,
              'gdn_dt_bias': gdn_dt_bias, 'g_gdn_norm': g_gdn_norm, 'w_lru_conv': w_lru_conv,
              'b_lru_conv': b_lru_conv, 'w_lru_gate_a': w_lru_gate_a, 'b_lru_gate_a': b_lru_gate_a,
              'w_lru_gate_x': w_lru_gate_x, 'b_lru_gate_x': b_lru_gate_x, 'lru_lambda': lru_lambda,
              'w_branch_out': w_branch_out, 'w_merge_gate': w_merge_gate, 'w_mix_out': w_mix_out,
              'w_ffn_in': w_ffn_in, 'w_ffn_out': w_ffn_out}
    b_p, s_p, _ = x_prompt.shape
    b_s, t_s, _ = x_sample.shape
    n_pages = page_table.shape[1]
    past_len = n_pages * cache_mla_ckv.shape[2]
    dt = x_prompt.dtype
    pos_p = jnp.arange(s_p, dtype=jnp.int32)
    pos_s = past_len + jnp.arange(t_s, dtype=jnp.int32)
    xp, xs = x_prompt, x_sample
    out_p = {k: [] for k in STATE_KEYS}
    out_s = {k: [] for k in STATE_KEYS}
    for l in range(DEPTH):
        pl = {k: v[l] for k, v in params.items()}
        st_p = {'sconv': jnp.zeros((b_p, SC_K - 1, SC_W), dt),
                'gdn_conv': jnp.zeros((b_p, GDN_K - 1, GDN_QKV), dt),
                'gdn': jnp.zeros((b_p, GDN_HEADS, GDN_DK, GDN_DV), dt),
                'lru_conv': jnp.zeros((b_p, LRU_K - 1, LRU_W), dt),
                'lru': jnp.zeros((b_p, LRU_W), dt)}
        xp, nst_p = trunk_layer(xp, c_prompt, pos_p, st_p, pl, None)
        st_s = {'sconv': state_sconv[l], 'gdn_conv': state_gdn_conv[l], 'gdn': state_gdn[l],
                'lru_conv': state_lru_conv[l], 'lru': state_lru[l]}
        past = (cache_mla_ckv[l][page_table].reshape(b_s, past_len, MLA_KVLORA),
                cache_mla_kpe[l][page_table].reshape(b_s, past_len, MLA_ROPE))
        xs, nst_s = trunk_layer(xs, c_sample, pos_s, st_s, pl, past)
        for k in STATE_KEYS:
            out_p[k].append(nst_p[k])
            out_s[k].append(nst_s[k])
    y_prompt = rmsnorm(xp, g_final)
    y_sample = rmsnorm(xs, g_final)
    p_ckv = jnp.stack(out_p['ckv'])
    p_kpe = jnp.stack(out_p['kpe'])
    p_sconv = jnp.stack(out_p['sconv'])
    p_gdn_conv = jnp.stack(out_p['gdn_conv'])
    p_gdn = jnp.stack(out_p['gdn'])
    p_lru_conv = jnp.stack(out_p['lru_conv'])
    p_lru = jnp.stack(out_p['lru'])
    s_ckv = jnp.stack(out_s['ckv'])
    s_kpe = jnp.stack(out_s['kpe'])
    s_sconv = jnp.stack(out_s['sconv'])
    s_gdn_conv = jnp.stack(out_s['gdn_conv'])
    s_gdn = jnp.stack(out_s['gdn'])
    s_lru_conv = jnp.stack(out_s['lru_conv'])
    s_lru = jnp.stack(out_s['lru'])
    return (y_prompt, y_sample, p_ckv, p_kpe, p_sconv, p_gdn_conv, p_gdn, p_lru_conv, p_lru,
            s_ckv, s_kpe, s_sconv, s_gdn_conv, s_gdn, s_lru_conv, s_lru)
```

```python
import functools
import math

import jax
import jax.numpy as jnp
from jax import lax
from jax.experimental import pallas as pl
from jax.experimental.pallas import tpu as pltpu

F32 = jnp.float32
BF16 = jnp.bfloat16

HEAD_DIM = 64
N_HEADS = 8
BRANCH_W = N_HEADS * HEAD_DIM
MLA_ROPE = HEAD_DIM // 2
MLA_QLORA = 256
MLA_KVLORA = 128
ROPE_THETA = 10000.0
SC_K = 3
GDN_K = 4
LRU_K = 4
LRU_C = 8.0
GDN_CHUNK = 64
EPS = 1e-6
NEG_BIG = -1e30
LANE = 128
SUBLANE = 8
QK_W = 2 * LANE

OFF_SC = 0
OFF_GQKV = 1536
OFF_GZ = 3072
OFF_GA = 3584
OFF_GB = 4096
OFF_LX = 4608
OFF_LG = 5120
OFF_QA = 5632
OFF_CKV = 5888
OFF_KPA = 6016
OFF_KPB = 6144
PROJ_W = 6272
PROJ_TN = 896


def _cparams(n_axes, vmem_mib):
    return pltpu.CompilerParams(dimension_semantics=("arbitrary",) * n_axes,
                                vmem_limit_bytes=vmem_mib * 1024 * 1024)


def _const_spec(shape):
    nd = len(shape)
    return pl.BlockSpec(shape, lambda *_: (0,) * nd, pipeline_mode=pl.Buffered(1))


def _bdot(a, b):
    return jnp.dot(a.astype(BF16), b.astype(BF16), preferred_element_type=F32)


def _bdot_nt(a, b):
    return lax.dot_general(a.astype(BF16), b.astype(BF16), (((1,), (1,)), ((), ())),
                           preferred_element_type=F32)


def _bdot_tn(a, b):
    return lax.dot_general(a.astype(BF16), b.astype(BF16), (((0,), (0,)), ((), ())),
                           preferred_element_type=F32)


def _silu(x):
    return x * jax.nn.sigmoid(x)


def _softplus(x):
    return jnp.maximum(x, 0.0) + jnp.log(1.0 + jnp.exp(-jnp.abs(x)))


def _gelu_tanh(x):
    return 0.5 * x * (1.0 + jnp.tanh(math.sqrt(2.0 / math.pi) * (x + 0.044715 * (x * x * x))))


def _rms(x, g):
    return x * lax.rsqrt(jnp.mean(x * x, axis=-1, keepdims=True) + EPS) * g


def _modulated_norm(x3, g, mod, shift_row, scale_row):
    y = _rms(x3, g)
    return y * (1.0 + mod[:, scale_row:scale_row + 1, :]) + mod[:, shift_row:shift_row + 1, :]


def _ada_kernel(c_ref, w_ref, b_ref, o_ref):
    o_ref[0] = _bdot(_silu(c_ref[...]), w_ref[0]) + b_ref[0]


def _ada(c_all, w_ada, b_ada):
    depth, d, n = w_ada.shape
    nb = c_all.shape[0]
    tn = n // 4
    return pl.pallas_call(
        _ada_kernel,
        grid=(depth, n // tn),
        in_specs=[pl.BlockSpec((nb, d), lambda l, j: (0, 0)),
                  pl.BlockSpec((1, d, tn), lambda l, j: (l, 0, j)),
                  pl.BlockSpec((1, 1, tn), lambda l, j: (l, 0, j))],
        out_specs=pl.BlockSpec((1, nb, tn), lambda l, j: (l, 0, j)),
        out_shape=jax.ShapeDtypeStruct((depth, nb, n), F32),
        compiler_params=_cparams(2, 40),
        name="ada_mod",
    )(c_all, w_ada, b_ada.reshape(depth, 1, n))


def _inproj_kernel(x_ref, mod_ref, g_ref, w_ref, o_ref, h_scr):
    @pl.when(pl.program_id(1) == 0)
    def _():
        h = _modulated_norm(x_ref[...], g_ref[...], mod_ref[...], 0, 1)
        h_scr[...] = h.reshape(h_scr.shape).astype(BF16)

    o_ref[...] = jnp.dot(h_scr[...], w_ref[...], preferred_element_type=F32)


def _inproj(x3, mod, g, w, nb, tt):
    nbat, t, d = x3.shape
    tper = t // tt
    tm = nb * tt
    n_m = (nbat // nb) * tper
    return pl.pallas_call(
        _inproj_kernel,
        grid=(n_m, PROJ_W // PROJ_TN),
        in_specs=[pl.BlockSpec((nb, tt, d), lambda i, j: (i // tper, i % tper, 0)),
                  pl.BlockSpec((nb, 6, d), lambda i, j: (i // tper, 0, 0)),
                  _const_spec((1, d)),
                  pl.BlockSpec((d, PROJ_TN), lambda i, j: (0, j))],
        out_specs=pl.BlockSpec((tm, PROJ_TN), lambda i, j: (i, j)),
        out_shape=jax.ShapeDtypeStruct((nbat * t, PROJ_W), F32),
        scratch_shapes=[pltpu.VMEM((tm, d), BF16)],
        compiler_params=_cparams(2, 40),
        name="in_proj",
    )(x3, mod, g, w)


def _causal_conv(u3, c8, w, k_w):
    nb, tt, c = u3.shape
    rows = lax.broadcasted_iota(jnp.int32, (nb, SUBLANE, c), 1)
    acc = None
    for j in range(k_w):
        s = k_w - 1 - j
        if s == 0:
            sh = u3
        else:
            full = pltpu.roll(u3, s, axis=1)
            top = jnp.where(rows < s, pltpu.roll(c8, s, axis=1), full[:, 0:SUBLANE, :])
            sh = top if tt == SUBLANE else jnp.concatenate([top, full[:, SUBLANE:, :]], axis=1)
        term = sh * w[j:j + 1, :]
        acc = term if acc is None else acc + term
    return acc


def _carry_in(t_idx, st_ref, c_scr):
    @pl.when(t_idx == 0)
    def _():
        c_scr[...] = st_ref[...]

    return c_scr[...]


def _sconv_kernel(bg_ref, cg_ref, xt_ref, st_ref, w_ref, y_ref, sto_ref, c_scr, *, nb, tt):
    c = bg_ref.shape[-1]
    c8 = _carry_in(pl.program_id(1), st_ref, c_scr)
    u3 = (cg_ref[...] * xt_ref[...]).reshape(nb, tt, c)
    conv = _causal_conv(u3, c8, w_ref[...], SC_K)
    y_ref[...] = bg_ref[...] * conv.reshape(nb * tt, c)
    last = u3[:, tt - SUBLANE:, :]
    c_scr[...] = last
    sto_ref[...] = last


def _tok_spec(tm, width, col_block, tper):
    return pl.BlockSpec((tm, width), lambda b, t: (b * tper + t, col_block))


def _state_spec(nb, c):
    return pl.BlockSpec((nb, SUBLANE, c), lambda b, t: (b, 0, 0))


def _sconv(proj, st8, w, nbat, t, nb, tt):
    tper = t // tt
    tm = nb * tt
    c = BRANCH_W
    kern = functools.partial(_sconv_kernel, nb=nb, tt=tt)
    return pl.pallas_call(
        kern,
        grid=(nbat // nb, tper),
        in_specs=[_tok_spec(tm, c, OFF_SC // c, tper), _tok_spec(tm, c, OFF_SC // c + 1, tper),
                  _tok_spec(tm, c, OFF_SC // c + 2, tper), _state_spec(nb, c), _const_spec((SC_K, c))],
        out_specs=[_tok_spec(tm, c, 0, tper), _state_spec(nb, c)],
        out_shape=[jax.ShapeDtypeStruct((nbat * t, c), F32), jax.ShapeDtypeStruct((nbat, SUBLANE, c), F32)],
        scratch_shapes=[pltpu.VMEM((nb, SUBLANE, c), F32)],
        compiler_params=_cparams(2, 40),
        name="short_conv",
    )(proj, proj, proj, st8, w)


def _gdn_pre_kernel(qkv_ref, ga_ref, gb_ref, st_ref, w_ref, alog_ref, dtb_ref,
                    o_ref, g_ref, beta_ref, sto_ref, c_scr, *, nb, tt):
    c = qkv_ref.shape[-1]
    c8 = _carry_in(pl.program_id(1), st_ref, c_scr)
    u3 = qkv_ref[...].reshape(nb, tt, c)
    o_ref[...] = _silu(_causal_conv(u3, c8, w_ref[...], GDN_K)).reshape(nb * tt, c)
    last = u3[:, tt - SUBLANE:, :]
    c_scr[...] = last
    sto_ref[...] = last
    g_ref[...] = -jnp.exp(alog_ref[...]) * _softplus(ga_ref[...] + dtb_ref[...])
    beta_ref[...] = jax.nn.sigmoid(gb_ref[...])


def _gdn_pre(proj, st8, w, alog_b, dtb_b, nbat, t, nb, tt):
    tper = t // tt
    tm = nb * tt
    c3 = 3 * BRANCH_W
    c = BRANCH_W
    kern = functools.partial(_gdn_pre_kernel, nb=nb, tt=tt)
    return pl.pallas_call(
        kern,
        grid=(nbat // nb, tper),
        in_specs=[_tok_spec(tm, c3, OFF_GQKV // c3, tper), _tok_spec(tm, c, OFF_GA // c, tper),
                  _tok_spec(tm, c, OFF_GB // c, tper), _state_spec(nb, c3), _const_spec((GDN_K, c3)),
                  _const_spec((1, c)), _const_spec((1, c))],
        out_specs=[_tok_spec(tm, c3, 0, tper), _tok_spec(tm, c, 0, tper), _tok_spec(tm, c, 0, tper),
                   _state_spec(nb, c3)],
        out_shape=[jax.ShapeDtypeStruct((nbat * t, c3), F32), jax.ShapeDtypeStruct((nbat * t, c), F32),
                   jax.ShapeDtypeStruct((nbat * t, c), F32), jax.ShapeDtypeStruct((nbat, SUBLANE, c3), F32)],
        scratch_shapes=[pltpu.VMEM((nb, SUBLANE, c3), F32)],
        compiler_params=_cparams(2, 48),
        name="gdn_pre",
    )(proj, proj, proj, st8, w, alog_b, dtb_b)


def _lru_pre_kernel(lx_ref, st_ref, w_ref, cb_ref, wg_ref, bg_ref, nsl_ref,
                    a_ref, bx_ref, sto_ref, c_scr, *, nb, tt, pos0):
    c = lx_ref.shape[-1]
    t_idx = pl.program_id(1)
    c8 = _carry_in(t_idx, st_ref, c_scr)
    u3 = lx_ref[...].reshape(nb, tt, c)
    u = (_causal_conv(u3, c8, w_ref[...], LRU_K) + cb_ref[...]).reshape(nb * tt, c)
    last = u3[:, tt - SUBLANE:, :]
    c_scr[...] = last
    sto_ref[...] = last
    gates = jax.nn.sigmoid(_bdot(u, wg_ref[...]) + bg_ref[...])
    r = gates[:, :c]
    ig = gates[:, c:]
    log_a = -LRU_C * r * _softplus(-nsl_ref[...])
    a = jnp.exp(log_a)
    mult = jnp.sqrt(1.0 - jnp.exp(2.0 * log_a))
    pos = pos0 + t_idx * tt + lax.broadcasted_iota(jnp.int32, (nb, tt, c), 1).reshape(nb * tt, c)
    mult = jnp.where(pos == 0, 1.0, mult)
    a_ref[...] = a
    bx_ref[...] = mult * (ig * u)


def _lru_pre(proj, st8, w, cb, wg, bg, nsl, nbat, t, nb, tt, pos0):
    tper = t // tt
    tm = nb * tt
    c = BRANCH_W
    kern = functools.partial(_lru_pre_kernel, nb=nb, tt=tt, pos0=pos0)
    return pl.pallas_call(
        kern,
        grid=(nbat // nb, tper),
        in_specs=[_tok_spec(tm, c, OFF_LX // c, tper), _state_spec(nb, c), _const_spec((LRU_K, c)),
                  _const_spec((1, c)), _const_spec((c, 2 * c)), _const_spec((1, 2 * c)), _const_spec((1, c))],
        out_specs=[_tok_spec(tm, c, 0, tper), _tok_spec(tm, c, 0, tper), _state_spec(nb, c)],
        out_shape=[jax.ShapeDtypeStruct((nbat * t, c), F32), jax.ShapeDtypeStruct((nbat * t, c), F32),
                   jax.ShapeDtypeStruct((nbat, SUBLANE, c), F32)],
        scratch_shapes=[pltpu.VMEM((nb, SUBLANE, c), F32)],
        compiler_params=_cparams(2, 40),
        name="lru_pre",
    )(proj, st8, w, cb, wg, bg, nsl)


def _lru_scan_kernel(a_ref, bx_ref, lg_ref, h0_ref, y_ref, hl_ref, h_scr, *, nb, tt):
    c = a_ref.shape[-1]

    @pl.when(pl.program_id(1) == 0)
    def _():
        h_scr[...] = h0_ref[...]

    a = a_ref[...].reshape(nb, tt, c)
    b = bx_ref[...].reshape(nb, tt, c)
    rows = lax.broadcasted_iota(jnp.int32, (nb, tt, c), 1)
    d = 1
    while d < tt:
        keep = rows >= d
        a_s = jnp.where(keep, pltpu.roll(a, d, axis=1), 1.0)
        b_s = jnp.where(keep, pltpu.roll(b, d, axis=1), 0.0)
        b = a * b_s + b
        a = a * a_s
        d *= 2
    hs = b + a * h_scr[...]
    last = hs[:, tt - 1:tt, :]
    h_scr[...] = last
    hl_ref[...] = last
    y_ref[...] = hs.reshape(nb * tt, c) * _gelu_tanh(lg_ref[...])


def _lru_scan(a, bx, proj, h0, nbat, t, nb, tt):
    tper = t // tt
    tm = nb * tt
    c = BRANCH_W
    kern = functools.partial(_lru_scan_kernel, nb=nb, tt=tt)
    h_spec = pl.BlockSpec((nb, 1, c), lambda b, t_: (b, 0, 0))
    return pl.pallas_call(
        kern,
        grid=(nbat // nb, tper),
        in_specs=[_tok_spec(tm, c, 0, tper), _tok_spec(tm, c, 0, tper), _tok_spec(tm, c, OFF_LG // c, tper), h_spec],
        out_specs=[_tok_spec(tm, c, 0, tper), h_spec],
        out_shape=[jax.ShapeDtypeStruct((nbat * t, c), F32), jax.ShapeDtypeStruct((nbat, 1, c), F32)],
        scratch_shapes=[pltpu.VMEM((nb, 1, c), F32)],
        compiler_params=_cparams(2, 40),
        name="lru_scan",
    )(a, bx, proj, h0)


def _mla_pre_kernel(qa_ref, ckv_ref, kpa_ref, kpb_ref, cos_ref, sin_ref, gq_ref, gkv_ref,
                    wn_ref, wpa_ref, wpb_ref, bdk_ref, q_out, ckv_out, kpe_out, kf_out, *, nb, tt):
    cos = cos_ref[...]
    sin = sin_ref[...]
    cq = _rms(qa_ref[...], gq_ref[...]).astype(BF16)
    qn = jnp.dot(cq, wn_ref[...], preferred_element_type=F32)
    qabs = _bdot(qn, bdk_ref[...])
    pa = jnp.dot(cq, wpa_ref[...], preferred_element_type=F32)
    pb = jnp.dot(cq, wpb_ref[...], preferred_element_type=F32)
    for h in range(N_HEADS):
        sl = slice(h * LANE, (h + 1) * LANE)
        q_pe = pa[:, sl] * cos + pb[:, sl] * sin
        q_out[:, h, :, 0:LANE] = qabs[:, sl].reshape(nb, tt, LANE).astype(q_out.dtype)
        q_out[:, h, :, LANE:QK_W] = q_pe.reshape(nb, tt, LANE).astype(q_out.dtype)
    ckv = _rms(ckv_ref[...], gkv_ref[...])
    kpe = kpa_ref[...] * cos + kpb_ref[...] * sin
    ckv_out[...] = ckv
    kpe_out[...] = kpe[:, :MLA_ROPE]
    kf_out[...] = jnp.concatenate([ckv, kpe], axis=-1).astype(BF16)


def _mla_pre(proj, cos, sin, gq, gkv, wn, wpa, wpb, bdk, nbat, t, nb, tt, q_dtype):
    tper = t // tt
    tm = nb * tt
    hw = N_HEADS * LANE
    kern = functools.partial(_mla_pre_kernel, nb=nb, tt=tt)
    tab_spec = pl.BlockSpec((tm, LANE), lambda b, t_: (t_, 0))
    return pl.pallas_call(
        kern,
        grid=(nbat // nb, tper),
        in_specs=[_tok_spec(tm, MLA_QLORA, OFF_QA // MLA_QLORA, tper), _tok_spec(tm, LANE, OFF_CKV // LANE, tper),
                  _tok_spec(tm, LANE, OFF_KPA // LANE, tper), _tok_spec(tm, LANE, OFF_KPB // LANE, tper),
                  tab_spec, tab_spec, _const_spec((1, MLA_QLORA)), _const_spec((1, MLA_KVLORA)),
                  _const_spec((MLA_QLORA, BRANCH_W)), _const_spec((MLA_QLORA, hw)), _const_spec((MLA_QLORA, hw)),
                  _const_spec((BRANCH_W, hw))],
        out_specs=[pl.BlockSpec((nb, N_HEADS, tt, QK_W), lambda b, t_: (b, 0, t_, 0)),
                   _tok_spec(tm, MLA_KVLORA, 0, tper), _tok_spec(tm, MLA_ROPE, 0, tper), _tok_spec(tm, QK_W, 0, tper)],
        out_shape=[jax.ShapeDtypeStruct((nbat, N_HEADS, t, QK_W), q_dtype),
                   jax.ShapeDtypeStruct((nbat * t, MLA_KVLORA), F32),
                   jax.ShapeDtypeStruct((nbat * t, MLA_ROPE), F32),
                   jax.ShapeDtypeStruct((nbat * t, QK_W), BF16)],
        compiler_params=_cparams(2, 48),
        name="mla_pre",
    )(proj, proj, proj, proj, cos, sin, gq, gkv, wn, wpa, wpb, bdk)


def _softmax_step(carry, s, v_bf16):
    m, l, acc = carry
    m_new = jnp.maximum(m, jnp.max(s, axis=-1, keepdims=True))
    alpha = jnp.exp(m - m_new)
    p = jnp.exp(s - m_new)
    l = alpha * l + jnp.sum(p, axis=-1, keepdims=True)
    acc = alpha * acc + jnp.dot(p.astype(BF16), v_bf16, preferred_element_type=F32)
    return m_new, l, acc


def _heads_to_lanes(o, rows_per_head):
    return jnp.concatenate([o[h * rows_per_head:(h + 1) * rows_per_head] for h in range(N_HEADS)], axis=-1)


def _attn_prompt_kernel(q_ref, kf_ref, bdv_ref, y_ref, *, tq, scale):
    i = pl.program_id(1)
    r = N_HEADS * tq
    q = q_ref[0].reshape(r, QK_W)

    def scores(j):
        kblk = kf_ref[pl.ds(pl.multiple_of(j * tq, tq), tq), :]
        s = lax.dot_general(q, kblk, (((1,), (1,)), ((), ())), preferred_element_type=F32) * scale
        return s, kblk[:, :MLA_KVLORA]

    def body(j, carry):
        s, v = scores(j)
        return _softmax_step(carry, s, v)

    init = (jnp.full((r, 1), NEG_BIG, F32), jnp.zeros((r, 1), F32), jnp.zeros((r, MLA_KVLORA), F32))
    carry = lax.fori_loop(0, i, body, init)
    s, v = scores(i)
    qpos = lax.broadcasted_iota(jnp.int32, (r, tq), 0) & (tq - 1)
    kpos = lax.broadcasted_iota(jnp.int32, (r, tq), 1)
    s = jnp.where(kpos <= qpos, s, NEG_BIG)
    _, l, acc = _softmax_step(carry, s, v)
    o = _heads_to_lanes(acc / l, tq)
    y_ref[...] = _bdot(o, bdv_ref[...])


def _attn_prompt(q_full, kf, bdv, nbat, t, tq):
    tper = t // tq
    scale = (HEAD_DIM + MLA_ROPE) ** -0.5
    kern = functools.partial(_attn_prompt_kernel, tq=tq, scale=scale)
    return pl.pallas_call(
        kern,
        grid=(nbat, tper),
        in_specs=[pl.BlockSpec((1, N_HEADS, tq, QK_W), lambda b, i: (b, 0, i, 0)),
                  pl.BlockSpec((t, QK_W), lambda b, i: (b, 0)),
                  _const_spec((N_HEADS * LANE, BRANCH_W))],
        out_specs=pl.BlockSpec((tq, BRANCH_W), lambda b, i: (b * tper + i, 0)),
        out_shape=jax.ShapeDtypeStruct((nbat * t, BRANCH_W), F32),
        compiler_params=_cparams(2, 40),
        name="attn_prompt",
    )(q_full, kf, bdv)


def _attn_sample_kernel(pt_ref, q_ref, *refs, pg, ts, page, scale):
    ckv_pages = refs[:pg]
    kpe_pages = refs[pg:2 * pg]
    ckvn_ref, kpen_ref, bdv_ref, y_ref, m_scr, l_scr, acc_scr = refs[2 * pg:]
    j = pl.program_id(1)
    r = N_HEADS * ts

    @pl.when(j == 0)
    def _():
        m_scr[...] = jnp.full(m_scr.shape, NEG_BIG, F32)
        l_scr[...] = jnp.zeros(l_scr.shape, F32)
        acc_scr[...] = jnp.zeros(acc_scr.shape, F32)

    q = q_ref[0].reshape(r, QK_W)
    q_lat = q[:, :MLA_KVLORA].astype(BF16)
    q_pe = q[:, MLA_KVLORA:MLA_KVLORA + MLA_ROPE].astype(BF16)

    def scores(c_bf16, kp_bf16):
        return (lax.dot_general(q_lat, c_bf16, (((1,), (1,)), ((), ())), preferred_element_type=F32)
                + lax.dot_general(q_pe, kp_bf16, (((1,), (1,)), ((), ())), preferred_element_type=F32)) * scale

    cs = [ckv_pages[i][0, 0].astype(BF16) for i in range(pg)]
    s = jnp.concatenate([scores(cs[i], kpe_pages[i][0, 0].astype(BF16)) for i in range(pg)], axis=-1)
    m = m_scr[...]
    m_new = jnp.maximum(m, jnp.max(s, axis=-1, keepdims=True))
    alpha = jnp.exp(m - m_new)
    p32 = jnp.exp(s - m_new)
    l_new = alpha * l_scr[...] + jnp.sum(p32, axis=-1, keepdims=True)
    p = p32.astype(BF16)
    pv = None
    for i in range(pg):
        term = jnp.dot(p[:, i * page:(i + 1) * page], cs[i], preferred_element_type=F32)
        pv = term if pv is None else pv + term
    acc_new = alpha * acc_scr[...] + pv
    m_scr[...] = m_new
    l_scr[...] = l_new
    acc_scr[...] = acc_new

    @pl.when(j == pl.num_programs(1) - 1)
    def _():
        pad = jnp.zeros((page - ts, MLA_KVLORA), F32)
        cn = jnp.concatenate([ckvn_ref[...], pad], axis=0).astype(BF16)
        kn = jnp.concatenate([kpen_ref[...], pad[:, :MLA_ROPE]], axis=0).astype(BF16)
        sn = scores(cn, kn)
        qpos = lax.broadcasted_iota(jnp.int32, (r, page), 0) & (ts - 1)
        kpos = lax.broadcasted_iota(jnp.int32, (r, page), 1)
        sn = jnp.where(kpos <= qpos, sn, NEG_BIG)
        m2 = jnp.maximum(m_new, jnp.max(sn, axis=-1, keepdims=True))
        a2 = jnp.exp(m_new - m2)
        p2 = jnp.exp(sn - m2)
        l2 = a2 * l_new + jnp.sum(p2, axis=-1, keepdims=True)
        acc2 = a2 * acc_new + jnp.dot(p2.astype(BF16), cn, preferred_element_type=F32)
        o = _heads_to_lanes(acc2 / l2, ts)
        y_ref[...] = _bdot(o, bdv_ref[...])


def _attn_sample(q_full, cache_ckv, cache_kpe, page_table, layer, ckv_new, kpe_new, bdv, nbat, ts):
    n_pages = page_table.shape[1]
    page = cache_ckv.shape[2]
    pg = math.gcd(n_pages, 16)
    scale = (HEAD_DIM + MLA_ROPE) ** -0.5
    r = N_HEADS * ts
    kern = functools.partial(_attn_sample_kernel, pg=pg, ts=ts, page=page, scale=scale)

    def page_spec(i, width):
        return pl.BlockSpec((1, 1, page, width), lambda b, j, pt: (layer, pt[b * n_pages + j * pg + i], 0, 0))

    in_specs = ([pl.BlockSpec((1, N_HEADS, ts, QK_W), lambda b, j, pt: (b, 0, 0, 0))]
                + [page_spec(i, MLA_KVLORA) for i in range(pg)]
                + [page_spec(i, MLA_ROPE) for i in range(pg)]
                + [pl.BlockSpec((ts, MLA_KVLORA), lambda b, j, pt: (b, 0)),
                   pl.BlockSpec((ts, MLA_ROPE), lambda b, j, pt: (b, 0)),
                   pl.BlockSpec((N_HEADS * LANE, BRANCH_W), lambda b, j, pt: (0, 0))])
    grid_spec = pltpu.PrefetchScalarGridSpec(
        num_scalar_prefetch=1,
        grid=(nbat, n_pages // pg),
        in_specs=in_specs,
        out_specs=pl.BlockSpec((ts, BRANCH_W), lambda b, j, pt: (b, 0)),
        scratch_shapes=[pltpu.VMEM((r, 1), F32), pltpu.VMEM((r, 1), F32), pltpu.VMEM((r, MLA_KVLORA), F32)],
    )
    return pl.pallas_call(
        kern,
        grid_spec=grid_spec,
        out_shape=jax.ShapeDtypeStruct((nbat * ts, BRANCH_W), F32),
        compiler_params=_cparams(2, 40),
        name="attn_sample",
    )(page_table.reshape(-1), q_full, *([cache_ckv] * pg), *([cache_kpe] * pg), ckv_new, kpe_new, bdv)


def _cumsum_rows(x, n):
    rows = lax.broadcasted_iota(jnp.int32, x.shape, 0)
    d = 1
    while d < n:
        x = x + jnp.where(rows >= d, pltpu.roll(x, d, axis=0), 0.0)
        d *= 2
    return x


def _inverse_masks(n):
    rows = lax.broadcasted_iota(jnp.int32, (n, n), 0)
    cols = lax.broadcasted_iota(jnp.int32, (n, n), 1)
    masks = [rows == cols, (rows >> 1) == (cols >> 1)]
    k = 1
    while (1 << k) < n:
        masks.append(((rows >> k) ^ (cols >> k)) == 1)
        k += 1
    return masks


def _unit_lower_inverse(a_mat, masks):
    t = jnp.where(masks[0], 1.0, 0.0) - jnp.where(masks[1], a_mat, 0.0)
    for m in masks[2:]:
        t = t - _bdot(_bdot(t, jnp.where(m, a_mat, 0.0)), t)
    return t


def _gdn_head_chunk(q, k, v, gc, beta, s_mat, tril, strict, masks):
    q = q * lax.rsqrt(jnp.sum(q * q, axis=-1, keepdims=True) + EPS) * (HEAD_DIM ** -0.5)
    k = k * lax.rsqrt(jnp.sum(k * k, axis=-1, keepdims=True) + EPS)
    diff = gc - gc.T
    decay = jnp.where(tril, jnp.exp(jnp.where(tril, diff, 0.0)), 0.0)
    kb = k * beta
    a_mat = jnp.where(strict, _bdot_nt(kb, k) * decay, 0.0)
    egc = jnp.exp(gc)
    x = jnp.concatenate([v * beta, kb * egc], axis=-1)
    x = _bdot(_unit_lower_inverse(a_mat, masks), x)
    u = x[:, :HEAD_DIM]
    w = x[:, HEAD_DIM:]
    intra = jnp.where(tril, _bdot_nt(q, k) * decay, 0.0)
    v_new = u - _bdot(w, s_mat)
    o = _bdot(q * egc, s_mat) + _bdot(intra, v_new)
    g_last = gc[GDN_CHUNK - 1:GDN_CHUNK, :]
    s_new = s_mat * jnp.exp(g_last) + _bdot_tn(k * jnp.exp(g_last - gc), v_new)
    return o, s_new


def _gdn_kernel(q_ref, k_ref, v_ref, g_ref, beta_ref, z_ref, s0_ref, gn_ref, y_ref, sf_ref, s_scr, *, n_chunks):
    @pl.when(pl.program_id(1) == 0)
    def _():
        s_scr[...] = s0_ref[0]

    cc = GDN_CHUNK
    rows = lax.broadcasted_iota(jnp.int32, (cc, cc), 0)
    cols = lax.broadcasted_iota(jnp.int32, (cc, cc), 1)
    tril = rows >= cols
    strict = rows > cols
    masks = _inverse_masks(cc)
    gn = gn_ref[...]
    for ci in range(n_chunks):
        rs = slice(ci * cc, (ci + 1) * cc)
        gc_all = _cumsum_rows(g_ref[rs, :], cc)
        outs = []
        for h in range(N_HEADS):
            hs = slice(h * HEAD_DIM, (h + 1) * HEAD_DIM)
            o, s_new = _gdn_head_chunk(q_ref[rs, hs], k_ref[rs, hs], v_ref[rs, hs], gc_all[:, hs],
                                       beta_ref[rs, hs], s_scr[h], tril, strict, masks)
            s_scr[h] = s_new
            o = _rms(o, gn[:, hs]) * _silu(z_ref[rs, hs])
            outs.append(o)
        y_ref[rs, :] = jnp.concatenate(outs, axis=-1)
    sf_ref[0] = s_scr[...]


def _gdn(qkvc, g_b, beta_b, proj, s0, gn_b, nbat, t, tc):
    tper = t // tc
    c = BRANCH_W
    kern = functools.partial(_gdn_kernel, n_chunks=tc // GDN_CHUNK)
    s_spec = pl.BlockSpec((1, N_HEADS, HEAD_DIM, HEAD_DIM), lambda b, t_: (b, 0, 0, 0))
    return pl.pallas_call(
        kern,
        grid=(nbat, tper),
        in_specs=[_tok_spec(tc, c, 0, tper), _tok_spec(tc, c, 1, tper), _tok_spec(tc, c, 2, tper),
                  _tok_spec(tc, c, 0, tper), _tok_spec(tc, c, 0, tper), _tok_spec(tc, c, OFF_GZ // c, tper),
                  s_spec, _const_spec((1, c))],
        out_specs=[_tok_spec(tc, c, 0, tper), s_spec],
        out_shape=[jax.ShapeDtypeStruct((nbat * t, c), F32),
                   jax.ShapeDtypeStruct((nbat, N_HEADS, HEAD_DIM, HEAD_DIM), F32)],
        scratch_shapes=[pltpu.VMEM((N_HEADS, HEAD_DIM, HEAD_DIM), F32)],
        compiler_params=_cparams(2, 40),
        name="gdn_rule",
    )(qkvc, qkvc, qkvc, g_b, beta_b, proj, s0, gn_b)


def _merge_kernel(x_ref, mod_ref, g_ref, ya_ref, yb_ref, yc_ref, yd_ref, wg_ref, wbo_ref, wmix_ref, o_ref):
    x = x_ref[...]
    nb, tt, d = x.shape
    mod = mod_ref[...]
    h = _modulated_norm(x, g_ref[...], mod, 0, 1).reshape(nb * tt, d).astype(BF16)
    acc = None
    for n, y_ref in enumerate((ya_ref, yb_ref, yc_ref, yd_ref)):
        gate = jax.nn.sigmoid(jnp.dot(h, wg_ref[:, n * d:(n + 1) * d], preferred_element_type=F32))
        term = gate * _bdot(y_ref[...], wbo_ref[n])
        acc = term if acc is None else acc + term
    mix = _bdot(acc, wmix_ref[...])
    o_ref[...] = x + mod[:, 2:3, :] * mix.reshape(nb, tt, d)


def _merge(x3, mod, g, ys, wg, wbo, wmix, nb, tt):
    nbat, t, d = x3.shape
    tper = t // tt
    tm = nb * tt
    c = BRANCH_W
    x_spec = pl.BlockSpec((nb, tt, d), lambda b, t_: (b, t_, 0))
    return pl.pallas_call(
        _merge_kernel,
        grid=(nbat // nb, tper),
        in_specs=[x_spec, pl.BlockSpec((nb, 6, d), lambda b, t_: (b, 0, 0)), _const_spec((1, d))]
                 + [_tok_spec(tm, c, 0, tper)] * 4
                 + [_const_spec(wg.shape), _const_spec(wbo.shape), _const_spec(wmix.shape)],
        out_specs=x_spec,
        out_shape=jax.ShapeDtypeStruct(x3.shape, F32),
        compiler_params=_cparams(2, 48),
        name="merge",
    )(x3, mod, g, *ys, wg, wbo, wmix)


def _ffn_kernel(x_ref, mod_ref, g_ref, win_ref, wout_ref, gf_ref, o_ref, *, final):
    x = x_ref[...]
    nb, tt, d = x.shape
    mod = mod_ref[...]
    fh = wout_ref.shape[0]
    h = _modulated_norm(x, g_ref[...], mod, 3, 4).reshape(nb * tt, d).astype(BF16)
    gate = jnp.dot(h, win_ref[:, :fh], preferred_element_type=F32)
    up = jnp.dot(h, win_ref[:, fh:], preferred_element_type=F32)
    out = _bdot(_silu(gate) * up, wout_ref[...])
    xo = x + mod[:, 5:6, :] * out.reshape(nb, tt, d)
    o_ref[...] = _rms(xo, gf_ref[...]) if final else xo


def _ffn(x3, mod, g, win, wout, gf, nb, tt, final):
    nbat, t, d = x3.shape
    x_spec = pl.BlockSpec((nb, tt, d), lambda b, t_: (b, t_, 0))
    kern = functools.partial(_ffn_kernel, final=final)
    return pl.pallas_call(
        kern,
        grid=(nbat // nb, t // tt),
        in_specs=[x_spec, pl.BlockSpec((nb, 6, d), lambda b, t_: (b, 0, 0)), _const_spec((1, d)),
                  _const_spec(win.shape), _const_spec(wout.shape), _const_spec((1, d))],
        out_specs=x_spec,
        out_shape=jax.ShapeDtypeStruct(x3.shape, F32),
        compiler_params=_cparams(2, 56),
        name="ffn",
    )(x3, mod, g, win, wout, gf)


def _block_diag(blocks):
    h, r, c = blocks.shape
    eye = jnp.eye(h, dtype=blocks.dtype)
    return jnp.einsum('hrc,hg->hrgc', blocks, eye).reshape(h * r, h * c)


def _prep_layer_weights(w_in, w_qb, w_kvb, w_lru_gate_a, w_lru_gate_x):
    d = w_in.shape[0]
    c = BRANCH_W
    o = 0
    segs = {}
    for name, width in (('sc', 3 * c), ('qa', MLA_QLORA), ('ckv', MLA_KVLORA), ('kpe', MLA_ROPE), ('gqkv', 3 * c),
                        ('gz', c), ('ga', N_HEADS), ('gb', N_HEADS), ('lx', c), ('lg', c)):
        segs[name] = w_in[:, o:o + width]
        o += width
    half = MLA_ROPE // 2
    zpad = jnp.zeros((d, LANE - MLA_ROPE), F32)
    kpe = segs['kpe']
    kpa = jnp.concatenate([kpe, zpad], axis=1)
    kpb = jnp.concatenate([-kpe[:, half:], kpe[:, :half], zpad], axis=1)
    w_proj = jnp.concatenate(
        [segs['sc'], segs['gqkv'], segs['gz'], jnp.repeat(segs['ga'], HEAD_DIM, axis=1),
         jnp.repeat(segs['gb'], HEAD_DIM, axis=1), segs['lx'], segs['lg'], segs['qa'], segs['ckv'], kpa, kpb],
        axis=1).astype(BF16)

    ql = w_qb.shape[0]
    wq = w_qb.reshape(ql, N_HEADS, HEAD_DIM + MLA_ROPE)
    wn = wq[:, :, :HEAD_DIM].reshape(ql, c)
    pe = wq[:, :, HEAD_DIM:]
    zq = jnp.zeros((ql, N_HEADS, LANE - MLA_ROPE), F32)
    wpa = jnp.concatenate([pe, zq], axis=2).reshape(ql, N_HEADS * LANE)
    wpb = jnp.concatenate([-pe[:, :, half:], pe[:, :, :half], zq], axis=2).reshape(ql, N_HEADS * LANE)
    w_kb = w_kvb[:, :, :HEAD_DIM]
    w_vb = w_kvb[:, :, HEAD_DIM:]
    bdk = _block_diag(jnp.transpose(w_kb, (1, 2, 0)))
    bdv = _block_diag(jnp.transpose(w_vb, (1, 0, 2)))
    wlg = jnp.concatenate([_block_diag(w_lru_gate_a), _block_diag(w_lru_gate_x)], axis=1)
    return dict(w_proj=w_proj, wn=wn.astype(BF16), wpa=wpa.astype(BF16), wpb=wpb.astype(BF16),
                bdk=bdk.astype(BF16), bdv=bdv.astype(BF16), wlg=wlg.astype(BF16))


def _rope_tables(pos):
    half = MLA_ROPE // 2
    inv = ROPE_THETA ** (-jnp.arange(half, dtype=F32) / half)
    ang = pos.astype(F32)[:, None] * inv[None, :]
    pad = jnp.zeros((pos.shape[0], LANE - MLA_ROPE), F32)
    cos = jnp.concatenate([jnp.cos(ang), jnp.cos(ang), pad], axis=1)
    sin = jnp.concatenate([jnp.sin(ang), jnp.sin(ang), pad], axis=1)
    return cos, sin


def _pad_state(st, k_w):
    return jnp.pad(st, ((0, 0), (SUBLANE - (k_w - 1), 0), (0, 0)))


def _tile(t, pref):
    tt = min(t, pref)
    while t % tt:
        tt //= 2
    return tt


def _group_layer(x3, mod, lw, p, st, rope, attend, cfg):
    nbat, t, d = x3.shape
    nb, tt = cfg['nb'], cfg['tt']
    proj = _inproj(x3, mod, p['g_norm_mix'], lw['w_proj'], cfg['nb_proj'], cfg['tt_proj'])
    y_a, sc8 = _sconv(proj, st['sconv'], p['w_sc_conv'], nbat, t, nb, tt)
    q_full, ckv, kpe, kf = _mla_pre(proj, rope[0], rope[1], p['g_q_norm'], p['g_kv_norm'], lw['wn'], lw['wpa'],
                                    lw['wpb'], lw['bdk'], nbat, t, nb, tt, cfg['q_dtype'])
    y_b = attend(q_full, ckv, kpe, kf, lw['bdv'])
    qkvc, g_b, beta_b, gc8 = _gdn_pre(proj, st['gdn_conv'], p['w_gdn_conv'], p['alog_b'], p['dtb_b'], nbat, t, nb, tt)
    if cfg['gdn_pad']:
        padn = GDN_CHUNK - t

        def padt(a):
            return jnp.pad(a.reshape(nbat, t, -1), ((0, 0), (0, padn), (0, 0))).reshape(nbat * GDN_CHUNK, -1)

        y_c, s_gdn = _gdn(padt(qkvc), padt(g_b), padt(beta_b), padt(proj[:, :OFF_GA]), st['gdn'], p['gn_b'],
                          nbat, GDN_CHUNK, GDN_CHUNK)
        y_c = y_c.reshape(nbat, GDN_CHUNK, -1)[:, :t].reshape(nbat * t, -1)
    else:
        y_c, s_gdn = _gdn(qkvc, g_b, beta_b, proj, st['gdn'], p['gn_b'], nbat, t, cfg['tc'])
    a, bx, lc8 = _lru_pre(proj, st['lru_conv'], p['w_lru_conv'], p['b_lru_conv'], lw['wlg'], p['b_lru_gates'],
                          p['nsl'], nbat, t, nb, tt, cfg['pos0'])
    y_d, h_last = _lru_scan(a, bx, proj, st['lru'], nbat, t, cfg['nb_scan'], cfg['tt_scan'])
    x1 = _merge(x3, mod, p['g_norm_mix'], (y_a, y_b, y_c, y_d), p['w_merge_gate'], p['w_branch_out'],
                p['w_mix_out'], cfg['nb_mm'], cfg['tt_mm'])
    x2 = _ffn(x1, mod, p['g_norm_ffn'], p['w_ffn_in'], p['w_ffn_out'], p['g_final'], cfg['nb_mm'], cfg['tt_mm'],
              cfg['final'])
    new_st = dict(ckv=ckv.reshape(nbat, t, -1), kpe=kpe.reshape(nbat, t, -1),
                  sconv=sc8[:, SUBLANE - (SC_K - 1):], gdn_conv=gc8[:, SUBLANE - (GDN_K - 1):], gdn=s_gdn,
                  lru_conv=lc8[:, SUBLANE - (LRU_K - 1):], lru=h_last[:, 0, :])
    return x2, new_st


STATE_KEYS = ('ckv', 'kpe', 'sconv', 'gdn_conv', 'gdn', 'lru_conv', 'lru')


def kernel(x_prompt, x_sample, c_prompt, c_sample, cache_mla_ckv, cache_mla_kpe, page_table, state_sconv, state_gdn_conv, state_gdn, state_lru_conv, state_lru, w_ada, b_ada, g_norm_mix, g_norm_ffn, w_in, w_sc_conv, g_q_norm, w_qb, g_kv_norm, w_kvb, w_gdn_conv, gdn_a_log, gdn_dt_bias, g_gdn_norm, w_lru_conv, b_lru_conv, w_lru_gate_a, b_lru_gate_a, w_lru_gate_x, b_lru_gate_x, lru_lambda, w_branch_out, w_merge_gate, w_mix_out, w_ffn_in, w_ffn_out, g_final):
    bp, tp, d = x_prompt.shape
    bs, ts, _ = x_sample.shape
    depth = w_in.shape[0]
    n_pages = page_table.shape[1]
    past_len = n_pages * cache_mla_ckv.shape[2]
    c = BRANCH_W
    assert ts == SUBLANE and tp % GDN_CHUNK == 0 and d % LANE == 0

    mod_all = _ada(jnp.concatenate([c_prompt, c_sample], axis=0), w_ada, b_ada).reshape(depth, bp + bs, 6, d)

    nb_s = _tile(bs, 32)
    cos_p, sin_p = _rope_tables(jnp.arange(tp, dtype=jnp.int32))
    cos_s, sin_s = _rope_tables(past_len + jnp.arange(ts, dtype=jnp.int32))
    rope_p = (cos_p, sin_p)
    rope_s = (jnp.tile(cos_s, (nb_s, 1)), jnp.tile(sin_s, (nb_s, 1)))

    tq = _tile(tp, 128)
    cfg_p = dict(nb=1, tt=_tile(tp, 256), nb_proj=1, tt_proj=_tile(tp, 1024), q_dtype=BF16, gdn_pad=False,
                 tc=_tile(tp, 128), pos0=0, nb_scan=1, tt_scan=_tile(tp, 256), nb_mm=1, tt_mm=_tile(tp, 512))
    cfg_s = dict(nb=nb_s, tt=ts, nb_proj=_tile(bs, 128), tt_proj=ts, q_dtype=F32, gdn_pad=True, tc=GDN_CHUNK,
                 pos0=past_len, nb_scan=nb_s, tt_scan=ts, nb_mm=_tile(bs, 64), tt_mm=ts)

    xp, xs = x_prompt, x_sample
    out_p = {k: [] for k in STATE_KEYS}
    out_s = {k: [] for k in STATE_KEYS}
    for l in range(depth):
        lw = _prep_layer_weights(w_in[l], w_qb[l], w_kvb[l], w_lru_gate_a[l], w_lru_gate_x[l])
        p = dict(
            g_norm_mix=g_norm_mix[l][None], g_norm_ffn=g_norm_ffn[l][None], g_final=g_final[None],
            w_sc_conv=w_sc_conv[l], g_q_norm=g_q_norm[l][None], g_kv_norm=g_kv_norm[l][None],
            w_gdn_conv=w_gdn_conv[l],
            alog_b=jnp.repeat(gdn_a_log[l], HEAD_DIM)[None], dtb_b=jnp.repeat(gdn_dt_bias[l], HEAD_DIM)[None],
            gn_b=jnp.tile(g_gdn_norm[l], N_HEADS)[None],
            w_lru_conv=w_lru_conv[l], b_lru_conv=b_lru_conv[l][None],
            b_lru_gates=jnp.concatenate([b_lru_gate_a[l], b_lru_gate_x[l]])[None],
            nsl=lru_lambda[l][None],
            w_merge_gate=w_merge_gate[l].astype(BF16), w_branch_out=w_branch_out[l].astype(BF16),
            w_mix_out=w_mix_out[l].astype(BF16), w_ffn_in=w_ffn_in[l].astype(BF16),
            w_ffn_out=w_ffn_out[l].astype(BF16))
        final = l == depth - 1

        st_p = dict(sconv=jnp.zeros((bp, SUBLANE, c), F32), gdn_conv=jnp.zeros((bp, SUBLANE, 3 * c), F32),
                    gdn=jnp.zeros((bp, N_HEADS, HEAD_DIM, HEAD_DIM), F32),
                    lru_conv=jnp.zeros((bp, SUBLANE, c), F32), lru=jnp.zeros((bp, 1, c), F32))

        def attend_p(q_full, ckv, kpe, kf, bdv):
            return _attn_prompt(q_full, kf, bdv, bp, tp, tq)

        xp, nst_p = _group_layer(xp, mod_all[l, :bp], lw, p, st_p, rope_p, attend_p, dict(cfg_p, final=final))

        st_s = dict(sconv=_pad_state(state_sconv[l], SC_K), gdn_conv=_pad_state(state_gdn_conv[l], GDN_K),
                    gdn=state_gdn[l], lru_conv=_pad_state(state_lru_conv[l], LRU_K), lru=state_lru[l][:, None, :])

        def attend_s(q_full, ckv, kpe, kf, bdv, layer=l):
            return _attn_sample(q_full, cache_mla_ckv, cache_mla_kpe, page_table, layer, ckv, kpe, bdv, bs, ts)

        xs, nst_s = _group_layer(xs, mod_all[l, bp:], lw, p, st_s, rope_s, attend_s, dict(cfg_s, final=final))
        for k in STATE_KEYS:
            out_p[k].append(nst_p[k])
            out_s[k].append(nst_s[k])

    return ((xp, xs) + tuple(jnp.stack(out_p[k]) for k in STATE_KEYS)
            + tuple(jnp.stack(out_s[k]) for k in STATE_KEYS))
```

```python
import functools
import math

import jax
import jax.numpy as jnp
from jax import lax
from jax.experimental import pallas as pl
from jax.experimental.pallas import tpu as pltpu

F32 = jnp.float32
BF16 = jnp.bfloat16

HEAD_DIM = 64
N_HEADS = 8
BRANCH_W = N_HEADS * HEAD_DIM
MLA_ROPE = HEAD_DIM // 2
MLA_QLORA = 256
MLA_KVLORA = 128
ROPE_THETA = 10000.0
SC_K = 3
GDN_K = 4
LRU_K = 4
LRU_C = 8.0
GDN_CHUNK = 64
EPS = 1e-6
NEG_BIG = -1e30
LANE = 128
SUBLANE = 8
QK_W = 2 * LANE

OFF_SC = 0
OFF_GQKV = 1536
OFF_GZ = 3072
OFF_GA = 3584
OFF_GB = 4096
OFF_LX = 4608
OFF_LG = 5120
OFF_QA = 5632
OFF_CKV = 5888
OFF_KPA = 6016
OFF_KPB = 6144
PROJ_W = 6272
PROJ_TN = 896


def _cparams(n_axes, vmem_mib):
    return pltpu.CompilerParams(dimension_semantics=("arbitrary",) * n_axes,
                                vmem_limit_bytes=vmem_mib * 1024 * 1024)


def _const_spec(shape):
    nd = len(shape)
    return pl.BlockSpec(shape, lambda *_: (0,) * nd, pipeline_mode=pl.Buffered(1))


def _bdot(a, b):
    return jnp.dot(a.astype(BF16), b.astype(BF16), preferred_element_type=F32)


def _bdot_nt(a, b):
    return lax.dot_general(a.astype(BF16), b.astype(BF16), (((1,), (1,)), ((), ())),
                           preferred_element_type=F32)


def _bdot_tn(a, b):
    return lax.dot_general(a.astype(BF16), b.astype(BF16), (((0,), (0,)), ((), ())),
                           preferred_element_type=F32)


def _silu(x):
    return x * jax.nn.sigmoid(x)


def _softplus(x):
    return jnp.maximum(x, 0.0) + jnp.log(1.0 + jnp.exp(-jnp.abs(x)))


def _gelu_tanh(x):
    return 0.5 * x * (1.0 + jnp.tanh(math.sqrt(2.0 / math.pi) * (x + 0.044715 * (x * x * x))))


def _rms(x, g):
    return x * lax.rsqrt(jnp.mean(x * x, axis=-1, keepdims=True) + EPS) * g


def _modulated_norm(x3, g, mod, shift_row, scale_row):
    y = _rms(x3, g)
    return y * (1.0 + mod[:, scale_row:scale_row + 1, :]) + mod[:, shift_row:shift_row + 1, :]


def _ada_kernel(c_ref, w_ref, b_ref, o_ref):
    o_ref[0] = _bdot(_silu(c_ref[...]), w_ref[0]) + b_ref[0]


def _ada(c_all, w_ada, b_ada):
    depth, d, n = w_ada.shape
    nb = c_all.shape[0]
    tn = n // 4
    return pl.pallas_call(
        _ada_kernel,
        grid=(depth, n // tn),
        in_specs=[pl.BlockSpec((nb, d), lambda l, j: (0, 0)),
                  pl.BlockSpec((1, d, tn), lambda l, j: (l, 0, j)),
                  pl.BlockSpec((1, 1, tn), lambda l, j: (l, 0, j))],
        out_specs=pl.BlockSpec((1, nb, tn), lambda l, j: (l, 0, j)),
        out_shape=jax.ShapeDtypeStruct((depth, nb, n), F32),
        compiler_params=_cparams(2, 40),
        name="ada_mod",
    )(c_all, w_ada, b_ada.reshape(depth, 1, n))


def _inproj_kernel(x_ref, mod_ref, g_ref, w_ref, o_ref, h_scr):
    @pl.when(pl.program_id(1) == 0)
    def _():
        h = _modulated_norm(x_ref[...], g_ref[...], mod_ref[...], 0, 1)
        h_scr[...] = h.reshape(h_scr.shape).astype(BF16)

    o_ref[...] = jnp.dot(h_scr[...], w_ref[...], preferred_element_type=F32)


def _inproj(x3, mod, g, w, nb, tt):
    nbat, t, d = x3.shape
    tper = t // tt
    tm = nb * tt
    n_m = (nbat // nb) * tper
    return pl.pallas_call(
        _inproj_kernel,
        grid=(n_m, PROJ_W // PROJ_TN),
        in_specs=[pl.BlockSpec((nb, tt, d), lambda i, j: (i // tper, i % tper, 0)),
                  pl.BlockSpec((nb, 6, d), lambda i, j: (i // tper, 0, 0)),
                  _const_spec((1, d)),
                  pl.BlockSpec((d, PROJ_TN), lambda i, j: (0, j))],
        out_specs=pl.BlockSpec((tm, PROJ_TN), lambda i, j: (i, j)),
        out_shape=jax.ShapeDtypeStruct((nbat * t, PROJ_W), F32),
        scratch_shapes=[pltpu.VMEM((tm, d), BF16)],
        compiler_params=_cparams(2, 40),
        name="in_proj",
    )(x3, mod, g, w)


def _causal_conv(u3, c8, w, k_w):
    nb, tt, c = u3.shape
    rows = lax.broadcasted_iota(jnp.int32, (nb, SUBLANE, c), 1)
    acc = None
    for j in range(k_w):
        s = k_w - 1 - j
        if s == 0:
            sh = u3
        else:
            full = pltpu.roll(u3, s, axis=1)
            top = jnp.where(rows < s, pltpu.roll(c8, s, axis=1), full[:, 0:SUBLANE, :])
            sh = top if tt == SUBLANE else jnp.concatenate([top, full[:, SUBLANE:, :]], axis=1)
        term = sh * w[j:j + 1, :]
        acc = term if acc is None else acc + term
    return acc


def _carry_in(t_idx, st_ref, c_scr):
    @pl.when(t_idx == 0)
    def _():
        c_scr[...] = st_ref[...]

    return c_scr[...]


def _sconv_kernel(bg_ref, cg_ref, xt_ref, st_ref, w_ref, y_ref, sto_ref, c_scr, *, nb, tt):
    c = bg_ref.shape[-1]
    c8 = _carry_in(pl.program_id(1), st_ref, c_scr)
    u3 = (cg_ref[...] * xt_ref[...]).reshape(nb, tt, c)
    conv = _causal_conv(u3, c8, w_ref[...], SC_K)
    y_ref[...] = bg_ref[...] * conv.reshape(nb * tt, c)
    last = u3[:, tt - SUBLANE:, :]
    c_scr[...] = last
    sto_ref[...] = last


def _tok_spec(tm, width, col_block, tper):
    return pl.BlockSpec((tm, width), lambda b, t: (b * tper + t, col_block))


def _state_spec(nb, c):
    return pl.BlockSpec((nb, SUBLANE, c), lambda b, t: (b, 0, 0))


def _sconv(proj, st8, w, nbat, t, nb, tt):
    tper = t // tt
    tm = nb * tt
    c = BRANCH_W
    kern = functools.partial(_sconv_kernel, nb=nb, tt=tt)
    return pl.pallas_call(
        kern,
        grid=(nbat // nb, tper),
        in_specs=[_tok_spec(tm, c, OFF_SC // c, tper), _tok_spec(tm, c, OFF_SC // c + 1, tper),
                  _tok_spec(tm, c, OFF_SC // c + 2, tper), _state_spec(nb, c), _const_spec((SC_K, c))],
        out_specs=[_tok_spec(tm, c, 0, tper), _state_spec(nb, c)],
        out_shape=[jax.ShapeDtypeStruct((nbat * t, c), F32), jax.ShapeDtypeStruct((nbat, SUBLANE, c), F32)],
        scratch_shapes=[pltpu.VMEM((nb, SUBLANE, c), F32)],
        compiler_params=_cparams(2, 40),
        name="short_conv",
    )(proj, proj, proj, st8, w)


def _gdn_pre_kernel(qkv_ref, ga_ref, gb_ref, st_ref, w_ref, alog_ref, dtb_ref,
                    o_ref, g_ref, beta_ref, sto_ref, c_scr, *, nb, tt):
    c = qkv_ref.shape[-1]
    c8 = _carry_in(pl.program_id(1), st_ref, c_scr)
    u3 = qkv_ref[...].reshape(nb, tt, c)
    o_ref[...] = _silu(_causal_conv(u3, c8, w_ref[...], GDN_K)).reshape(nb * tt, c)
    last = u3[:, tt - SUBLANE:, :]
    c_scr[...] = last
    sto_ref[...] = last
    g_ref[...] = -jnp.exp(alog_ref[...]) * _softplus(ga_ref[...] + dtb_ref[...])
    beta_ref[...] = jax.nn.sigmoid(gb_ref[...])


def _gdn_pre(proj, st8, w, alog_b, dtb_b, nbat, t, nb, tt):
    tper = t // tt
    tm = nb * tt
    c3 = 3 * BRANCH_W
    c = BRANCH_W
    kern = functools.partial(_gdn_pre_kernel, nb=nb, tt=tt)
    return pl.pallas_call(
        kern,
        grid=(nbat // nb, tper),
        in_specs=[_tok_spec(tm, c3, OFF_GQKV // c3, tper), _tok_spec(tm, c, OFF_GA // c, tper),
                  _tok_spec(tm, c, OFF_GB // c, tper), _state_spec(nb, c3), _const_spec((GDN_K, c3)),
                  _const_spec((1, c)), _const_spec((1, c))],
        out_specs=[_tok_spec(tm, c3, 0, tper), _tok_spec(tm, c, 0, tper), _tok_spec(tm, c, 0, tper),
                   _state_spec(nb, c3)],
        out_shape=[jax.ShapeDtypeStruct((nbat * t, c3), F32), jax.ShapeDtypeStruct((nbat * t, c), F32),
                   jax.ShapeDtypeStruct((nbat * t, c), F32), jax.ShapeDtypeStruct((nbat, SUBLANE, c3), F32)],
        scratch_shapes=[pltpu.VMEM((nb, SUBLANE, c3), F32)],
        compiler_params=_cparams(2, 48),
        name="gdn_pre",
    )(proj, proj, proj, st8, w, alog_b, dtb_b)


def _lru_pre_kernel(lx_ref, st_ref, w_ref, cb_ref, wg_ref, bg_ref, nsl_ref,
                    a_ref, bx_ref, sto_ref, c_scr, *, nb, tt, pos0):
    c = lx_ref.shape[-1]
    t_idx = pl.program_id(1)
    c8 = _carry_in(t_idx, st_ref, c_scr)
    u3 = lx_ref[...].reshape(nb, tt, c)
    u = (_causal_conv(u3, c8, w_ref[...], LRU_K) + cb_ref[...]).reshape(nb * tt, c)
    last = u3[:, tt - SUBLANE:, :]
    c_scr[...] = last
    sto_ref[...] = last
    gates = jax.nn.sigmoid(_bdot(u, wg_ref[...]) + bg_ref[...])
    r = gates[:, :c]
    ig = gates[:, c:]
    log_a = -LRU_C * r * _softplus(-nsl_ref[...])
    a = jnp.exp(log_a)
    mult = jnp.sqrt(1.0 - jnp.exp(2.0 * log_a))
    pos = pos0 + t_idx * tt + lax.broadcasted_iota(jnp.int32, (nb, tt, c), 1).reshape(nb * tt, c)
    mult = jnp.where(pos == 0, 1.0, mult)
    a_ref[...] = a
    bx_ref[...] = mult * (ig * u)


def _lru_pre(proj, st8, w, cb, wg, bg, nsl, nbat, t, nb, tt, pos0):
    tper = t // tt
    tm = nb * tt
    c = BRANCH_W
    kern = functools.partial(_lru_pre_kernel, nb=nb, tt=tt, pos0=pos0)
    return pl.pallas_call(
        kern,
        grid=(nbat // nb, tper),
        in_specs=[_tok_spec(tm, c, OFF_LX // c, tper), _state_spec(nb, c), _const_spec((LRU_K, c)),
                  _const_spec((1, c)), _const_spec((c, 2 * c)), _const_spec((1, 2 * c)), _const_spec((1, c))],
        out_specs=[_tok_spec(tm, c, 0, tper), _tok_spec(tm, c, 0, tper), _state_spec(nb, c)],
        out_shape=[jax.ShapeDtypeStruct((nbat * t, c), F32), jax.ShapeDtypeStruct((nbat * t, c), F32),
                   jax.ShapeDtypeStruct((nbat, SUBLANE, c), F32)],
        scratch_shapes=[pltpu.VMEM((nb, SUBLANE, c), F32)],
        compiler_params=_cparams(2, 40),
        name="lru_pre",
    )(proj, st8, w, cb, wg, bg, nsl)


def _lru_scan_kernel(a_ref, bx_ref, lg_ref, h0_ref, y_ref, hl_ref, h_scr, *, nb, tt):
    c = a_ref.shape[-1]

    @pl.when(pl.program_id(1) == 0)
    def _():
        h_scr[...] = h0_ref[...]

    a = a_ref[...].reshape(nb, tt, c)
    b = bx_ref[...].reshape(nb, tt, c)
    rows = lax.broadcasted_iota(jnp.int32, (nb, tt, c), 1)
    d = 1
    while d < tt:
        keep = rows >= d
        a_s = jnp.where(keep, pltpu.roll(a, d, axis=1), 1.0)
        b_s = jnp.where(keep, pltpu.roll(b, d, axis=1), 0.0)
        b = a * b_s + b
        a = a * a_s
        d *= 2
    hs = b + a * h_scr[...]
    last = hs[:, tt - 1:tt, :]
    h_scr[...] = last
    hl_ref[...] = last
    y_ref[...] = hs.reshape(nb * tt, c) * _gelu_tanh(lg_ref[...])


def _lru_scan(a, bx, proj, h0, nbat, t, nb, tt):
    tper = t // tt
    tm = nb * tt
    c = BRANCH_W
    kern = functools.partial(_lru_scan_kernel, nb=nb, tt=tt)
    h_spec = pl.BlockSpec((nb, 1, c), lambda b, t_: (b, 0, 0))
    return pl.pallas_call(
        kern,
        grid=(nbat // nb, tper),
        in_specs=[_tok_spec(tm, c, 0, tper), _tok_spec(tm, c, 0, tper), _tok_spec(tm, c, OFF_LG // c, tper), h_spec],
        out_specs=[_tok_spec(tm, c, 0, tper), h_spec],
        out_shape=[jax.ShapeDtypeStruct((nbat * t, c), F32), jax.ShapeDtypeStruct((nbat, 1, c), F32)],
        scratch_shapes=[pltpu.VMEM((nb, 1, c), F32)],
        compiler_params=_cparams(2, 40),
        name="lru_scan",
    )(a, bx, proj, h0)


def _mla_pre_kernel(qa_ref, ckv_ref, kpa_ref, kpb_ref, cos_ref, sin_ref, gq_ref, gkv_ref,
                    wn_ref, wpa_ref, wpb_ref, bdk_ref, q_out, ckv_out, kpe_out, kf_out, *rest, nb, tt):
    cos = cos_ref[...]
    sin = sin_ref[...]
    cq = _rms(qa_ref[...], gq_ref[...]).astype(BF16)
    qn = jnp.dot(cq, wn_ref[...], preferred_element_type=F32)
    qabs = _bdot(qn, bdk_ref[...])
    pa = jnp.dot(cq, wpa_ref[...], preferred_element_type=F32)
    pb = jnp.dot(cq, wpb_ref[...], preferred_element_type=F32)
    for h in range(N_HEADS):
        sl = slice(h * LANE, (h + 1) * LANE)
        q_pe = pa[:, sl] * cos + pb[:, sl] * sin
        q_out[:, h, :, 0:LANE] = qabs[:, sl].reshape(nb, tt, LANE).astype(q_out.dtype)
        q_out[:, h, :, LANE:QK_W] = q_pe.reshape(nb, tt, LANE).astype(q_out.dtype)
    ckv = _rms(ckv_ref[...], gkv_ref[...])
    kpe = kpa_ref[...] * cos + kpb_ref[...] * sin
    ckv_out[...] = ckv
    kpe_out[...] = kpe[:, :MLA_ROPE]
    kf_out[...] = jnp.concatenate([ckv, kpe], axis=-1).astype(BF16)
    if rest:
        rest[0][0] = ckv.T.astype(BF16)


def _mla_pre(proj, cos, sin, gq, gkv, wn, wpa, wpb, bdk, nbat, t, nb, tt, q_dtype, emit_vt):
    tper = t // tt
    tm = nb * tt
    hw = N_HEADS * LANE
    kern = functools.partial(_mla_pre_kernel, nb=nb, tt=tt)
    tab_spec = pl.BlockSpec((tm, LANE), lambda b, t_: (t_, 0))
    vt_specs, vt_shapes = [], []
    if emit_vt:
        assert nb == 1
        vt_specs = [pl.BlockSpec((1, MLA_KVLORA, tt), lambda b, t_: (b * tper + t_, 0, 0))]
        vt_shapes = [jax.ShapeDtypeStruct((nbat * tper, MLA_KVLORA, tt), BF16)]
    return pl.pallas_call(
        kern,
        grid=(nbat // nb, tper),
        in_specs=[_tok_spec(tm, MLA_QLORA, OFF_QA // MLA_QLORA, tper), _tok_spec(tm, LANE, OFF_CKV // LANE, tper),
                  _tok_spec(tm, LANE, OFF_KPA // LANE, tper), _tok_spec(tm, LANE, OFF_KPB // LANE, tper),
                  tab_spec, tab_spec, _const_spec((1, MLA_QLORA)), _const_spec((1, MLA_KVLORA)),
                  _const_spec((MLA_QLORA, BRANCH_W)), _const_spec((MLA_QLORA, hw)), _const_spec((MLA_QLORA, hw)),
                  _const_spec((BRANCH_W, hw))],
        out_specs=[pl.BlockSpec((nb, N_HEADS, tt, QK_W), lambda b, t_: (b, 0, t_, 0)),
                   _tok_spec(tm, MLA_KVLORA, 0, tper), _tok_spec(tm, MLA_ROPE, 0, tper), _tok_spec(tm, QK_W, 0, tper)]
                  + vt_specs,
        out_shape=[jax.ShapeDtypeStruct((nbat, N_HEADS, t, QK_W), q_dtype),
                   jax.ShapeDtypeStruct((nbat * t, MLA_KVLORA), F32),
                   jax.ShapeDtypeStruct((nbat * t, MLA_ROPE), F32),
                   jax.ShapeDtypeStruct((nbat * t, QK_W), BF16)] + vt_shapes,
        compiler_params=_cparams(2, 48),
        name="mla_pre",
    )(proj, proj, proj, proj, cos, sin, gq, gkv, wn, wpa, wpb, bdk)


def _heads_to_lanes(o, rows_per_head):
    return jnp.concatenate([o[h * rows_per_head:(h + 1) * rows_per_head] for h in range(N_HEADS)], axis=-1)


def _attn_prompt_kernel(q_ref, kf_ref, vt_ref, wvt_ref, y_ref, m_scr, l_scr, acc_scr, *, tq, tk, scale):
    i = pl.program_id(1)
    n_full = (i * tq) // tk
    w2 = 2 * tq
    krow = lax.broadcasted_iota(jnp.int32, (tk, w2), 0)
    qcol = i * tq + (lax.broadcasted_iota(jnp.int32, (tk, w2), 1) & (tq - 1))
    heads = range(N_HEADS // 2)
    m_scr[...] = jnp.full(m_scr.shape, NEG_BIG, F32)
    l_scr[...] = jnp.zeros(l_scr.shape, F32)
    acc_scr[...] = jnp.zeros(acc_scr.shape, F32)

    def step(j, masked):
        kblk = kf_ref[pl.ds(pl.multiple_of(j * tk, tk), tk), :]
        vt = vt_ref[j]
        s = [lax.dot_general(kblk, q_ref[0, 2 * h:2 * h + 2].reshape(w2, QK_W), (((1,), (1,)), ((), ())),
                             preferred_element_type=F32) * scale
             for h in heads]
        if masked:
            keep = j * tk + krow <= qcol
            s = [jnp.where(keep, s[h], NEG_BIG) for h in heads]
        m_old = [m_scr[h] for h in heads]
        m_new = [jnp.maximum(m_old[h], jnp.max(s[h], axis=0, keepdims=True)) for h in heads]
        p = [jnp.exp(s[h] - m_new[h]) for h in heads]
        alpha = [jnp.exp(m_old[h] - m_new[h]) for h in heads]
        pv = [jnp.dot(vt, p[h].astype(BF16), preferred_element_type=F32) for h in heads]
        for h in heads:
            m_scr[h] = m_new[h]
            l_scr[h] = alpha[h] * l_scr[h] + jnp.sum(p[h], axis=0, keepdims=True)
            acc_scr[h] = alpha[h] * acc_scr[h] + pv[h]

    def body(j, carry):
        step(j, False)
        return carry

    lax.fori_loop(0, n_full, body, 0)
    step(n_full, True)
    o = [acc_scr[h] / l_scr[h] for h in heads]
    outs = [_bdot(wvt_ref[hh], o[hh // 2][:, (hh % 2) * tq:(hh % 2 + 1) * tq]) for hh in range(N_HEADS)]
    y_ref[...] = jnp.concatenate(outs, axis=0).T


def _attn_prompt(q_full, kf, vt, wvt, nbat, t, tq, tk):
    tper = t // tq
    kper = t // tk
    scale = (HEAD_DIM + MLA_ROPE) ** -0.5
    kern = functools.partial(_attn_prompt_kernel, tq=tq, tk=tk, scale=scale)
    return pl.pallas_call(
        kern,
        grid=(nbat, tper),
        in_specs=[pl.BlockSpec((1, N_HEADS, tq, QK_W), lambda b, i: (b, 0, i, 0)),
                  pl.BlockSpec((t, QK_W), lambda b, i: (b, 0)),
                  pl.BlockSpec((kper, MLA_KVLORA, tk), lambda b, i: (b, 0, 0)),
                  _const_spec(wvt.shape)],
        out_specs=pl.BlockSpec((tq, BRANCH_W), lambda b, i: (b * tper + i, 0)),
        out_shape=jax.ShapeDtypeStruct((nbat * t, BRANCH_W), F32),
        scratch_shapes=[pltpu.VMEM((N_HEADS // 2, 1, 2 * tq), F32), pltpu.VMEM((N_HEADS // 2, 1, 2 * tq), F32),
                        pltpu.VMEM((N_HEADS // 2, MLA_KVLORA, 2 * tq), F32)],
        compiler_params=_cparams(2, 40),
        name="attn_prompt",
    )(q_full, kf, vt, wvt)


def _attn_sample_kernel(pt_ref, q_ref, *refs, pg, ts, page, scale):
    ckv_pages = refs[:pg]
    kpe_pages = refs[pg:2 * pg]
    ckvn_ref, kpen_ref, bdv_ref, y_ref, m_scr, l_scr, acc_scr = refs[2 * pg:]
    j = pl.program_id(1)
    r = N_HEADS * ts

    @pl.when(j == 0)
    def _():
        m_scr[...] = jnp.full(m_scr.shape, NEG_BIG, F32)
        l_scr[...] = jnp.zeros(l_scr.shape, F32)
        acc_scr[...] = jnp.zeros(acc_scr.shape, F32)

    q = q_ref[0].reshape(r, QK_W)
    q_lat = q[:, :MLA_KVLORA].astype(BF16)
    q_pe = q[:, MLA_KVLORA:MLA_KVLORA + MLA_ROPE].astype(BF16)

    c_all = jnp.concatenate([ckv_pages[i][0, 0].astype(BF16) for i in range(pg)], axis=0)
    kpt_all = jnp.concatenate([kpe_pages[i][0, 0].astype(BF16) for i in range(pg)], axis=1)
    s = (lax.dot_general(q_lat, c_all, (((1,), (1,)), ((), ())), preferred_element_type=F32)
         + jnp.dot(q_pe, kpt_all, preferred_element_type=F32)) * scale
    m = m_scr[...]
    m_new = jnp.maximum(m, jnp.max(s, axis=-1, keepdims=True))
    alpha = jnp.exp(m - m_new)
    p32 = jnp.exp(s - m_new)
    l_new = alpha * l_scr[...] + jnp.sum(p32, axis=-1, keepdims=True)
    acc_new = alpha * acc_scr[...] + jnp.dot(p32.astype(BF16), c_all, preferred_element_type=F32)
    m_scr[...] = m_new
    l_scr[...] = l_new
    acc_scr[...] = acc_new

    @pl.when(j == pl.num_programs(1) - 1)
    def _():
        pad = jnp.zeros((page - ts, MLA_KVLORA), F32)
        cn = jnp.concatenate([ckvn_ref[...], pad], axis=0).astype(BF16)
        kn = jnp.concatenate([kpen_ref[...], pad[:, :MLA_ROPE]], axis=0).astype(BF16)
        sn = (lax.dot_general(q_lat, cn, (((1,), (1,)), ((), ())), preferred_element_type=F32)
              + lax.dot_general(q_pe, kn, (((1,), (1,)), ((), ())), preferred_element_type=F32)) * scale
        qpos = lax.broadcasted_iota(jnp.int32, (r, page), 0) & (ts - 1)
        kpos = lax.broadcasted_iota(jnp.int32, (r, page), 1)
        sn = jnp.where(kpos <= qpos, sn, NEG_BIG)
        m2 = jnp.maximum(m_new, jnp.max(sn, axis=-1, keepdims=True))
        a2 = jnp.exp(m_new - m2)
        p2 = jnp.exp(sn - m2)
        l2 = a2 * l_new + jnp.sum(p2, axis=-1, keepdims=True)
        acc2 = a2 * acc_new + jnp.dot(p2.astype(BF16), cn, preferred_element_type=F32)
        o = _heads_to_lanes(acc2 / l2, ts)
        y_ref[...] = _bdot(o, bdv_ref[...])


def _attn_sample(q_full, cache_ckv, cache_kpe, page_table, layer, ckv_new, kpe_new, bdv, nbat, ts):
    n_pages = page_table.shape[1]
    page = cache_ckv.shape[2]
    pg = math.gcd(n_pages, 16)
    scale = (HEAD_DIM + MLA_ROPE) ** -0.5
    r = N_HEADS * ts
    kern = functools.partial(_attn_sample_kernel, pg=pg, ts=ts, page=page, scale=scale)

    def page_spec(i, rows, width):
        return pl.BlockSpec((1, 1, rows, width), lambda b, j, pt: (layer, pt[b * n_pages + j * pg + i], 0, 0))

    in_specs = ([pl.BlockSpec((1, N_HEADS, ts, QK_W), lambda b, j, pt: (b, 0, 0, 0))]
                + [page_spec(i, page, MLA_KVLORA) for i in range(pg)]
                + [page_spec(i, MLA_ROPE, page) for i in range(pg)]
                + [pl.BlockSpec((ts, MLA_KVLORA), lambda b, j, pt: (b, 0)),
                   pl.BlockSpec((ts, MLA_ROPE), lambda b, j, pt: (b, 0)),
                   pl.BlockSpec((N_HEADS * LANE, BRANCH_W), lambda b, j, pt: (0, 0))])
    grid_spec = pltpu.PrefetchScalarGridSpec(
        num_scalar_prefetch=1,
        grid=(nbat, n_pages // pg),
        in_specs=in_specs,
        out_specs=pl.BlockSpec((ts, BRANCH_W), lambda b, j, pt: (b, 0)),
        scratch_shapes=[pltpu.VMEM((r, 1), F32), pltpu.VMEM((r, 1), F32), pltpu.VMEM((r, MLA_KVLORA), F32)],
    )
    return pl.pallas_call(
        kern,
        grid_spec=grid_spec,
        out_shape=jax.ShapeDtypeStruct((nbat * ts, BRANCH_W), F32),
        compiler_params=_cparams(2, 40),
        name="attn_sample",
    )(page_table.reshape(-1), q_full, *([cache_ckv] * pg), *([cache_kpe] * pg), ckv_new, kpe_new, bdv)


def _cumsum_rows(x, n):
    rows = lax.broadcasted_iota(jnp.int32, x.shape, 0)
    d = 1
    while d < n:
        x = x + jnp.where(rows >= d, pltpu.roll(x, d, axis=0), 0.0)
        d *= 2
    return x


def _inverse_masks(n):
    rows = lax.broadcasted_iota(jnp.int32, (n, n), 0)
    cols = lax.broadcasted_iota(jnp.int32, (n, n), 1)
    masks = [rows == cols, (rows >> 1) == (cols >> 1)]
    k = 1
    while (1 << k) < n:
        masks.append(((rows >> k) ^ (cols >> k)) == 1)
        k += 1
    return masks


def _gdn_kernel(q_ref, k_ref, v_ref, g_ref, beta_ref, z_ref, s0_ref, gn_ref, y_ref, sf_ref, s_scr, *, n_chunks):
    @pl.when(pl.program_id(1) == 0)
    def _():
        s_scr[...] = s0_ref[0]

    cc = GDN_CHUNK
    rows = lax.broadcasted_iota(jnp.int32, (cc, cc), 0)
    cols = lax.broadcasted_iota(jnp.int32, (cc, cc), 1)
    tril = rows >= cols
    strict = rows > cols
    masks = _inverse_masks(cc)
    eye = jnp.where(masks[0], 1.0, 0.0)
    gn = gn_ref[...]
    probs = [(ci, h) for ci in range(n_chunks) for h in range(N_HEADS)]

    def rs(ci):
        return slice(ci * cc, (ci + 1) * cc)

    def hs(h):
        return slice(h * HEAD_DIM, (h + 1) * HEAD_DIM)

    gc_all = [_cumsum_rows(g_ref[rs(ci), :], cc) for ci in range(n_chunks)]
    gc = [gc_all[ci][:, hs(h)] for ci, h in probs]
    beta = [beta_ref[rs(ci), hs(h)] for ci, h in probs]
    q = []
    k = []
    for ci, h in probs:
        qr = q_ref[rs(ci), hs(h)]
        kr = k_ref[rs(ci), hs(h)]
        q.append(qr * lax.rsqrt(jnp.sum(qr * qr, axis=-1, keepdims=True) + EPS) * (HEAD_DIM ** -0.5))
        k.append(kr * lax.rsqrt(jnp.sum(kr * kr, axis=-1, keepdims=True) + EPS))
    kb = [k[p] * beta[p] for p in range(len(probs))]
    decay = []
    for p in range(len(probs)):
        diff = gc[p] - gc[p].T
        decay.append(jnp.where(tril, jnp.exp(jnp.where(tril, diff, 0.0)), 0.0))
    kk = [_bdot_nt(kb[p], k[p]) for p in range(len(probs))]
    a_mat = [jnp.where(strict, kk[p] * decay[p], 0.0) for p in range(len(probs))]
    t = [eye - jnp.where(masks[1], a_mat[p], 0.0) for p in range(len(probs))]
    for m in masks[2:]:
        tl = [_bdot(t[p], jnp.where(m, a_mat[p], 0.0)) for p in range(len(probs))]
        t = [t[p] - _bdot(tl[p], t[p]) for p in range(len(probs))]
    egc = [jnp.exp(gc[p]) for p in range(len(probs))]
    sol = [_bdot(t[p], jnp.concatenate([v_ref[rs(ci), hs(h)] * beta[p], kb[p] * egc[p]], axis=-1))
           for p, (ci, h) in enumerate(probs)]
    qk = [_bdot_nt(q[p], k[p]) for p in range(len(probs))]
    intra = [jnp.where(tril, qk[p] * decay[p], 0.0) for p in range(len(probs))]
    qe = [q[p] * egc[p] for p in range(len(probs))]
    g_last = [gc[p][cc - 1:cc, :] for p in range(len(probs))]
    kdec = [k[p] * jnp.exp(g_last[p] - gc[p]) for p in range(len(probs))]
    sdec = [jnp.exp(g_last[p]) for p in range(len(probs))]

    for ci in range(n_chunks):
        ps = [ci * N_HEADS + h for h in range(N_HEADS)]
        s_old = [s_scr[h] for h in range(N_HEADS)]
        v_new = [sol[p][:, :HEAD_DIM] - _bdot(sol[p][:, HEAD_DIM:], s_old[h]) for h, p in enumerate(ps)]
        o_st = [_bdot(qe[p], s_old[h]) for h, p in enumerate(ps)]
        o_in = [_bdot(intra[p], v_new[h]) for h, p in enumerate(ps)]
        s_up = [_bdot_tn(kdec[p], v_new[h]) for h, p in enumerate(ps)]
        outs = []
        for h, p in enumerate(ps):
            s_scr[h] = s_old[h] * sdec[p] + s_up[h]
            outs.append(_rms(o_st[h] + o_in[h], gn[:, hs(h)]) * _silu(z_ref[rs(ci), hs(h)]))
        y_ref[rs(ci), :] = jnp.concatenate(outs, axis=-1)
    sf_ref[0] = s_scr[...]


def _gdn(qkvc, g_b, beta_b, proj, s0, gn_b, nbat, t, tc):
    tper = t // tc
    c = BRANCH_W
    kern = functools.partial(_gdn_kernel, n_chunks=tc // GDN_CHUNK)
    s_spec = pl.BlockSpec((1, N_HEADS, HEAD_DIM, HEAD_DIM), lambda b, t_: (b, 0, 0, 0))
    return pl.pallas_call(
        kern,
        grid=(nbat, tper),
        in_specs=[_tok_spec(tc, c, 0, tper), _tok_spec(tc, c, 1, tper), _tok_spec(tc, c, 2, tper),
                  _tok_spec(tc, c, 0, tper), _tok_spec(tc, c, 0, tper), _tok_spec(tc, c, OFF_GZ // c, tper),
                  s_spec, _const_spec((1, c))],
        out_specs=[_tok_spec(tc, c, 0, tper), s_spec],
        out_shape=[jax.ShapeDtypeStruct((nbat * t, c), F32),
                   jax.ShapeDtypeStruct((nbat, N_HEADS, HEAD_DIM, HEAD_DIM), F32)],
        scratch_shapes=[pltpu.VMEM((N_HEADS, HEAD_DIM, HEAD_DIM), F32)],
        compiler_params=_cparams(2, 40),
        name="gdn_rule",
    )(qkvc, qkvc, qkvc, g_b, beta_b, proj, s0, gn_b)


def _merge_kernel(x_ref, mod_ref, g_ref, ya_ref, yb_ref, yc_ref, yd_ref, wg_ref, wbo_ref, wmix_ref, o_ref):
    x = x_ref[...]
    nb, tt, d = x.shape
    mod = mod_ref[...]
    h = _modulated_norm(x, g_ref[...], mod, 0, 1).reshape(nb * tt, d).astype(BF16)
    acc = None
    for n, y_ref in enumerate((ya_ref, yb_ref, yc_ref, yd_ref)):
        gate = jax.nn.sigmoid(jnp.dot(h, wg_ref[:, n * d:(n + 1) * d], preferred_element_type=F32))
        term = gate * _bdot(y_ref[...], wbo_ref[n])
        acc = term if acc is None else acc + term
    mix = _bdot(acc, wmix_ref[...])
    o_ref[...] = x + mod[:, 2:3, :] * mix.reshape(nb, tt, d)


def _merge(x3, mod, g, ys, wg, wbo, wmix, nb, tt):
    nbat, t, d = x3.shape
    tper = t // tt
    tm = nb * tt
    c = BRANCH_W
    x_spec = pl.BlockSpec((nb, tt, d), lambda b, t_: (b, t_, 0))
    return pl.pallas_call(
        _merge_kernel,
        grid=(nbat // nb, tper),
        in_specs=[x_spec, pl.BlockSpec((nb, 6, d), lambda b, t_: (b, 0, 0)), _const_spec((1, d))]
                 + [_tok_spec(tm, c, 0, tper)] * 4
                 + [_const_spec(wg.shape), _const_spec(wbo.shape), _const_spec(wmix.shape)],
        out_specs=x_spec,
        out_shape=jax.ShapeDtypeStruct(x3.shape, F32),
        compiler_params=_cparams(2, 48),
        name="merge",
    )(x3, mod, g, *ys, wg, wbo, wmix)


def _ffn_kernel(x_ref, mod_ref, g_ref, win_ref, wout_ref, gf_ref, o_ref, *, final):
    x = x_ref[...]
    nb, tt, d = x.shape
    mod = mod_ref[...]
    fh = wout_ref.shape[0]
    h = _modulated_norm(x, g_ref[...], mod, 3, 4).reshape(nb * tt, d).astype(BF16)
    gate = jnp.dot(h, win_ref[:, :fh], preferred_element_type=F32)
    up = jnp.dot(h, win_ref[:, fh:], preferred_element_type=F32)
    out = _bdot(_silu(gate) * up, wout_ref[...])
    xo = x + mod[:, 5:6, :] * out.reshape(nb, tt, d)
    o_ref[...] = _rms(xo, gf_ref[...]) if final else xo


def _ffn(x3, mod, g, win, wout, gf, nb, tt, final):
    nbat, t, d = x3.shape
    x_spec = pl.BlockSpec((nb, tt, d), lambda b, t_: (b, t_, 0))
    kern = functools.partial(_ffn_kernel, final=final)
    return pl.pallas_call(
        kern,
        grid=(nbat // nb, t // tt),
        in_specs=[x_spec, pl.BlockSpec((nb, 6, d), lambda b, t_: (b, 0, 0)), _const_spec((1, d)),
                  _const_spec(win.shape), _const_spec(wout.shape), _const_spec((1, d))],
        out_specs=x_spec,
        out_shape=jax.ShapeDtypeStruct(x3.shape, F32),
        compiler_params=_cparams(2, 56),
        name="ffn",
    )(x3, mod, g, win, wout, gf)


def _block_diag(blocks):
    h, r, c = blocks.shape
    eye = jnp.eye(h, dtype=blocks.dtype)
    return jnp.einsum('hrc,hg->hrgc', blocks, eye).reshape(h * r, h * c)


def _prep_layer_weights(w_in, w_qb, w_kvb, w_lru_gate_a, w_lru_gate_x):
    d = w_in.shape[0]
    c = BRANCH_W
    o = 0
    segs = {}
    for name, width in (('sc', 3 * c), ('qa', MLA_QLORA), ('ckv', MLA_KVLORA), ('kpe', MLA_ROPE), ('gqkv', 3 * c),
                        ('gz', c), ('ga', N_HEADS), ('gb', N_HEADS), ('lx', c), ('lg', c)):
        segs[name] = w_in[:, o:o + width]
        o += width
    half = MLA_ROPE // 2
    zpad = jnp.zeros((d, LANE - MLA_ROPE), F32)
    kpe = segs['kpe']
    kpa = jnp.concatenate([kpe, zpad], axis=1)
    kpb = jnp.concatenate([-kpe[:, half:], kpe[:, :half], zpad], axis=1)
    w_proj = jnp.concatenate(
        [segs['sc'], segs['gqkv'], segs['gz'], jnp.repeat(segs['ga'], HEAD_DIM, axis=1),
         jnp.repeat(segs['gb'], HEAD_DIM, axis=1), segs['lx'], segs['lg'], segs['qa'], segs['ckv'], kpa, kpb],
        axis=1).astype(BF16)

    ql = w_qb.shape[0]
    wq = w_qb.reshape(ql, N_HEADS, HEAD_DIM + MLA_ROPE)
    wn = wq[:, :, :HEAD_DIM].reshape(ql, c)
    pe = wq[:, :, HEAD_DIM:]
    zq = jnp.zeros((ql, N_HEADS, LANE - MLA_ROPE), F32)
    wpa = jnp.concatenate([pe, zq], axis=2).reshape(ql, N_HEADS * LANE)
    wpb = jnp.concatenate([-pe[:, :, half:], pe[:, :, :half], zq], axis=2).reshape(ql, N_HEADS * LANE)
    w_kb = w_kvb[:, :, :HEAD_DIM]
    w_vb = w_kvb[:, :, HEAD_DIM:]
    bdk = _block_diag(jnp.transpose(w_kb, (1, 2, 0)))
    bdv = _block_diag(jnp.transpose(w_vb, (1, 0, 2)))
    wlg = jnp.concatenate([_block_diag(w_lru_gate_a), _block_diag(w_lru_gate_x)], axis=1)
    wvt = jnp.transpose(w_vb, (1, 2, 0))
    return dict(w_proj=w_proj, wn=wn.astype(BF16), wpa=wpa.astype(BF16), wpb=wpb.astype(BF16),
                bdk=bdk.astype(BF16), bdv=bdv.astype(BF16), wvt=wvt.astype(BF16), wlg=wlg.astype(BF16))


def _rope_tables(pos):
    half = MLA_ROPE // 2
    inv = ROPE_THETA ** (-jnp.arange(half, dtype=F32) / half)
    ang = pos.astype(F32)[:, None] * inv[None, :]
    pad = jnp.zeros((pos.shape[0], LANE - MLA_ROPE), F32)
    cos = jnp.concatenate([jnp.cos(ang), jnp.cos(ang), pad], axis=1)
    sin = jnp.concatenate([jnp.sin(ang), jnp.sin(ang), pad], axis=1)
    return cos, sin


def _pad_state(st, k_w):
    return jnp.pad(st, ((0, 0), (SUBLANE - (k_w - 1), 0), (0, 0)))


def _tile(t, pref):
    tt = min(t, pref)
    while t % tt:
        tt //= 2
    return tt


def _group_layer(x3, mod, lw, p, st, rope, attend, cfg):
    nbat, t, d = x3.shape
    nb, tt = cfg['nb'], cfg['tt']
    proj = _inproj(x3, mod, p['g_norm_mix'], lw['w_proj'], cfg['nb_proj'], cfg['tt_proj'])
    y_a, sc8 = _sconv(proj, st['sconv'], p['w_sc_conv'], nbat, t, nb, tt)
    q_full, ckv, kpe, kf, *vt = _mla_pre(proj, rope[0], rope[1], p['g_q_norm'], p['g_kv_norm'], lw['wn'], lw['wpa'],
                                         lw['wpb'], lw['bdk'], nbat, t, nb, tt, cfg['q_dtype'], cfg['emit_vt'])
    y_b = attend(q_full, ckv, kpe, kf, vt, lw)
    qkvc, g_b, beta_b, gc8 = _gdn_pre(proj, st['gdn_conv'], p['w_gdn_conv'], p['alog_b'], p['dtb_b'], nbat, t, nb, tt)
    if cfg['gdn_pad']:
        padn = GDN_CHUNK - t

        def padt(a):
            return jnp.pad(a.reshape(nbat, t, -1), ((0, 0), (0, padn), (0, 0))).reshape(nbat * GDN_CHUNK, -1)

        y_c, s_gdn = _gdn(padt(qkvc), padt(g_b), padt(beta_b), padt(proj[:, :OFF_GA]), st['gdn'], p['gn_b'],
                          nbat, GDN_CHUNK, GDN_CHUNK)
        y_c = y_c.reshape(nbat, GDN_CHUNK, -1)[:, :t].reshape(nbat * t, -1)
    else:
        y_c, s_gdn = _gdn(qkvc, g_b, beta_b, proj, st['gdn'], p['gn_b'], nbat, t, cfg['tc'])
    a, bx, lc8 = _lru_pre(proj, st['lru_conv'], p['w_lru_conv'], p['b_lru_conv'], lw['wlg'], p['b_lru_gates'],
                          p['nsl'], nbat, t, nb, tt, cfg['pos0'])
    y_d, h_last = _lru_scan(a, bx, proj, st['lru'], nbat, t, cfg['nb_scan'], cfg['tt_scan'])
    x1 = _merge(x3, mod, p['g_norm_mix'], (y_a, y_b, y_c, y_d), p['w_merge_gate'], p['w_branch_out'],
                p['w_mix_out'], cfg['nb_mm'], cfg['tt_mm'])
    x2 = _ffn(x1, mod, p['g_norm_ffn'], p['w_ffn_in'], p['w_ffn_out'], p['g_final'], cfg['nb_mm'], cfg['tt_mm'],
              cfg['final'])
    new_st = dict(ckv=ckv.reshape(nbat, t, -1), kpe=kpe.reshape(nbat, t, -1),
                  sconv=sc8[:, SUBLANE - (SC_K - 1):], gdn_conv=gc8[:, SUBLANE - (GDN_K - 1):], gdn=s_gdn,
                  lru_conv=lc8[:, SUBLANE - (LRU_K - 1):], lru=h_last[:, 0, :])
    return x2, new_st


STATE_KEYS = ('ckv', 'kpe', 'sconv', 'gdn_conv', 'gdn', 'lru_conv', 'lru')


def kernel(x_prompt, x_sample, c_prompt, c_sample, cache_mla_ckv, cache_mla_kpe, page_table, state_sconv, state_gdn_conv, state_gdn, state_lru_conv, state_lru, w_ada, b_ada, g_norm_mix, g_norm_ffn, w_in, w_sc_conv, g_q_norm, w_qb, g_kv_norm, w_kvb, w_gdn_conv, gdn_a_log, gdn_dt_bias, g_gdn_norm, w_lru_conv, b_lru_conv, w_lru_gate_a, b_lru_gate_a, w_lru_gate_x, b_lru_gate_x, lru_lambda, w_branch_out, w_merge_gate, w_mix_out, w_ffn_in, w_ffn_out, g_final):
    bp, tp, d = x_prompt.shape
    bs, ts, _ = x_sample.shape
    depth = w_in.shape[0]
    n_pages = page_table.shape[1]
    past_len = n_pages * cache_mla_ckv.shape[2]
    c = BRANCH_W
    assert ts == SUBLANE and tp % GDN_CHUNK == 0 and d % LANE == 0

    mod_all = _ada(jnp.concatenate([c_prompt, c_sample], axis=0), w_ada, b_ada).reshape(depth, bp + bs, 6, d)

    nb_s = _tile(bs, 32)
    cos_p, sin_p = _rope_tables(jnp.arange(tp, dtype=jnp.int32))
    cos_s, sin_s = _rope_tables(past_len + jnp.arange(ts, dtype=jnp.int32))
    rope_p = (cos_p, sin_p)
    rope_s = (jnp.tile(cos_s, (nb_s, 1)), jnp.tile(sin_s, (nb_s, 1)))

    tq = _tile(tp, 128)
    tk = _tile(tp, 256)
    cfg_p = dict(nb=1, tt=tk, nb_proj=1, tt_proj=_tile(tp, 1024), q_dtype=BF16, emit_vt=True, gdn_pad=False,
                 tc=_tile(tp, 128), pos0=0, nb_scan=1, tt_scan=_tile(tp, 256), nb_mm=1, tt_mm=_tile(tp, 512))
    cfg_s = dict(nb=nb_s, tt=ts, nb_proj=_tile(bs, 128), tt_proj=ts, q_dtype=F32, emit_vt=False, gdn_pad=True,
                 tc=GDN_CHUNK, pos0=past_len, nb_scan=nb_s, tt_scan=ts, nb_mm=_tile(bs, 64), tt_mm=ts)
    cache_kpe_t = jnp.swapaxes(cache_mla_kpe, 2, 3)

    xp, xs = x_prompt, x_sample
    out_p = {k: [] for k in STATE_KEYS}
    out_s = {k: [] for k in STATE_KEYS}
    for l in range(depth):
        lw = _prep_layer_weights(w_in[l], w_qb[l], w_kvb[l], w_lru_gate_a[l], w_lru_gate_x[l])
        p = dict(
            g_norm_mix=g_norm_mix[l][None], g_norm_ffn=g_norm_ffn[l][None], g_final=g_final[None],
            w_sc_conv=w_sc_conv[l], g_q_norm=g_q_norm[l][None], g_kv_norm=g_kv_norm[l][None],
            w_gdn_conv=w_gdn_conv[l],
            alog_b=jnp.repeat(gdn_a_log[l], HEAD_DIM)[None], dtb_b=jnp.repeat(gdn_dt_bias[l], HEAD_DIM)[None],
            gn_b=jnp.tile(g_gdn_norm[l], N_HEADS)[None],
            w_lru_conv=w_lru_conv[l], b_lru_conv=b_lru_conv[l][None],
            b_lru_gates=jnp.concatenate([b_lru_gate_a[l], b_lru_gate_x[l]])[None],
            nsl=lru_lambda[l][None],
            w_merge_gate=w_merge_gate[l].astype(BF16), w_branch_out=w_branch_out[l].astype(BF16),
            w_mix_out=w_mix_out[l].astype(BF16), w_ffn_in=w_ffn_in[l].astype(BF16),
            w_ffn_out=w_ffn_out[l].astype(BF16))
        final = l == depth - 1

        st_p = dict(sconv=jnp.zeros((bp, SUBLANE, c), F32), gdn_conv=jnp.zeros((bp, SUBLANE, 3 * c), F32),
                    gdn=jnp.zeros((bp, N_HEADS, HEAD_DIM, HEAD_DIM), F32),
                    lru_conv=jnp.zeros((bp, SUBLANE, c), F32), lru=jnp.zeros((bp, 1, c), F32))

        def attend_p(q_full, ckv, kpe, kf, vt, lw_):
            return _attn_prompt(q_full, kf, vt[0], lw_['wvt'], bp, tp, tq, tk)

        xp, nst_p = _group_layer(xp, mod_all[l, :bp], lw, p, st_p, rope_p, attend_p, dict(cfg_p, final=final))

        st_s = dict(sconv=_pad_state(state_sconv[l], SC_K), gdn_conv=_pad_state(state_gdn_conv[l], GDN_K),
                    gdn=state_gdn[l], lru_conv=_pad_state(state_lru_conv[l], LRU_K), lru=state_lru[l][:, None, :])

        def attend_s(q_full, ckv, kpe, kf, vt, lw_, layer=l):
            return _attn_sample(q_full, cache_mla_ckv, cache_kpe_t, page_table, layer, ckv, kpe, lw_['bdv'], bs, ts)

        xs, nst_s = _group_layer(xs, mod_all[l, bp:], lw, p, st_s, rope_s, attend_s, dict(cfg_s, final=final))
        for k in STATE_KEYS:
            out_p[k].append(nst_p[k])
            out_s[k].append(nst_s[k])

    return ((xp, xs) + tuple(jnp.stack(out_p[k]) for k in STATE_KEYS)
            + tuple(jnp.stack(out_s[k]) for k in STATE_KEYS))
```

```python
import functools
import math

import jax
import jax.numpy as jnp
from jax import lax
from jax.experimental import pallas as pl
from jax.experimental.pallas import tpu as pltpu

F32 = jnp.float32
BF16 = jnp.bfloat16

HEAD_DIM = 64
N_HEADS = 8
BRANCH_W = N_HEADS * HEAD_DIM
MLA_ROPE = HEAD_DIM // 2
MLA_QLORA = 256
MLA_KVLORA = 128
ROPE_THETA = 10000.0
SC_K = 3
GDN_K = 4
LRU_K = 4
LRU_C = 8.0
GDN_CHUNK = 64
EPS = 1e-6
NEG_BIG = -1e30
LANE = 128
SUBLANE = 8
QK_W = 2 * LANE

OFF_SC = 0
OFF_GQKV = 1536
OFF_GZ = 3072
OFF_GA = 3584
OFF_GB = 4096
OFF_LX = 4608
OFF_LG = 5120
OFF_QA = 5632
OFF_CKV = 5888
OFF_KPA = 6016
OFF_KPB = 6144
PROJ_W = 6272
PROJ_TN = 896


def _cparams(n_axes, vmem_mib):
    return pltpu.CompilerParams(dimension_semantics=("arbitrary",) * n_axes,
                                vmem_limit_bytes=vmem_mib * 1024 * 1024)


def _const_spec(shape):
    nd = len(shape)
    return pl.BlockSpec(shape, lambda *_: (0,) * nd, pipeline_mode=pl.Buffered(1))


def _bdot(a, b):
    return jnp.dot(a.astype(BF16), b.astype(BF16), preferred_element_type=F32)


def _bdot_nt(a, b):
    return lax.dot_general(a.astype(BF16), b.astype(BF16), (((1,), (1,)), ((), ())),
                           preferred_element_type=F32)


def _bdot_tn(a, b):
    return lax.dot_general(a.astype(BF16), b.astype(BF16), (((0,), (0,)), ((), ())),
                           preferred_element_type=F32)


def _silu(x):
    return x * jax.nn.sigmoid(x)


def _softplus(x):
    return jnp.maximum(x, 0.0) + jnp.log(1.0 + jnp.exp(-jnp.abs(x)))


def _gelu_tanh(x):
    return 0.5 * x * (1.0 + jnp.tanh(math.sqrt(2.0 / math.pi) * (x + 0.044715 * (x * x * x))))


def _rms(x, g):
    return x * lax.rsqrt(jnp.mean(x * x, axis=-1, keepdims=True) + EPS) * g


def _modulated_norm(x3, g, mod, shift_row, scale_row):
    y = _rms(x3, g)
    return y * (1.0 + mod[:, scale_row:scale_row + 1, :]) + mod[:, shift_row:shift_row + 1, :]


def _ada_kernel(c_ref, w_ref, b_ref, o_ref):
    o_ref[0] = _bdot(_silu(c_ref[...]), w_ref[0]) + b_ref[0]


def _ada(c_all, w_ada, b_ada):
    depth, d, n = w_ada.shape
    nb = c_all.shape[0]
    tn = n // 4
    return pl.pallas_call(
        _ada_kernel,
        grid=(depth, n // tn),
        in_specs=[pl.BlockSpec((nb, d), lambda l, j: (0, 0)),
                  pl.BlockSpec((1, d, tn), lambda l, j: (l, 0, j)),
                  pl.BlockSpec((1, 1, tn), lambda l, j: (l, 0, j))],
        out_specs=pl.BlockSpec((1, nb, tn), lambda l, j: (l, 0, j)),
        out_shape=jax.ShapeDtypeStruct((depth, nb, n), F32),
        compiler_params=_cparams(2, 40),
        name="ada_mod",
    )(c_all, w_ada, b_ada.reshape(depth, 1, n))


def _inproj_kernel(x_ref, mod_ref, g_ref, w_ref, o_ref, h_scr):
    @pl.when(pl.program_id(1) == 0)
    def _():
        h = _modulated_norm(x_ref[...], g_ref[...], mod_ref[...], 0, 1)
        h_scr[...] = h.reshape(h_scr.shape).astype(BF16)

    o_ref[...] = jnp.dot(h_scr[...], w_ref[...], preferred_element_type=F32)


def _inproj(x3, mod, g, w, nb, tt):
    nbat, t, d = x3.shape
    tper = t // tt
    tm = nb * tt
    n_m = (nbat // nb) * tper
    return pl.pallas_call(
        _inproj_kernel,
        grid=(n_m, PROJ_W // PROJ_TN),
        in_specs=[pl.BlockSpec((nb, tt, d), lambda i, j: (i // tper, i % tper, 0)),
                  pl.BlockSpec((nb, 6, d), lambda i, j: (i // tper, 0, 0)),
                  _const_spec((1, d)),
                  pl.BlockSpec((d, PROJ_TN), lambda i, j: (0, j))],
        out_specs=pl.BlockSpec((tm, PROJ_TN), lambda i, j: (i, j)),
        out_shape=jax.ShapeDtypeStruct((nbat * t, PROJ_W), F32),
        scratch_shapes=[pltpu.VMEM((tm, d), BF16)],
        compiler_params=_cparams(2, 40),
        name="in_proj",
    )(x3, mod, g, w)


def _causal_conv(u3, c8, w, k_w):
    nb, tt, c = u3.shape
    rows = lax.broadcasted_iota(jnp.int32, (nb, SUBLANE, c), 1)
    acc = None
    for j in range(k_w):
        s = k_w - 1 - j
        if s == 0:
            sh = u3
        else:
            full = pltpu.roll(u3, s, axis=1)
            top = jnp.where(rows < s, pltpu.roll(c8, s, axis=1), full[:, 0:SUBLANE, :])
            sh = top if tt == SUBLANE else jnp.concatenate([top, full[:, SUBLANE:, :]], axis=1)
        term = sh * w[j:j + 1, :]
        acc = term if acc is None else acc + term
    return acc


def _carry_in(t_idx, st_ref, c_scr):
    @pl.when(t_idx == 0)
    def _():
        c_scr[...] = st_ref[...]

    return c_scr[...]


def _sconv_kernel(bg_ref, cg_ref, xt_ref, st_ref, w_ref, y_ref, sto_ref, c_scr, *, nb, tt):
    c = bg_ref.shape[-1]
    c8 = _carry_in(pl.program_id(1), st_ref, c_scr)
    u3 = (cg_ref[...] * xt_ref[...]).reshape(nb, tt, c)
    conv = _causal_conv(u3, c8, w_ref[...], SC_K)
    y_ref[...] = bg_ref[...] * conv.reshape(nb * tt, c)
    last = u3[:, tt - SUBLANE:, :]
    c_scr[...] = last
    sto_ref[...] = last


def _tok_spec(tm, width, col_block, tper):
    return pl.BlockSpec((tm, width), lambda b, t: (b * tper + t, col_block))


def _state_spec(nb, c):
    return pl.BlockSpec((nb, SUBLANE, c), lambda b, t: (b, 0, 0))


def _sconv(proj, st8, w, nbat, t, nb, tt):
    tper = t // tt
    tm = nb * tt
    c = BRANCH_W
    kern = functools.partial(_sconv_kernel, nb=nb, tt=tt)
    return pl.pallas_call(
        kern,
        grid=(nbat // nb, tper),
        in_specs=[_tok_spec(tm, c, OFF_SC // c, tper), _tok_spec(tm, c, OFF_SC // c + 1, tper),
                  _tok_spec(tm, c, OFF_SC // c + 2, tper), _state_spec(nb, c), _const_spec((SC_K, c))],
        out_specs=[_tok_spec(tm, c, 0, tper), _state_spec(nb, c)],
        out_shape=[jax.ShapeDtypeStruct((nbat * t, c), F32), jax.ShapeDtypeStruct((nbat, SUBLANE, c), F32)],
        scratch_shapes=[pltpu.VMEM((nb, SUBLANE, c), F32)],
        compiler_params=_cparams(2, 40),
        name="short_conv",
    )(proj, proj, proj, st8, w)


def _gdn_pre_kernel(qkv_ref, ga_ref, gb_ref, st_ref, w_ref, alog_ref, dtb_ref,
                    o_ref, g_ref, beta_ref, sto_ref, c_scr, *, nb, tt):
    c = qkv_ref.shape[-1]
    c8 = _carry_in(pl.program_id(1), st_ref, c_scr)
    u3 = qkv_ref[...].reshape(nb, tt, c)
    o_ref[...] = _silu(_causal_conv(u3, c8, w_ref[...], GDN_K)).reshape(nb * tt, c)
    last = u3[:, tt - SUBLANE:, :]
    c_scr[...] = last
    sto_ref[...] = last
    g_ref[...] = -jnp.exp(alog_ref[...]) * _softplus(ga_ref[...] + dtb_ref[...])
    beta_ref[...] = jax.nn.sigmoid(gb_ref[...])


def _gdn_pre(proj, st8, w, alog_b, dtb_b, nbat, t, nb, tt):
    tper = t // tt
    tm = nb * tt
    c3 = 3 * BRANCH_W
    c = BRANCH_W
    kern = functools.partial(_gdn_pre_kernel, nb=nb, tt=tt)
    return pl.pallas_call(
        kern,
        grid=(nbat // nb, tper),
        in_specs=[_tok_spec(tm, c3, OFF_GQKV // c3, tper), _tok_spec(tm, c, OFF_GA // c, tper),
                  _tok_spec(tm, c, OFF_GB // c, tper), _state_spec(nb, c3), _const_spec((GDN_K, c3)),
                  _const_spec((1, c)), _const_spec((1, c))],
        out_specs=[_tok_spec(tm, c3, 0, tper), _tok_spec(tm, c, 0, tper), _tok_spec(tm, c, 0, tper),
                   _state_spec(nb, c3)],
        out_shape=[jax.ShapeDtypeStruct((nbat * t, c3), F32), jax.ShapeDtypeStruct((nbat * t, c), F32),
                   jax.ShapeDtypeStruct((nbat * t, c), F32), jax.ShapeDtypeStruct((nbat, SUBLANE, c3), F32)],
        scratch_shapes=[pltpu.VMEM((nb, SUBLANE, c3), F32)],
        compiler_params=_cparams(2, 48),
        name="gdn_pre",
    )(proj, proj, proj, st8, w, alog_b, dtb_b)


def _lru_kernel(lx_ref, lg_ref, st_ref, h0_ref, w_ref, cb_ref, wg_ref, bg_ref, lam_ref,
                y_ref, sto_ref, hl_ref, c_scr, h_scr, *, nb, tt, pos0):
    c = lx_ref.shape[-1]
    t_idx = pl.program_id(1)
    c8 = _carry_in(t_idx, st_ref, c_scr)

    @pl.when(t_idx == 0)
    def _():
        h_scr[...] = h0_ref[...]

    u3 = lx_ref[...].reshape(nb, tt, c)
    u = (_causal_conv(u3, c8, w_ref[...], LRU_K) + cb_ref[...]).reshape(nb * tt, c)
    last = u3[:, tt - SUBLANE:, :]
    c_scr[...] = last
    sto_ref[...] = last
    gates = jax.nn.sigmoid(_bdot(u, wg_ref[...]) + bg_ref[...])
    r = gates[:, :c]
    ig = gates[:, c:]
    log_a = -LRU_C * r * _softplus(-lam_ref[...])
    rows = lax.broadcasted_iota(jnp.int32, (nb, tt, c), 1)
    mult = jnp.sqrt(1.0 - jnp.exp(2.0 * log_a)).reshape(nb, tt, c)
    mult = jnp.where(pos0 + t_idx * tt + rows == 0, 1.0, mult)
    a = jnp.exp(log_a).reshape(nb, tt, c)
    b = mult * (ig * u).reshape(nb, tt, c)
    d = 1
    while d < tt:
        keep = rows >= d
        a_s = jnp.where(keep, pltpu.roll(a, d, axis=1), 1.0)
        b_s = jnp.where(keep, pltpu.roll(b, d, axis=1), 0.0)
        b = a * b_s + b
        a = a * a_s
        d *= 2
    hs = b + a * h_scr[...]
    h_last = hs[:, tt - 1:tt, :]
    h_scr[...] = h_last
    hl_ref[...] = h_last
    y_ref[...] = hs.reshape(nb * tt, c) * _gelu_tanh(lg_ref[...])


def _lru(proj, st8, h0, w, cb, wg, bg, lam, nbat, t, nb, tt, pos0):
    tper = t // tt
    tm = nb * tt
    c = BRANCH_W
    kern = functools.partial(_lru_kernel, nb=nb, tt=tt, pos0=pos0)
    h_spec = pl.BlockSpec((nb, 1, c), lambda b, t_: (b, 0, 0))
    return pl.pallas_call(
        kern,
        grid=(nbat // nb, tper),
        in_specs=[_tok_spec(tm, c, OFF_LX // c, tper), _tok_spec(tm, c, OFF_LG // c, tper), _state_spec(nb, c),
                  h_spec, _const_spec((LRU_K, c)), _const_spec((1, c)), _const_spec((c, 2 * c)),
                  _const_spec((1, 2 * c)), _const_spec((1, c))],
        out_specs=[_tok_spec(tm, c, 0, tper), _state_spec(nb, c), h_spec],
        out_shape=[jax.ShapeDtypeStruct((nbat * t, c), F32), jax.ShapeDtypeStruct((nbat, SUBLANE, c), F32),
                   jax.ShapeDtypeStruct((nbat, 1, c), F32)],
        scratch_shapes=[pltpu.VMEM((nb, SUBLANE, c), F32), pltpu.VMEM((nb, 1, c), F32)],
        compiler_params=_cparams(2, 40),
        name="lru",
    )(proj, proj, st8, h0, w, cb, wg, bg, lam)


def _mla_pre_kernel(qa_ref, ckv_ref, kpa_ref, kpb_ref, cos_ref, sin_ref, gq_ref, gkv_ref,
                    wn_ref, wpa_ref, wpb_ref, bdk_ref, q_out, ckv_out, kpe_out, kf_out, *rest, nb, tt):
    cos = cos_ref[...]
    sin = sin_ref[...]
    cq = _rms(qa_ref[...], gq_ref[...]).astype(BF16)
    qn = jnp.dot(cq, wn_ref[...], preferred_element_type=F32)
    qabs = _bdot(qn, bdk_ref[...])
    pa = jnp.dot(cq, wpa_ref[...], preferred_element_type=F32)
    pb = jnp.dot(cq, wpb_ref[...], preferred_element_type=F32)
    for h in range(N_HEADS):
        sl = slice(h * LANE, (h + 1) * LANE)
        q_pe = pa[:, sl] * cos + pb[:, sl] * sin
        q_out[:, h, :, 0:LANE] = qabs[:, sl].reshape(nb, tt, LANE).astype(q_out.dtype)
        q_out[:, h, :, LANE:QK_W] = q_pe.reshape(nb, tt, LANE).astype(q_out.dtype)
    ckv = _rms(ckv_ref[...], gkv_ref[...])
    kpe = kpa_ref[...] * cos + kpb_ref[...] * sin
    ckv_out[...] = ckv
    kpe_out[...] = kpe[:, :MLA_ROPE]
    kf_out[...] = jnp.concatenate([ckv, kpe], axis=-1).astype(BF16)
    if rest:
        rest[0][0] = ckv.T.astype(BF16)


def _mla_pre(proj, cos, sin, gq, gkv, wn, wpa, wpb, bdk, nbat, t, nb, tt, q_dtype, emit_vt):
    tper = t // tt
    tm = nb * tt
    hw = N_HEADS * LANE
    kern = functools.partial(_mla_pre_kernel, nb=nb, tt=tt)
    tab_spec = pl.BlockSpec((tm, LANE), lambda b, t_: (t_, 0))
    vt_specs, vt_shapes = [], []
    if emit_vt:
        assert nb == 1
        vt_specs = [pl.BlockSpec((1, MLA_KVLORA, tt), lambda b, t_: (b * tper + t_, 0, 0))]
        vt_shapes = [jax.ShapeDtypeStruct((nbat * tper, MLA_KVLORA, tt), BF16)]
    return pl.pallas_call(
        kern,
        grid=(nbat // nb, tper),
        in_specs=[_tok_spec(tm, MLA_QLORA, OFF_QA // MLA_QLORA, tper), _tok_spec(tm, LANE, OFF_CKV // LANE, tper),
                  _tok_spec(tm, LANE, OFF_KPA // LANE, tper), _tok_spec(tm, LANE, OFF_KPB // LANE, tper),
                  tab_spec, tab_spec, _const_spec((1, MLA_QLORA)), _const_spec((1, MLA_KVLORA)),
                  _const_spec((MLA_QLORA, BRANCH_W)), _const_spec((MLA_QLORA, hw)), _const_spec((MLA_QLORA, hw)),
                  _const_spec((BRANCH_W, hw))],
        out_specs=[pl.BlockSpec((nb, N_HEADS, tt, QK_W), lambda b, t_: (b, 0, t_, 0)),
                   _tok_spec(tm, MLA_KVLORA, 0, tper), _tok_spec(tm, MLA_ROPE, 0, tper), _tok_spec(tm, QK_W, 0, tper)]
                  + vt_specs,
        out_shape=[jax.ShapeDtypeStruct((nbat, N_HEADS, t, QK_W), q_dtype),
                   jax.ShapeDtypeStruct((nbat * t, MLA_KVLORA), F32),
                   jax.ShapeDtypeStruct((nbat * t, MLA_ROPE), F32),
                   jax.ShapeDtypeStruct((nbat * t, QK_W), BF16)] + vt_shapes,
        compiler_params=_cparams(2, 48),
        name="mla_pre",
    )(proj, proj, proj, proj, cos, sin, gq, gkv, wn, wpa, wpb, bdk)


def _heads_to_lanes(o, rows_per_head):
    return jnp.concatenate([o[h * rows_per_head:(h + 1) * rows_per_head] for h in range(N_HEADS)], axis=-1)


def _attn_prompt_kernel(q_ref, kf_ref, vt_ref, wvt_ref, y_ref, m_scr, l_scr, acc_scr, *, tq, tk, scale):
    i = pl.program_id(1)
    n_full = (i * tq) // tk
    w2 = 2 * tq
    krow = lax.broadcasted_iota(jnp.int32, (tk, w2), 0)
    qcol = i * tq + (lax.broadcasted_iota(jnp.int32, (tk, w2), 1) & (tq - 1))
    heads = range(N_HEADS // 2)
    m_scr[...] = jnp.full(m_scr.shape, NEG_BIG, F32)
    l_scr[...] = jnp.zeros(l_scr.shape, F32)
    acc_scr[...] = jnp.zeros(acc_scr.shape, F32)

    def step(j, masked):
        kblk = kf_ref[pl.ds(pl.multiple_of(j * tk, tk), tk), :]
        vt = vt_ref[j]
        s = [lax.dot_general(kblk, q_ref[0, 2 * h:2 * h + 2].reshape(w2, QK_W), (((1,), (1,)), ((), ())),
                             preferred_element_type=F32) * scale
             for h in heads]
        if masked:
            keep = j * tk + krow <= qcol
            s = [jnp.where(keep, s[h], NEG_BIG) for h in heads]
        m_old = [m_scr[h] for h in heads]
        m_new = [jnp.maximum(m_old[h], jnp.max(s[h], axis=0, keepdims=True)) for h in heads]
        p = [jnp.exp(s[h] - m_new[h]) for h in heads]
        alpha = [jnp.exp(m_old[h] - m_new[h]) for h in heads]
        pv = [jnp.dot(vt, p[h].astype(BF16), preferred_element_type=F32) for h in heads]
        for h in heads:
            m_scr[h] = m_new[h]
            l_scr[h] = alpha[h] * l_scr[h] + jnp.sum(p[h], axis=0, keepdims=True)
            acc_scr[h] = alpha[h] * acc_scr[h] + pv[h]

    def body(j, carry):
        step(j, False)
        return carry

    lax.fori_loop(0, n_full, body, 0)
    step(n_full, True)
    o = [acc_scr[h] / l_scr[h] for h in heads]
    outs = [_bdot(wvt_ref[hh], o[hh // 2][:, (hh % 2) * tq:(hh % 2 + 1) * tq]) for hh in range(N_HEADS)]
    y_ref[...] = jnp.concatenate(outs, axis=0).T


def _attn_prompt(q_full, kf, vt, wvt, nbat, t, tq, tk):
    tper = t // tq
    kper = t // tk
    scale = (HEAD_DIM + MLA_ROPE) ** -0.5
    kern = functools.partial(_attn_prompt_kernel, tq=tq, tk=tk, scale=scale)
    return pl.pallas_call(
        kern,
        grid=(nbat, tper),
        in_specs=[pl.BlockSpec((1, N_HEADS, tq, QK_W), lambda b, i: (b, 0, i, 0)),
                  pl.BlockSpec((t, QK_W), lambda b, i: (b, 0)),
                  pl.BlockSpec((kper, MLA_KVLORA, tk), lambda b, i: (b, 0, 0)),
                  _const_spec(wvt.shape)],
        out_specs=pl.BlockSpec((tq, BRANCH_W), lambda b, i: (b * tper + i, 0)),
        out_shape=jax.ShapeDtypeStruct((nbat * t, BRANCH_W), F32),
        scratch_shapes=[pltpu.VMEM((N_HEADS // 2, 1, 2 * tq), F32), pltpu.VMEM((N_HEADS // 2, 1, 2 * tq), F32),
                        pltpu.VMEM((N_HEADS // 2, MLA_KVLORA, 2 * tq), F32)],
        compiler_params=_cparams(2, 40),
        name="attn_prompt",
    )(q_full, kf, vt, wvt)


def _attn_sample_kernel(pt_ref, q_ref, *refs, n_pages, pg, ts, page, scale):
    ckv_pages = refs[:n_pages]
    kpe_pages = refs[n_pages:2 * n_pages]
    ckvn_ref, kpen_ref, bdv_ref, y_ref = refs[2 * n_pages:]
    r = N_HEADS * ts
    groups = range(n_pages // pg)

    q = q_ref[0].reshape(r, QK_W)
    q_lat = q[:, :MLA_KVLORA].astype(BF16)
    q_pe = q[:, MLA_KVLORA:MLA_KVLORA + MLA_ROPE].astype(BF16)

    c_all = [jnp.concatenate([ckv_pages[g * pg + i][0, 0].astype(BF16) for i in range(pg)], axis=0)
             for g in groups]
    kpt_all = [jnp.concatenate([kpe_pages[g * pg + i][0, 0].astype(BF16) for i in range(pg)], axis=1)
               for g in groups]
    s = [(lax.dot_general(q_lat, c_all[g], (((1,), (1,)), ((), ())), preferred_element_type=F32)
          + jnp.dot(q_pe, kpt_all[g], preferred_element_type=F32)) * scale for g in groups]
    pad = jnp.zeros((page - ts, MLA_KVLORA), F32)
    cn = jnp.concatenate([ckvn_ref[...], pad], axis=0).astype(BF16)
    kn = jnp.concatenate([kpen_ref[...], pad[:, :MLA_ROPE]], axis=0).astype(BF16)
    sn = (lax.dot_general(q_lat, cn, (((1,), (1,)), ((), ())), preferred_element_type=F32)
          + lax.dot_general(q_pe, kn, (((1,), (1,)), ((), ())), preferred_element_type=F32)) * scale
    qpos = lax.broadcasted_iota(jnp.int32, (r, page), 0) & (ts - 1)
    kpos = lax.broadcasted_iota(jnp.int32, (r, page), 1)
    s.append(jnp.where(kpos <= qpos, sn, NEG_BIG))
    vals = c_all + [cn]
    parts = range(len(s))
    m = [jnp.max(s[g], axis=-1, keepdims=True) for g in parts]
    p = [jnp.exp(s[g] - m[g]) for g in parts]
    l = [jnp.sum(p[g], axis=-1, keepdims=True) for g in parts]
    acc = [jnp.dot(p[g].astype(BF16), vals[g], preferred_element_type=F32) for g in parts]
    m_all = m[0]
    for g in parts[1:]:
        m_all = jnp.maximum(m_all, m[g])
    w = [jnp.exp(m[g] - m_all) for g in parts]
    l_all = w[0] * l[0]
    acc_all = w[0] * acc[0]
    for g in parts[1:]:
        l_all = l_all + w[g] * l[g]
        acc_all = acc_all + w[g] * acc[g]
    o = _heads_to_lanes(acc_all / l_all, ts)
    y_ref[...] = _bdot(o, bdv_ref[...])


def _attn_sample(q_full, cache_ckv, cache_kpe, page_table, layer, ckv_new, kpe_new, bdv, nbat, ts):
    n_pages = page_table.shape[1]
    page = cache_ckv.shape[2]
    pg = math.gcd(n_pages, 16)
    scale = (HEAD_DIM + MLA_ROPE) ** -0.5
    kern = functools.partial(_attn_sample_kernel, n_pages=n_pages, pg=pg, ts=ts, page=page, scale=scale)

    def page_spec(i, rows, width):
        return pl.BlockSpec((1, 1, rows, width), lambda b, pt: (layer, pt[b * n_pages + i], 0, 0))

    in_specs = ([pl.BlockSpec((1, N_HEADS, ts, QK_W), lambda b, pt: (b, 0, 0, 0))]
                + [page_spec(i, page, MLA_KVLORA) for i in range(n_pages)]
                + [page_spec(i, MLA_ROPE, page) for i in range(n_pages)]
                + [pl.BlockSpec((ts, MLA_KVLORA), lambda b, pt: (b, 0)),
                   pl.BlockSpec((ts, MLA_ROPE), lambda b, pt: (b, 0)),
                   pl.BlockSpec((N_HEADS * LANE, BRANCH_W), lambda b, pt: (0, 0))])
    grid_spec = pltpu.PrefetchScalarGridSpec(
        num_scalar_prefetch=1,
        grid=(nbat,),
        in_specs=in_specs,
        out_specs=pl.BlockSpec((ts, BRANCH_W), lambda b, pt: (b, 0)),
    )
    return pl.pallas_call(
        kern,
        grid_spec=grid_spec,
        out_shape=jax.ShapeDtypeStruct((nbat * ts, BRANCH_W), F32),
        compiler_params=_cparams(1, 40),
        name="attn_sample",
    )(page_table.reshape(-1), q_full, *([cache_ckv] * n_pages), *([cache_kpe] * n_pages), ckv_new, kpe_new, bdv)


def _cumsum_rows(x, n):
    rows = lax.broadcasted_iota(jnp.int32, x.shape, 0)
    d = 1
    while d < n:
        x = x + jnp.where(rows >= d, pltpu.roll(x, d, axis=0), 0.0)
        d *= 2
    return x


def _inverse_masks(n):
    rows = lax.broadcasted_iota(jnp.int32, (n, n), 0)
    cols = lax.broadcasted_iota(jnp.int32, (n, n), 1)
    masks = [rows == cols, (rows >> 1) == (cols >> 1)]
    k = 1
    while (1 << k) < n:
        masks.append(((rows >> k) ^ (cols >> k)) == 1)
        k += 1
    return masks


def _gdn_kernel(q_ref, k_ref, v_ref, g_ref, beta_ref, z_ref, s0_ref, gn_ref, y_ref, sf_ref, s_scr, *, n_chunks):
    cc = GDN_CHUNK
    n2 = 2 * cc
    npair = N_HEADS // 2
    zero_blk = jnp.zeros((HEAD_DIM, HEAD_DIM), F32)

    @pl.when(pl.program_id(1) == 0)
    def _():
        for pr in range(npair):
            top = jnp.concatenate([s0_ref[0, 2 * pr], zero_blk], axis=-1)
            bot = jnp.concatenate([zero_blk, s0_ref[0, 2 * pr + 1]], axis=-1)
            s_scr[pr] = jnp.concatenate([top, bot], axis=0)

    rows = lax.broadcasted_iota(jnp.int32, (n2, n2), 0)
    cols = lax.broadcasted_iota(jnp.int32, (n2, n2), 1)
    same = (rows >> 6) == (cols >> 6)
    tril = same & (rows >= cols)
    strict = same & (rows > cols)
    masks = _inverse_masks(n2)[:7]
    eye = jnp.where(masks[0], 1.0, 0.0)
    gn = gn_ref[...]
    probs = [(ci, pr) for ci in range(n_chunks) for pr in range(npair)]

    def rs(ci):
        return slice(ci * cc, (ci + 1) * cc)

    def ls(pr):
        return slice(pr * LANE, (pr + 1) * LANE)

    def stack(x):
        return jnp.where(same, jnp.concatenate([x, x], axis=0), 0.0)

    def twice(x):
        return jnp.concatenate([x, x], axis=0)

    gc_all = [_cumsum_rows(g_ref[rs(ci), :], cc) for ci in range(n_chunks)]
    gcm = [stack(gc_all[ci][:, ls(pr)]) for ci, pr in probs]
    beta = [twice(beta_ref[rs(ci), ls(pr)]) for ci, pr in probs]
    q = []
    k = []
    for ci, pr in probs:
        qr = stack(q_ref[rs(ci), ls(pr)])
        kr = stack(k_ref[rs(ci), ls(pr)])
        q.append(qr * lax.rsqrt(jnp.sum(qr * qr, axis=-1, keepdims=True) + EPS) * (HEAD_DIM ** -0.5))
        k.append(kr * lax.rsqrt(jnp.sum(kr * kr, axis=-1, keepdims=True) + EPS))
    kb = [k[p] * beta[p] for p in range(len(probs))]
    decay = []
    for p in range(len(probs)):
        diff = gcm[p] - gcm[p].T
        decay.append(jnp.where(tril, jnp.exp(jnp.where(tril, diff, 0.0)), 0.0))
    kk = [_bdot_nt(kb[p], k[p]) for p in range(len(probs))]
    a_mat = [jnp.where(strict, kk[p] * decay[p], 0.0) for p in range(len(probs))]
    t = [eye - jnp.where(masks[1], a_mat[p], 0.0) for p in range(len(probs))]
    for m in masks[2:]:
        tl = [_bdot(t[p], jnp.where(m, a_mat[p], 0.0)) for p in range(len(probs))]
        t = [t[p] - _bdot(tl[p], t[p]) for p in range(len(probs))]
    egc = [jnp.exp(gcm[p]) for p in range(len(probs))]
    sol = [_bdot(t[p], jnp.concatenate([stack(v_ref[rs(ci), ls(pr)]) * beta[p], kb[p] * egc[p]], axis=-1))
           for p, (ci, pr) in enumerate(probs)]
    qk = [_bdot_nt(q[p], k[p]) for p in range(len(probs))]
    intra = [jnp.where(tril, qk[p] * decay[p], 0.0) for p in range(len(probs))]
    qe = [q[p] * egc[p] for p in range(len(probs))]
    g_last = [gc_all[ci][cc - 1:cc, ls(pr)] for ci, pr in probs]
    kdec = [k[p] * jnp.exp(g_last[p] - gcm[p]) for p in range(len(probs))]
    sdec = [jnp.exp(g_last[p]) for p in range(len(probs))]

    for ci in range(n_chunks):
        ps = [ci * npair + pr for pr in range(npair)]
        s_old = [s_scr[pr] for pr in range(npair)]
        v_new = [sol[p][:, :n2] - _bdot(sol[p][:, n2:], s_old[pr]) for pr, p in enumerate(ps)]
        o_st = [_bdot(qe[p], s_old[pr]) for pr, p in enumerate(ps)]
        o_in = [_bdot(intra[p], v_new[pr]) for pr, p in enumerate(ps)]
        s_up = [_bdot_tn(kdec[p], v_new[pr]) for pr, p in enumerate(ps)]
        for pr, p in enumerate(ps):
            s_scr[pr] = s_old[pr] * sdec[p] + s_up[pr]
            o = o_st[pr] + o_in[pr]
            o = o * lax.rsqrt(jnp.sum(o * o, axis=-1, keepdims=True) * (1.0 / HEAD_DIM) + EPS)
            o = o[:cc] + o[cc:]
            y_ref[rs(ci), ls(pr)] = o * gn[:, ls(pr)] * _silu(z_ref[rs(ci), ls(pr)])

    @pl.when(pl.program_id(1) == pl.num_programs(1) - 1)
    def _():
        for pr in range(npair):
            s_pair = s_scr[pr]
            sf_ref[0, 2 * pr] = s_pair[:HEAD_DIM, :HEAD_DIM]
            sf_ref[0, 2 * pr + 1] = s_pair[HEAD_DIM:, HEAD_DIM:]


def _gdn(qkvc, g_b, beta_b, proj, s0, gn_b, nbat, t, tc):
    tper = t // tc
    c = BRANCH_W
    kern = functools.partial(_gdn_kernel, n_chunks=tc // GDN_CHUNK)
    s_spec = pl.BlockSpec((1, N_HEADS, HEAD_DIM, HEAD_DIM), lambda b, t_: (b, 0, 0, 0))
    return pl.pallas_call(
        kern,
        grid=(nbat, tper),
        in_specs=[_tok_spec(tc, c, 0, tper), _tok_spec(tc, c, 1, tper), _tok_spec(tc, c, 2, tper),
                  _tok_spec(tc, c, 0, tper), _tok_spec(tc, c, 0, tper), _tok_spec(tc, c, OFF_GZ // c, tper),
                  s_spec, _const_spec((1, c))],
        out_specs=[_tok_spec(tc, c, 0, tper), s_spec],
        out_shape=[jax.ShapeDtypeStruct((nbat * t, c), F32),
                   jax.ShapeDtypeStruct((nbat, N_HEADS, HEAD_DIM, HEAD_DIM), F32)],
        scratch_shapes=[pltpu.VMEM((N_HEADS // 2, 2 * HEAD_DIM, 2 * HEAD_DIM), F32)],
        compiler_params=_cparams(2, 40),
        name="gdn_rule",
    )(qkvc, qkvc, qkvc, g_b, beta_b, proj, s0, gn_b)


def _merge_kernel(x_ref, mod_ref, g_ref, ya_ref, yb_ref, yc_ref, yd_ref, wg_ref, wbo_ref, wmix_ref, o_ref):
    x = x_ref[...]
    nb, tt, d = x.shape
    mod = mod_ref[...]
    h = _modulated_norm(x, g_ref[...], mod, 0, 1).reshape(nb * tt, d).astype(BF16)
    acc = None
    for n, y_ref in enumerate((ya_ref, yb_ref, yc_ref, yd_ref)):
        gate = jax.nn.sigmoid(jnp.dot(h, wg_ref[:, n * d:(n + 1) * d], preferred_element_type=F32))
        term = gate * _bdot(y_ref[...], wbo_ref[n])
        acc = term if acc is None else acc + term
    mix = _bdot(acc, wmix_ref[...])
    o_ref[...] = x + mod[:, 2:3, :] * mix.reshape(nb, tt, d)


def _merge(x3, mod, g, ys, wg, wbo, wmix, nb, tt):
    nbat, t, d = x3.shape
    tper = t // tt
    tm = nb * tt
    c = BRANCH_W
    x_spec = pl.BlockSpec((nb, tt, d), lambda b, t_: (b, t_, 0))
    return pl.pallas_call(
        _merge_kernel,
        grid=(nbat // nb, tper),
        in_specs=[x_spec, pl.BlockSpec((nb, 6, d), lambda b, t_: (b, 0, 0)), _const_spec((1, d))]
                 + [_tok_spec(tm, c, 0, tper)] * 4
                 + [_const_spec(wg.shape), _const_spec(wbo.shape), _const_spec(wmix.shape)],
        out_specs=x_spec,
        out_shape=jax.ShapeDtypeStruct(x3.shape, F32),
        compiler_params=_cparams(2, 48),
        name="merge",
    )(x3, mod, g, *ys, wg, wbo, wmix)


def _ffn_kernel(x_ref, mod_ref, g_ref, win_ref, wout_ref, gf_ref, o_ref, *, final):
    x = x_ref[...]
    nb, tt, d = x.shape
    mod = mod_ref[...]
    fh = wout_ref.shape[0]
    h = _modulated_norm(x, g_ref[...], mod, 3, 4).reshape(nb * tt, d).astype(BF16)
    gate = jnp.dot(h, win_ref[:, :fh], preferred_element_type=F32)
    up = jnp.dot(h, win_ref[:, fh:], preferred_element_type=F32)
    out = _bdot(_silu(gate) * up, wout_ref[...])
    xo = x + mod[:, 5:6, :] * out.reshape(nb, tt, d)
    o_ref[...] = _rms(xo, gf_ref[...]) if final else xo


def _ffn(x3, mod, g, win, wout, gf, nb, tt, final):
    nbat, t, d = x3.shape
    x_spec = pl.BlockSpec((nb, tt, d), lambda b, t_: (b, t_, 0))
    kern = functools.partial(_ffn_kernel, final=final)
    return pl.pallas_call(
        kern,
        grid=(nbat // nb, t // tt),
        in_specs=[x_spec, pl.BlockSpec((nb, 6, d), lambda b, t_: (b, 0, 0)), _const_spec((1, d)),
                  _const_spec(win.shape), _const_spec(wout.shape), _const_spec((1, d))],
        out_specs=x_spec,
        out_shape=jax.ShapeDtypeStruct(x3.shape, F32),
        compiler_params=_cparams(2, 56),
        name="ffn",
    )(x3, mod, g, win, wout, gf)


def _block_diag(blocks):
    h, r, c = blocks.shape
    eye = jnp.eye(h, dtype=blocks.dtype)
    return jnp.einsum('hrc,hg->hrgc', blocks, eye).reshape(h * r, h * c)


def _prep_layer_weights(w_in, w_qb, w_kvb, w_lru_gate_a, w_lru_gate_x):
    d = w_in.shape[0]
    c = BRANCH_W
    o = 0
    segs = {}
    for name, width in (('sc', 3 * c), ('qa', MLA_QLORA), ('ckv', MLA_KVLORA), ('kpe', MLA_ROPE), ('gqkv', 3 * c),
                        ('gz', c), ('ga', N_HEADS), ('gb', N_HEADS), ('lx', c), ('lg', c)):
        segs[name] = w_in[:, o:o + width]
        o += width
    half = MLA_ROPE // 2
    zpad = jnp.zeros((d, LANE - MLA_ROPE), F32)
    kpe = segs['kpe']
    kpa = jnp.concatenate([kpe, zpad], axis=1)
    kpb = jnp.concatenate([-kpe[:, half:], kpe[:, :half], zpad], axis=1)
    w_proj = jnp.concatenate(
        [segs['sc'], segs['gqkv'], segs['gz'], jnp.repeat(segs['ga'], HEAD_DIM, axis=1),
         jnp.repeat(segs['gb'], HEAD_DIM, axis=1), segs['lx'], segs['lg'], segs['qa'], segs['ckv'], kpa, kpb],
        axis=1).astype(BF16)

    ql = w_qb.shape[0]
    wq = w_qb.reshape(ql, N_HEADS, HEAD_DIM + MLA_ROPE)
    wn = wq[:, :, :HEAD_DIM].reshape(ql, c)
    pe = wq[:, :, HEAD_DIM:]
    zq = jnp.zeros((ql, N_HEADS, LANE - MLA_ROPE), F32)
    wpa = jnp.concatenate([pe, zq], axis=2).reshape(ql, N_HEADS * LANE)
    wpb = jnp.concatenate([-pe[:, :, half:], pe[:, :, :half], zq], axis=2).reshape(ql, N_HEADS * LANE)
    w_kb = w_kvb[:, :, :HEAD_DIM]
    w_vb = w_kvb[:, :, HEAD_DIM:]
    bdk = _block_diag(jnp.transpose(w_kb, (1, 2, 0)))
    bdv = _block_diag(jnp.transpose(w_vb, (1, 0, 2)))
    wlg = jnp.concatenate([_block_diag(w_lru_gate_a), _block_diag(w_lru_gate_x)], axis=1)
    wvt = jnp.transpose(w_vb, (1, 2, 0))
    return dict(w_proj=w_proj, wn=wn.astype(BF16), wpa=wpa.astype(BF16), wpb=wpb.astype(BF16),
                bdk=bdk.astype(BF16), bdv=bdv.astype(BF16), wvt=wvt.astype(BF16), wlg=wlg.astype(BF16))


def _rope_tables(pos):
    half = MLA_ROPE // 2
    inv = ROPE_THETA ** (-jnp.arange(half, dtype=F32) / half)
    ang = pos.astype(F32)[:, None] * inv[None, :]
    pad = jnp.zeros((pos.shape[0], LANE - MLA_ROPE), F32)
    cos = jnp.concatenate([jnp.cos(ang), jnp.cos(ang), pad], axis=1)
    sin = jnp.concatenate([jnp.sin(ang), jnp.sin(ang), pad], axis=1)
    return cos, sin


def _pad_state(st, k_w):
    return jnp.pad(st, ((0, 0), (SUBLANE - (k_w - 1), 0), (0, 0)))


def _tile(t, pref):
    tt = min(t, pref)
    while t % tt:
        tt //= 2
    return tt


def _group_layer(x3, mod, lw, p, st, rope, attend, cfg):
    nbat, t, d = x3.shape
    nb, tt = cfg['nb'], cfg['tt']
    proj = _inproj(x3, mod, p['g_norm_mix'], lw['w_proj'], cfg['nb_proj'], cfg['tt_proj'])
    y_a, sc8 = _sconv(proj, st['sconv'], p['w_sc_conv'], nbat, t, nb, tt)
    q_full, ckv, kpe, kf, *vt = _mla_pre(proj, rope[0], rope[1], p['g_q_norm'], p['g_kv_norm'], lw['wn'], lw['wpa'],
                                         lw['wpb'], lw['bdk'], nbat, t, nb, tt, cfg['q_dtype'], cfg['emit_vt'])
    y_b = attend(q_full, ckv, kpe, kf, vt, lw)
    qkvc, g_b, beta_b, gc8 = _gdn_pre(proj, st['gdn_conv'], p['w_gdn_conv'], p['alog_b'], p['dtb_b'], nbat, t, nb, tt)
    if cfg['gdn_pad']:
        padn = GDN_CHUNK - t

        def padt(a):
            return jnp.pad(a.reshape(nbat, t, -1), ((0, 0), (0, padn), (0, 0))).reshape(nbat * GDN_CHUNK, -1)

        y_c, s_gdn = _gdn(padt(qkvc), padt(g_b), padt(beta_b), padt(proj[:, :OFF_GA]), st['gdn'], p['gn_b'],
                          nbat, GDN_CHUNK, GDN_CHUNK)
        y_c = y_c.reshape(nbat, GDN_CHUNK, -1)[:, :t].reshape(nbat * t, -1)
    else:
        y_c, s_gdn = _gdn(qkvc, g_b, beta_b, proj, st['gdn'], p['gn_b'], nbat, t, cfg['tc'])
    y_d, lc8, h_last = _lru(proj, st['lru_conv'], st['lru'], p['w_lru_conv'], p['b_lru_conv'], lw['wlg'],
                            p['b_lru_gates'], p['nsl'], nbat, t, cfg['nb_scan'], cfg['tt_scan'], cfg['pos0'])
    x1 = _merge(x3, mod, p['g_norm_mix'], (y_a, y_b, y_c, y_d), p['w_merge_gate'], p['w_branch_out'],
                p['w_mix_out'], cfg['nb_mm'], cfg['tt_mm'])
    x2 = _ffn(x1, mod, p['g_norm_ffn'], p['w_ffn_in'], p['w_ffn_out'], p['g_final'], cfg['nb_mm'], cfg['tt_mm'],
              cfg['final'])
    new_st = dict(ckv=ckv.reshape(nbat, t, -1), kpe=kpe.reshape(nbat, t, -1),
                  sconv=sc8[:, SUBLANE - (SC_K - 1):], gdn_conv=gc8[:, SUBLANE - (GDN_K - 1):], gdn=s_gdn,
                  lru_conv=lc8[:, SUBLANE - (LRU_K - 1):], lru=h_last[:, 0, :])
    return x2, new_st


STATE_KEYS = ('ckv', 'kpe', 'sconv', 'gdn_conv', 'gdn', 'lru_conv', 'lru')


def kernel(x_prompt, x_sample, c_prompt, c_sample, cache_mla_ckv, cache_mla_kpe, page_table, state_sconv, state_gdn_conv, state_gdn, state_lru_conv, state_lru, w_ada, b_ada, g_norm_mix, g_norm_ffn, w_in, w_sc_conv, g_q_norm, w_qb, g_kv_norm, w_kvb, w_gdn_conv, gdn_a_log, gdn_dt_bias, g_gdn_norm, w_lru_conv, b_lru_conv, w_lru_gate_a, b_lru_gate_a, w_lru_gate_x, b_lru_gate_x, lru_lambda, w_branch_out, w_merge_gate, w_mix_out, w_ffn_in, w_ffn_out, g_final):
    bp, tp, d = x_prompt.shape
    bs, ts, _ = x_sample.shape
    depth = w_in.shape[0]
    n_pages = page_table.shape[1]
    past_len = n_pages * cache_mla_ckv.shape[2]
    c = BRANCH_W
    assert ts == SUBLANE and tp % GDN_CHUNK == 0 and d % LANE == 0

    mod_all = _ada(jnp.concatenate([c_prompt, c_sample], axis=0), w_ada, b_ada).reshape(depth, bp + bs, 6, d)

    nb_s = _tile(bs, 32)
    cos_p, sin_p = _rope_tables(jnp.arange(tp, dtype=jnp.int32))
    cos_s, sin_s = _rope_tables(past_len + jnp.arange(ts, dtype=jnp.int32))
    rope_p = (cos_p, sin_p)
    rope_s = (jnp.tile(cos_s, (nb_s, 1)), jnp.tile(sin_s, (nb_s, 1)))

    tq = _tile(tp, 128)
    tk = _tile(tp, 256)
    cfg_p = dict(nb=1, tt=tk, nb_proj=1, tt_proj=_tile(tp, 1024), q_dtype=BF16, emit_vt=True, gdn_pad=False,
                 tc=_tile(tp, 256), pos0=0, nb_scan=1, tt_scan=_tile(tp, 256), nb_mm=1, tt_mm=_tile(tp, 512))
    cfg_s = dict(nb=nb_s, tt=ts, nb_proj=_tile(bs, 128), tt_proj=ts, q_dtype=F32, emit_vt=False, gdn_pad=True,
                 tc=GDN_CHUNK, pos0=past_len, nb_scan=nb_s, tt_scan=ts, nb_mm=_tile(bs, 64), tt_mm=ts)
    cache_kpe_t = jnp.swapaxes(cache_mla_kpe, 2, 3)

    xp, xs = x_prompt, x_sample
    out_p = {k: [] for k in STATE_KEYS}
    out_s = {k: [] for k in STATE_KEYS}
    for l in range(depth):
        lw = _prep_layer_weights(w_in[l], w_qb[l], w_kvb[l], w_lru_gate_a[l], w_lru_gate_x[l])
        p = dict(
            g_norm_mix=g_norm_mix[l][None], g_norm_ffn=g_norm_ffn[l][None], g_final=g_final[None],
            w_sc_conv=w_sc_conv[l], g_q_norm=g_q_norm[l][None], g_kv_norm=g_kv_norm[l][None],
            w_gdn_conv=w_gdn_conv[l],
            alog_b=jnp.repeat(gdn_a_log[l], HEAD_DIM)[None], dtb_b=jnp.repeat(gdn_dt_bias[l], HEAD_DIM)[None],
            gn_b=jnp.tile(g_gdn_norm[l], N_HEADS)[None],
            w_lru_conv=w_lru_conv[l], b_lru_conv=b_lru_conv[l][None],
            b_lru_gates=jnp.concatenate([b_lru_gate_a[l], b_lru_gate_x[l]])[None],
            nsl=lru_lambda[l][None],
            w_merge_gate=w_merge_gate[l].astype(BF16), w_branch_out=w_branch_out[l].astype(BF16),
            w_mix_out=w_mix_out[l].astype(BF16), w_ffn_in=w_ffn_in[l].astype(BF16),
            w_ffn_out=w_ffn_out[l].astype(BF16))
        final = l == depth - 1

        st_p = dict(sconv=jnp.zeros((bp, SUBLANE, c), F32), gdn_conv=jnp.zeros((bp, SUBLANE, 3 * c), F32),
                    gdn=jnp.zeros((bp, N_HEADS, HEAD_DIM, HEAD_DIM), F32),
                    lru_conv=jnp.zeros((bp, SUBLANE, c), F32), lru=jnp.zeros((bp, 1, c), F32))

        def attend_p(q_full, ckv, kpe, kf, vt, lw_):
            return _attn_prompt(q_full, kf, vt[0], lw_['wvt'], bp, tp, tq, tk)

        xp, nst_p = _group_layer(xp, mod_all[l, :bp], lw, p, st_p, rope_p, attend_p, dict(cfg_p, final=final))

        st_s = dict(sconv=_pad_state(state_sconv[l], SC_K), gdn_conv=_pad_state(state_gdn_conv[l], GDN_K),
                    gdn=state_gdn[l], lru_conv=_pad_state(state_lru_conv[l], LRU_K), lru=state_lru[l][:, None, :])

        def attend_s(q_full, ckv, kpe, kf, vt, lw_, layer=l):
            return _attn_sample(q_full, cache_mla_ckv, cache_kpe_t, page_table, layer, ckv, kpe, lw_['bdv'], bs, ts)

        xs, nst_s = _group_layer(xs, mod_all[l, bp:], lw, p, st_s, rope_s, attend_s, dict(cfg_s, final=final))
        for k in STATE_KEYS:
            out_p[k].append(nst_p[k])
            out_s[k].append(nst_s[k])

    return ((xp, xs) + tuple(jnp.stack(out_p[k]) for k in STATE_KEYS)
            + tuple(jnp.stack(out_s[k]) for k in STATE_KEYS))
```

```python
import functools
import math

import jax
import jax.numpy as jnp
from jax import lax
from jax.experimental import pallas as pl
from jax.experimental.pallas import tpu as pltpu

F32 = jnp.float32
BF16 = jnp.bfloat16

HEAD_DIM = 64
N_HEADS = 8
BRANCH_W = N_HEADS * HEAD_DIM
MLA_ROPE = HEAD_DIM // 2
MLA_QLORA = 256
MLA_KVLORA = 128
ROPE_THETA = 10000.0
SC_K = 3
GDN_K = 4
LRU_K = 4
LRU_C = 8.0
GDN_CHUNK = 64
EPS = 1e-6
NEG_BIG = -1e30
LANE = 128
SUBLANE = 8
QK_W = 2 * LANE
ATTN_GROUP_W = 256

OFF_SC = 0
OFF_GQKV = 1536
OFF_GZ = 3072
OFF_GA = 3584
OFF_GB = 4096
OFF_LX = 4608
OFF_LG = 5120
OFF_QA = 5632
OFF_CKV = 5888
OFF_KPA = 6016
OFF_KPB = 6144
PROJ_W = 6272
PROJ_TN = 896


def _cparams(n_axes, vmem_mib):
    return pltpu.CompilerParams(dimension_semantics=("arbitrary",) * n_axes,
                                vmem_limit_bytes=vmem_mib * 1024 * 1024)


def _const_spec(shape):
    nd = len(shape)
    return pl.BlockSpec(shape, lambda *_: (0,) * nd, pipeline_mode=pl.Buffered(1))


def _bdot(a, b):
    return jnp.dot(a.astype(BF16), b.astype(BF16), preferred_element_type=F32)


def _bdot_nt(a, b):
    return lax.dot_general(a.astype(BF16), b.astype(BF16), (((1,), (1,)), ((), ())),
                           preferred_element_type=F32)


def _bdot_tn(a, b):
    return lax.dot_general(a.astype(BF16), b.astype(BF16), (((0,), (0,)), ((), ())),
                           preferred_element_type=F32)


def _silu(x):
    return x * jax.nn.sigmoid(x)


def _softplus(x):
    return jnp.maximum(x, 0.0) + jnp.log(1.0 + jnp.exp(-jnp.abs(x)))


def _gelu_tanh(x):
    return 0.5 * x * (1.0 + jnp.tanh(math.sqrt(2.0 / math.pi) * (x + 0.044715 * (x * x * x))))


def _rms(x, g):
    return x * lax.rsqrt(jnp.mean(x * x, axis=-1, keepdims=True) + EPS) * g


def _modulated_norm(x3, g, mod, shift_row, scale_row):
    y = _rms(x3, g)
    return y * (1.0 + mod[:, scale_row:scale_row + 1, :]) + mod[:, shift_row:shift_row + 1, :]


def _ada_kernel(c_ref, w_ref, b_ref, o_ref):
    o_ref[0] = _bdot(_silu(c_ref[...]), w_ref[0]) + b_ref[0]


def _ada(c_all, w_ada, b_ada):
    depth, d, n = w_ada.shape
    nb = c_all.shape[0]
    tn = n // 4
    return pl.pallas_call(
        _ada_kernel,
        grid=(depth, n // tn),
        in_specs=[pl.BlockSpec((nb, d), lambda l, j: (0, 0)),
                  pl.BlockSpec((1, d, tn), lambda l, j: (l, 0, j)),
                  pl.BlockSpec((1, 1, tn), lambda l, j: (l, 0, j))],
        out_specs=pl.BlockSpec((1, nb, tn), lambda l, j: (l, 0, j)),
        out_shape=jax.ShapeDtypeStruct((depth, nb, n), F32),
        compiler_params=_cparams(2, 40),
        name="ada_mod",
    )(c_all, w_ada, b_ada.reshape(depth, 1, n))


def _inproj_kernel(x_ref, mod_ref, g_ref, w_ref, o_ref, h_scr):
    @pl.when(pl.program_id(1) == 0)
    def _():
        h = _modulated_norm(x_ref[...], g_ref[...], mod_ref[...], 0, 1)
        h_scr[...] = h.reshape(h_scr.shape).astype(BF16)

    o_ref[...] = jnp.dot(h_scr[...], w_ref[...], preferred_element_type=F32).astype(o_ref.dtype)


def _inproj(x3, mod, g, w, nb, tt):
    nbat, t, d = x3.shape
    tper = t // tt
    tm = nb * tt
    n_m = (nbat // nb) * tper
    return pl.pallas_call(
        _inproj_kernel,
        grid=(n_m, PROJ_W // PROJ_TN),
        in_specs=[pl.BlockSpec((nb, tt, d), lambda i, j: (i // tper, i % tper, 0)),
                  pl.BlockSpec((nb, 6, d), lambda i, j: (i // tper, 0, 0)),
                  _const_spec((1, d)),
                  pl.BlockSpec((d, PROJ_TN), lambda i, j: (0, j))],
        out_specs=pl.BlockSpec((tm, PROJ_TN), lambda i, j: (i, j)),
        out_shape=jax.ShapeDtypeStruct((nbat * t, PROJ_W), BF16),
        scratch_shapes=[pltpu.VMEM((tm, d), BF16)],
        compiler_params=_cparams(2, 40),
        name="in_proj",
    )(x3, mod, g, w)


def _causal_conv(u3, c8, w, k_w):
    nb, tt, c = u3.shape
    rows = lax.broadcasted_iota(jnp.int32, (nb, SUBLANE, c), 1)
    acc = None
    for j in range(k_w):
        s = k_w - 1 - j
        if s == 0:
            sh = u3
        else:
            full = pltpu.roll(u3, s, axis=1)
            top = jnp.where(rows < s, pltpu.roll(c8, s, axis=1), full[:, 0:SUBLANE, :])
            sh = top if tt == SUBLANE else jnp.concatenate([top, full[:, SUBLANE:, :]], axis=1)
        term = sh * w[j:j + 1, :]
        acc = term if acc is None else acc + term
    return acc


def _carry_in(t_idx, st_ref, c_scr):
    @pl.when(t_idx == 0)
    def _():
        c_scr[...] = st_ref[...]

    return c_scr[...]


def _sconv_kernel(bg_ref, cg_ref, xt_ref, st_ref, w_ref, y_ref, sto_ref, c_scr, *, nb, tt):
    c = bg_ref.shape[-1]
    c8 = _carry_in(pl.program_id(1), st_ref, c_scr)
    u3 = (cg_ref[...].astype(F32) * xt_ref[...].astype(F32)).reshape(nb, tt, c)
    conv = _causal_conv(u3, c8, w_ref[...], SC_K)
    y_ref[...] = bg_ref[...].astype(F32) * conv.reshape(nb * tt, c)
    last = u3[:, tt - SUBLANE:, :]
    c_scr[...] = last
    sto_ref[...] = last


def _tok_spec(tm, width, col_block, tper):
    return pl.BlockSpec((tm, width), lambda b, t: (b * tper + t, col_block))


def _state_spec(nb, c):
    return pl.BlockSpec((nb, SUBLANE, c), lambda b, t: (b, 0, 0))


def _sconv(proj, st8, w, nbat, t, nb, tt):
    tper = t // tt
    tm = nb * tt
    c = BRANCH_W
    kern = functools.partial(_sconv_kernel, nb=nb, tt=tt)
    return pl.pallas_call(
        kern,
        grid=(nbat // nb, tper),
        in_specs=[_tok_spec(tm, c, OFF_SC // c, tper), _tok_spec(tm, c, OFF_SC // c + 1, tper),
                  _tok_spec(tm, c, OFF_SC // c + 2, tper), _state_spec(nb, c), _const_spec((SC_K, c))],
        out_specs=[_tok_spec(tm, c, 0, tper), _state_spec(nb, c)],
        out_shape=[jax.ShapeDtypeStruct((nbat * t, c), F32), jax.ShapeDtypeStruct((nbat, SUBLANE, c), F32)],
        scratch_shapes=[pltpu.VMEM((nb, SUBLANE, c), F32)],
        compiler_params=_cparams(2, 40),
        name="short_conv",
    )(proj, proj, proj, st8, w)


def _lru_kernel(lx_ref, lg_ref, st_ref, h0_ref, w_ref, cb_ref, wg_ref, bg_ref, lam_ref,
                y_ref, sto_ref, hl_ref, c_scr, h_scr, *, nb, tt, pos0):
    c = lx_ref.shape[-1]
    t_idx = pl.program_id(1)
    c8 = _carry_in(t_idx, st_ref, c_scr)

    @pl.when(t_idx == 0)
    def _():
        h_scr[...] = h0_ref[...]

    u3 = lx_ref[...].astype(F32).reshape(nb, tt, c)
    u = (_causal_conv(u3, c8, w_ref[...], LRU_K) + cb_ref[...]).reshape(nb * tt, c)
    last = u3[:, tt - SUBLANE:, :]
    c_scr[...] = last
    sto_ref[...] = last
    gates = jax.nn.sigmoid(_bdot(u, wg_ref[...]) + bg_ref[...])
    r = gates[:, :c]
    ig = gates[:, c:]
    log_a = -LRU_C * r * _softplus(-lam_ref[...])
    rows = lax.broadcasted_iota(jnp.int32, (nb, tt, c), 1)
    mult = jnp.sqrt(1.0 - jnp.exp(2.0 * log_a)).reshape(nb, tt, c)
    mult = jnp.where(pos0 + t_idx * tt + rows == 0, 1.0, mult)
    a = jnp.exp(log_a).reshape(nb, tt, c)
    b = mult * (ig * u).reshape(nb, tt, c)
    d = 1
    while d < tt:
        keep = rows >= d
        a_s = jnp.where(keep, pltpu.roll(a, d, axis=1), 1.0)
        b_s = jnp.where(keep, pltpu.roll(b, d, axis=1), 0.0)
        b = a * b_s + b
        a = a * a_s
        d *= 2
    hs = b + a * h_scr[...]
    h_last = hs[:, tt - 1:tt, :]
    h_scr[...] = h_last
    hl_ref[...] = h_last
    y_ref[...] = hs.reshape(nb * tt, c) * _gelu_tanh(lg_ref[...].astype(F32))


def _lru(proj, st8, h0, w, cb, wg, bg, lam, nbat, t, nb, tt, pos0):
    tper = t // tt
    tm = nb * tt
    c = BRANCH_W
    kern = functools.partial(_lru_kernel, nb=nb, tt=tt, pos0=pos0)
    h_spec = pl.BlockSpec((nb, 1, c), lambda b, t_: (b, 0, 0))
    return pl.pallas_call(
        kern,
        grid=(nbat // nb, tper),
        in_specs=[_tok_spec(tm, c, OFF_LX // c, tper), _tok_spec(tm, c, OFF_LG // c, tper), _state_spec(nb, c),
                  h_spec, _const_spec((LRU_K, c)), _const_spec((1, c)), _const_spec((c, 2 * c)),
                  _const_spec((1, 2 * c)), _const_spec((1, c))],
        out_specs=[_tok_spec(tm, c, 0, tper), _state_spec(nb, c), h_spec],
        out_shape=[jax.ShapeDtypeStruct((nbat * t, c), F32), jax.ShapeDtypeStruct((nbat, SUBLANE, c), F32),
                   jax.ShapeDtypeStruct((nbat, 1, c), F32)],
        scratch_shapes=[pltpu.VMEM((nb, SUBLANE, c), F32), pltpu.VMEM((nb, 1, c), F32)],
        compiler_params=_cparams(2, 40),
        name="lru",
    )(proj, proj, st8, h0, w, cb, wg, bg, lam)


def _mla_pre_kernel(qa_ref, ckv_ref, kpa_ref, kpb_ref, cos_ref, sin_ref, gq_ref, gkv_ref,
                    wn_ref, wpa_ref, wpb_ref, bdk_ref, q_out, ckv_out, kpe_out, kf_out, *rest, nb, tt):
    cos = cos_ref[...]
    sin = sin_ref[...]
    cq = _rms(qa_ref[...].astype(F32), gq_ref[...]).astype(BF16)
    qn = jnp.dot(cq, wn_ref[...], preferred_element_type=F32)
    qabs = _bdot(qn, bdk_ref[...])
    pa = jnp.dot(cq, wpa_ref[...], preferred_element_type=F32)
    pb = jnp.dot(cq, wpb_ref[...], preferred_element_type=F32)
    for h in range(N_HEADS):
        sl = slice(h * LANE, (h + 1) * LANE)
        q_pe = pa[:, sl] * cos + pb[:, sl] * sin
        q_out[:, h, :, 0:LANE] = qabs[:, sl].reshape(nb, tt, LANE).astype(q_out.dtype)
        q_out[:, h, :, LANE:QK_W] = q_pe.reshape(nb, tt, LANE).astype(q_out.dtype)
    ckv = _rms(ckv_ref[...].astype(F32), gkv_ref[...])
    kpe = kpa_ref[...].astype(F32) * cos + kpb_ref[...].astype(F32) * sin
    ckv_out[...] = ckv
    kpe_out[...] = kpe[:, :MLA_ROPE]
    kf_out[...] = jnp.concatenate([ckv, kpe], axis=-1).astype(BF16)
    if rest:
        rest[0][0] = ckv.T.astype(BF16)


def _mla_pre(proj, cos, sin, gq, gkv, wn, wpa, wpb, bdk, nbat, t, nb, tt, q_dtype, emit_vt):
    tper = t // tt
    tm = nb * tt
    hw = N_HEADS * LANE
    kern = functools.partial(_mla_pre_kernel, nb=nb, tt=tt)
    tab_spec = pl.BlockSpec((tm, LANE), lambda b, t_: (t_, 0))
    vt_specs, vt_shapes = [], []
    if emit_vt:
        assert nb == 1
        vt_specs = [pl.BlockSpec((1, MLA_KVLORA, tt), lambda b, t_: (b * tper + t_, 0, 0))]
        vt_shapes = [jax.ShapeDtypeStruct((nbat * tper, MLA_KVLORA, tt), BF16)]
    return pl.pallas_call(
        kern,
        grid=(nbat // nb, tper),
        in_specs=[_tok_spec(tm, MLA_QLORA, OFF_QA // MLA_QLORA, tper), _tok_spec(tm, LANE, OFF_CKV // LANE, tper),
                  _tok_spec(tm, LANE, OFF_KPA // LANE, tper), _tok_spec(tm, LANE, OFF_KPB // LANE, tper),
                  tab_spec, tab_spec, _const_spec((1, MLA_QLORA)), _const_spec((1, MLA_KVLORA)),
                  _const_spec((MLA_QLORA, BRANCH_W)), _const_spec((MLA_QLORA, hw)), _const_spec((MLA_QLORA, hw)),
                  _const_spec((BRANCH_W, hw))],
        out_specs=[pl.BlockSpec((nb, N_HEADS, tt, QK_W), lambda b, t_: (b, 0, t_, 0)),
                   _tok_spec(tm, MLA_KVLORA, 0, tper), _tok_spec(tm, MLA_ROPE, 0, tper), _tok_spec(tm, QK_W, 0, tper)]
                  + vt_specs,
        out_shape=[jax.ShapeDtypeStruct((nbat, N_HEADS, t, QK_W), q_dtype),
                   jax.ShapeDtypeStruct((nbat * t, MLA_KVLORA), F32),
                   jax.ShapeDtypeStruct((nbat * t, MLA_ROPE), F32),
                   jax.ShapeDtypeStruct((nbat * t, QK_W), BF16)] + vt_shapes,
        compiler_params=_cparams(2, 48),
        name="mla_pre",
    )(proj, proj, proj, proj, cos, sin, gq, gkv, wn, wpa, wpb, bdk)


def _heads_to_lanes(o, rows_per_head):
    return jnp.concatenate([o[h * rows_per_head:(h + 1) * rows_per_head] for h in range(N_HEADS)], axis=-1)


def _attn_prompt_kernel(q_ref, kf_ref, vt_ref, wvt_ref, y_ref, m_scr, l_scr, acc_scr, *, tq, tk, scale):
    i = pl.program_id(1)
    n_full = (i * tq) // tk
    hpg = ATTN_GROUP_W // tq
    w2 = hpg * tq
    krow = lax.broadcasted_iota(jnp.int32, (tk, w2), 0)
    qcol = i * tq + (lax.broadcasted_iota(jnp.int32, (tk, w2), 1) & (tq - 1))
    heads = range(N_HEADS // hpg)
    m_scr[...] = jnp.full(m_scr.shape, NEG_BIG, F32)
    l_scr[...] = jnp.zeros(l_scr.shape, F32)
    acc_scr[...] = jnp.zeros(acc_scr.shape, F32)

    def step(j, masked):
        kblk = kf_ref[pl.ds(pl.multiple_of(j * tk, tk), tk), :]
        vt = vt_ref[j]
        s = [lax.dot_general(kblk, q_ref[0, hpg * h:hpg * (h + 1)].reshape(w2, QK_W), (((1,), (1,)), ((), ())),
                             preferred_element_type=F32) * scale
             for h in heads]
        if masked:
            keep = j * tk + krow <= qcol
            s = [jnp.where(keep, s[h], NEG_BIG) for h in heads]
        m_old = [m_scr[h] for h in heads]
        m_new = [jnp.maximum(m_old[h], jnp.max(s[h], axis=0, keepdims=True)) for h in heads]
        p = [jnp.exp(s[h] - m_new[h]) for h in heads]
        alpha = [jnp.exp(m_old[h] - m_new[h]) for h in heads]
        pv = [jnp.dot(vt, p[h].astype(BF16), preferred_element_type=F32) for h in heads]
        for h in heads:
            m_scr[h] = m_new[h]
            l_scr[h] = alpha[h] * l_scr[h] + jnp.sum(p[h], axis=0, keepdims=True)
            acc_scr[h] = alpha[h] * acc_scr[h] + pv[h]

    def body(j, carry):
        step(j, False)
        return carry

    lax.fori_loop(0, n_full, body, 0)
    step(n_full, True)
    o = [acc_scr[h] / l_scr[h] for h in heads]
    outs = [_bdot(wvt_ref[hh], o[hh // hpg][:, (hh % hpg) * tq:(hh % hpg + 1) * tq]) for hh in range(N_HEADS)]
    y_ref[...] = jnp.concatenate(outs, axis=0).T


def _attn_prompt(q_full, kf, vt, wvt, nbat, t, tq, tk):
    tper = t // tq
    kper = t // tk
    assert ATTN_GROUP_W % tq == 0
    ngrp = N_HEADS * tq // ATTN_GROUP_W
    scale = (HEAD_DIM + MLA_ROPE) ** -0.5
    kern = functools.partial(_attn_prompt_kernel, tq=tq, tk=tk, scale=scale)
    return pl.pallas_call(
        kern,
        grid=(nbat, tper),
        in_specs=[pl.BlockSpec((1, N_HEADS, tq, QK_W), lambda b, i: (b, 0, i, 0)),
                  pl.BlockSpec((t, QK_W), lambda b, i: (b, 0)),
                  pl.BlockSpec((kper, MLA_KVLORA, tk), lambda b, i: (b, 0, 0)),
                  _const_spec(wvt.shape)],
        out_specs=pl.BlockSpec((tq, BRANCH_W), lambda b, i: (b * tper + i, 0)),
        out_shape=jax.ShapeDtypeStruct((nbat * t, BRANCH_W), F32),
        scratch_shapes=[pltpu.VMEM((ngrp, 1, ATTN_GROUP_W), F32), pltpu.VMEM((ngrp, 1, ATTN_GROUP_W), F32),
                        pltpu.VMEM((ngrp, MLA_KVLORA, ATTN_GROUP_W), F32)],
        compiler_params=_cparams(2, 40),
        name="attn_prompt",
    )(q_full, kf, vt, wvt)


def _attn_sample_kernel(pt_ref, q_ref, *refs, n_pages, pg, ts, page, scale):
    ckv_pages = refs[:n_pages]
    kpe_pages = refs[n_pages:2 * n_pages]
    ckvn_ref, kpen_ref, bdv_ref, y_ref = refs[2 * n_pages:]
    r = N_HEADS * ts
    groups = range(n_pages // pg)

    q = q_ref[0].reshape(r, QK_W)
    q_lat = q[:, :MLA_KVLORA].astype(BF16)
    q_pe = q[:, MLA_KVLORA:MLA_KVLORA + MLA_ROPE].astype(BF16)

    c_all = [jnp.concatenate([ckv_pages[g * pg + i][0, 0].astype(BF16) for i in range(pg)], axis=0)
             for g in groups]
    kpt_all = [jnp.concatenate([kpe_pages[g * pg + i][0, 0].astype(BF16) for i in range(pg)], axis=1)
               for g in groups]
    s = [(lax.dot_general(q_lat, c_all[g], (((1,), (1,)), ((), ())), preferred_element_type=F32)
          + jnp.dot(q_pe, kpt_all[g], preferred_element_type=F32)) * scale for g in groups]
    pad = jnp.zeros((page - ts, MLA_KVLORA), F32)
    cn = jnp.concatenate([ckvn_ref[...], pad], axis=0).astype(BF16)
    kn = jnp.concatenate([kpen_ref[...], pad[:, :MLA_ROPE]], axis=0).astype(BF16)
    sn = (lax.dot_general(q_lat, cn, (((1,), (1,)), ((), ())), preferred_element_type=F32)
          + lax.dot_general(q_pe, kn, (((1,), (1,)), ((), ())), preferred_element_type=F32)) * scale
    qpos = lax.broadcasted_iota(jnp.int32, (r, page), 0) & (ts - 1)
    kpos = lax.broadcasted_iota(jnp.int32, (r, page), 1)
    s.append(jnp.where(kpos <= qpos, sn, NEG_BIG))
    vals = c_all + [cn]
    parts = range(len(s))
    m = [jnp.max(s[g], axis=-1, keepdims=True) for g in parts]
    p = [jnp.exp(s[g] - m[g]) for g in parts]
    l = [jnp.sum(p[g], axis=-1, keepdims=True) for g in parts]
    acc = [jnp.dot(p[g].astype(BF16), vals[g], preferred_element_type=F32) for g in parts]
    m_all = m[0]
    for g in parts[1:]:
        m_all = jnp.maximum(m_all, m[g])
    w = [jnp.exp(m[g] - m_all) for g in parts]
    l_all = w[0] * l[0]
    acc_all = w[0] * acc[0]
    for g in parts[1:]:
        l_all = l_all + w[g] * l[g]
        acc_all = acc_all + w[g] * acc[g]
    o = _heads_to_lanes(acc_all / l_all, ts)
    y_ref[...] = _bdot(o, bdv_ref[...])


def _attn_sample(q_full, cache_ckv, cache_kpe, page_table, layer, ckv_new, kpe_new, bdv, nbat, ts):
    n_pages = page_table.shape[1]
    page = cache_ckv.shape[2]
    pg = math.gcd(n_pages, 16)
    scale = (HEAD_DIM + MLA_ROPE) ** -0.5
    kern = functools.partial(_attn_sample_kernel, n_pages=n_pages, pg=pg, ts=ts, page=page, scale=scale)

    def page_spec(i, rows, width):
        return pl.BlockSpec((1, 1, rows, width), lambda b, pt: (layer, pt[b * n_pages + i], 0, 0))

    in_specs = ([pl.BlockSpec((1, N_HEADS, ts, QK_W), lambda b, pt: (b, 0, 0, 0))]
                + [page_spec(i, page, MLA_KVLORA) for i in range(n_pages)]
                + [page_spec(i, MLA_ROPE, page) for i in range(n_pages)]
                + [pl.BlockSpec((ts, MLA_KVLORA), lambda b, pt: (b, 0)),
                   pl.BlockSpec((ts, MLA_ROPE), lambda b, pt: (b, 0)),
                   pl.BlockSpec((N_HEADS * LANE, BRANCH_W), lambda b, pt: (0, 0))])
    grid_spec = pltpu.PrefetchScalarGridSpec(
        num_scalar_prefetch=1,
        grid=(nbat,),
        in_specs=in_specs,
        out_specs=pl.BlockSpec((ts, BRANCH_W), lambda b, pt: (b, 0)),
    )
    return pl.pallas_call(
        kern,
        grid_spec=grid_spec,
        out_shape=jax.ShapeDtypeStruct((nbat * ts, BRANCH_W), F32),
        compiler_params=_cparams(1, 40),
        name="attn_sample",
    )(page_table.reshape(-1), q_full, *([cache_ckv] * n_pages), *([cache_kpe] * n_pages), ckv_new, kpe_new, bdv)


def _cumsum_rows(x, n):
    rows = lax.broadcasted_iota(jnp.int32, x.shape, 0)
    d = 1
    while d < n:
        x = x + jnp.where(rows >= d, pltpu.roll(x, d, axis=0), 0.0)
        d *= 2
    return x


def _inverse_masks(n):
    rows = lax.broadcasted_iota(jnp.int32, (n, n), 0)
    cols = lax.broadcasted_iota(jnp.int32, (n, n), 1)
    masks = [rows == cols, (rows >> 1) == (cols >> 1)]
    k = 1
    while (1 << k) < n:
        masks.append(((rows >> k) ^ (cols >> k)) == 1)
        k += 1
    return masks


def _gdn_kernel(qkv_ref, ga_ref, gb_ref, z_ref, cst_ref, s0_ref, w_ref, alog_ref, dtb_ref, gn_ref,
                y_ref, sf_ref, csto_ref, s_scr, c_scr, qkv_s, g_ref, beta_ref, *, n_chunks, valid):
    tc = qkv_ref.shape[0]
    c3 = qkv_ref.shape[1]
    c8 = _carry_in(pl.program_id(1), cst_ref, c_scr)
    u3 = qkv_ref[...].astype(F32).reshape(1, tc, c3)
    conv = _silu(_causal_conv(u3, c8, w_ref[...], GDN_K)).reshape(tc, c3)
    g_tile = -jnp.exp(alog_ref[...]) * _softplus(ga_ref[...].astype(F32) + dtb_ref[...])
    b_tile = jax.nn.sigmoid(gb_ref[...].astype(F32))
    if valid < tc:
        conv = jnp.where(lax.broadcasted_iota(jnp.int32, (tc, c3), 0) < valid, conv, 0.0)
        live = lax.broadcasted_iota(jnp.int32, g_tile.shape, 0) < valid
        g_tile = jnp.where(live, g_tile, 0.0)
        b_tile = jnp.where(live, b_tile, 0.0)
    qkv_s[...] = conv
    g_ref[...] = g_tile
    beta_ref[...] = b_tile
    hist = u3[:, valid - SUBLANE:valid, :]
    c_scr[...] = hist
    csto_ref[...] = hist
    q_ref = qkv_s.at[:, 0:BRANCH_W]
    k_ref = qkv_s.at[:, BRANCH_W:2 * BRANCH_W]
    v_ref = qkv_s.at[:, 2 * BRANCH_W:3 * BRANCH_W]

    cc = GDN_CHUNK
    n2 = 2 * cc
    npair = N_HEADS // 2
    zero_blk = jnp.zeros((HEAD_DIM, HEAD_DIM), F32)

    @pl.when(pl.program_id(1) == 0)
    def _():
        for pr in range(npair):
            top = jnp.concatenate([s0_ref[0, 2 * pr], zero_blk], axis=-1)
            bot = jnp.concatenate([zero_blk, s0_ref[0, 2 * pr + 1]], axis=-1)
            s_scr[pr] = jnp.concatenate([top, bot], axis=0)

    rows = lax.broadcasted_iota(jnp.int32, (n2, n2), 0)
    cols = lax.broadcasted_iota(jnp.int32, (n2, n2), 1)
    same = (rows >> 6) == (cols >> 6)
    tril = same & (rows >= cols)
    strict = same & (rows > cols)
    masks = _inverse_masks(n2)[:7]
    eye = jnp.where(masks[0], 1.0, 0.0)
    gn = gn_ref[...]
    probs = [(ci, pr) for ci in range(n_chunks) for pr in range(npair)]

    def rs(ci):
        return slice(ci * cc, (ci + 1) * cc)

    def ls(pr):
        return slice(pr * LANE, (pr + 1) * LANE)

    def stack(x):
        return jnp.where(same, jnp.concatenate([x, x], axis=0), 0.0)

    def twice(x):
        return jnp.concatenate([x, x], axis=0)

    gc_all = [_cumsum_rows(g_ref[rs(ci), :], cc) for ci in range(n_chunks)]
    gcm = [stack(gc_all[ci][:, ls(pr)]) for ci, pr in probs]
    beta = [twice(beta_ref[rs(ci), ls(pr)]) for ci, pr in probs]
    q = []
    k = []
    for ci, pr in probs:
        qr = stack(q_ref[rs(ci), ls(pr)])
        kr = stack(k_ref[rs(ci), ls(pr)])
        q.append(qr * lax.rsqrt(jnp.sum(qr * qr, axis=-1, keepdims=True) + EPS) * (HEAD_DIM ** -0.5))
        k.append(kr * lax.rsqrt(jnp.sum(kr * kr, axis=-1, keepdims=True) + EPS))
    kb = [k[p] * beta[p] for p in range(len(probs))]
    decay = []
    for p in range(len(probs)):
        diff = gcm[p] - gcm[p].T
        decay.append(jnp.where(tril, jnp.exp(jnp.where(tril, diff, 0.0)), 0.0))
    kk = [_bdot_nt(kb[p], k[p]) for p in range(len(probs))]
    a_mat = [jnp.where(strict, kk[p] * decay[p], 0.0) for p in range(len(probs))]
    t = [eye - jnp.where(masks[1], a_mat[p], 0.0) for p in range(len(probs))]
    for m in masks[2:]:
        tl = [_bdot(t[p], jnp.where(m, a_mat[p], 0.0)) for p in range(len(probs))]
        t = [t[p] - _bdot(tl[p], t[p]) for p in range(len(probs))]
    egc = [jnp.exp(gcm[p]) for p in range(len(probs))]
    sol = [_bdot(t[p], jnp.concatenate([stack(v_ref[rs(ci), ls(pr)]) * beta[p], kb[p] * egc[p]], axis=-1))
           for p, (ci, pr) in enumerate(probs)]
    qk = [_bdot_nt(q[p], k[p]) for p in range(len(probs))]
    intra = [jnp.where(tril, qk[p] * decay[p], 0.0) for p in range(len(probs))]
    qe = [q[p] * egc[p] for p in range(len(probs))]
    g_last = [gc_all[ci][cc - 1:cc, ls(pr)] for ci, pr in probs]
    kdec = [k[p] * jnp.exp(g_last[p] - gcm[p]) for p in range(len(probs))]
    sdec = [jnp.exp(g_last[p]) for p in range(len(probs))]

    for ci in range(n_chunks):
        ps = [ci * npair + pr for pr in range(npair)]
        s_old = [s_scr[pr] for pr in range(npair)]
        v_new = [sol[p][:, :n2] - _bdot(sol[p][:, n2:], s_old[pr]) for pr, p in enumerate(ps)]
        o_st = [_bdot(qe[p], s_old[pr]) for pr, p in enumerate(ps)]
        o_in = [_bdot(intra[p], v_new[pr]) for pr, p in enumerate(ps)]
        s_up = [_bdot_tn(kdec[p], v_new[pr]) for pr, p in enumerate(ps)]
        for pr, p in enumerate(ps):
            s_scr[pr] = s_old[pr] * sdec[p] + s_up[pr]
            o = o_st[pr] + o_in[pr]
            o = o * lax.rsqrt(jnp.sum(o * o, axis=-1, keepdims=True) * (1.0 / HEAD_DIM) + EPS)
            o = o[:cc] + o[cc:]
            y_ref[rs(ci), ls(pr)] = o * gn[:, ls(pr)] * _silu(z_ref[rs(ci), ls(pr)].astype(F32))

    @pl.when(pl.program_id(1) == pl.num_programs(1) - 1)
    def _():
        for pr in range(npair):
            s_pair = s_scr[pr]
            sf_ref[0, 2 * pr] = s_pair[:HEAD_DIM, :HEAD_DIM]
            sf_ref[0, 2 * pr + 1] = s_pair[HEAD_DIM:, HEAD_DIM:]


def _gdn(src, col0, cst8, s0, w, alog_b, dtb_b, gn_b, nbat, t, tc, valid):
    tper = t // tc
    c = BRANCH_W
    c3 = 3 * c
    kern = functools.partial(_gdn_kernel, n_chunks=tc // GDN_CHUNK, valid=valid)
    s_spec = pl.BlockSpec((1, N_HEADS, HEAD_DIM, HEAD_DIM), lambda b, t_: (b, 0, 0, 0))
    cst_spec = pl.BlockSpec((1, SUBLANE, c3), lambda b, t_: (b, 0, 0))
    return pl.pallas_call(
        kern,
        grid=(nbat, tper),
        in_specs=[_tok_spec(tc, c3, col0 // c3, tper),
                  _tok_spec(tc, c, (col0 + OFF_GA - OFF_GQKV) // c, tper),
                  _tok_spec(tc, c, (col0 + OFF_GB - OFF_GQKV) // c, tper),
                  _tok_spec(tc, c, (col0 + OFF_GZ - OFF_GQKV) // c, tper),
                  cst_spec, s_spec, _const_spec((GDN_K, c3)), _const_spec((1, c)), _const_spec((1, c)),
                  _const_spec((1, c))],
        out_specs=[_tok_spec(tc, c, 0, tper), s_spec, cst_spec],
        out_shape=[jax.ShapeDtypeStruct((nbat * t, c), F32),
                   jax.ShapeDtypeStruct((nbat, N_HEADS, HEAD_DIM, HEAD_DIM), F32),
                   jax.ShapeDtypeStruct((nbat, SUBLANE, c3), F32)],
        scratch_shapes=[pltpu.VMEM((N_HEADS // 2, 2 * HEAD_DIM, 2 * HEAD_DIM), F32),
                        pltpu.VMEM((1, SUBLANE, c3), F32), pltpu.VMEM((tc, c3), F32),
                        pltpu.VMEM((tc, c), F32), pltpu.VMEM((tc, c), F32)],
        compiler_params=_cparams(2, 40),
        name="gdn_rule",
    )(src, src, src, src, cst8, s0, w, alog_b, dtb_b, gn_b)


def _merge_kernel(x_ref, mod_ref, g_ref, ya_ref, yb_ref, yc_ref, yd_ref, wg_ref, wbo_ref, wmix_ref, o_ref):
    x = x_ref[...]
    nb, tt, d = x.shape
    mod = mod_ref[...]
    h = _modulated_norm(x, g_ref[...], mod, 0, 1).reshape(nb * tt, d).astype(BF16)
    acc = None
    for n, y_ref in enumerate((ya_ref, yb_ref, yc_ref, yd_ref)):
        gate = jax.nn.sigmoid(jnp.dot(h, wg_ref[:, n * d:(n + 1) * d], preferred_element_type=F32))
        term = gate * _bdot(y_ref[...], wbo_ref[n])
        acc = term if acc is None else acc + term
    mix = _bdot(acc, wmix_ref[...])
    o_ref[...] = x + mod[:, 2:3, :] * mix.reshape(nb, tt, d)


def _merge(x3, mod, g, ys, wg, wbo, wmix, nb, tt):
    nbat, t, d = x3.shape
    tper = t // tt
    tm = nb * tt
    c = BRANCH_W
    x_spec = pl.BlockSpec((nb, tt, d), lambda b, t_: (b, t_, 0))
    return pl.pallas_call(
        _merge_kernel,
        grid=(nbat // nb, tper),
        in_specs=[x_spec, pl.BlockSpec((nb, 6, d), lambda b, t_: (b, 0, 0)), _const_spec((1, d))]
                 + [_tok_spec(tm, c, 0, tper)] * 4
                 + [_const_spec(wg.shape), _const_spec(wbo.shape), _const_spec(wmix.shape)],
        out_specs=x_spec,
        out_shape=jax.ShapeDtypeStruct(x3.shape, F32),
        compiler_params=_cparams(2, 48),
        name="merge",
    )(x3, mod, g, *ys, wg, wbo, wmix)


def _ffn_kernel(x_ref, mod_ref, g_ref, win_ref, wout_ref, gf_ref, o_ref, *, final):
    x = x_ref[...]
    nb, tt, d = x.shape
    mod = mod_ref[...]
    fh = wout_ref.shape[0]
    h = _modulated_norm(x, g_ref[...], mod, 3, 4).reshape(nb * tt, d).astype(BF16)
    gate = jnp.dot(h, win_ref[:, :fh], preferred_element_type=F32)
    up = jnp.dot(h, win_ref[:, fh:], preferred_element_type=F32)
    out = _bdot(_silu(gate) * up, wout_ref[...])
    xo = x + mod[:, 5:6, :] * out.reshape(nb, tt, d)
    o_ref[...] = _rms(xo, gf_ref[...]) if final else xo


def _ffn(x3, mod, g, win, wout, gf, nb, tt, final):
    nbat, t, d = x3.shape
    x_spec = pl.BlockSpec((nb, tt, d), lambda b, t_: (b, t_, 0))
    kern = functools.partial(_ffn_kernel, final=final)
    return pl.pallas_call(
        kern,
        grid=(nbat // nb, t // tt),
        in_specs=[x_spec, pl.BlockSpec((nb, 6, d), lambda b, t_: (b, 0, 0)), _const_spec((1, d)),
                  _const_spec(win.shape), _const_spec(wout.shape), _const_spec((1, d))],
        out_specs=x_spec,
        out_shape=jax.ShapeDtypeStruct(x3.shape, F32),
        compiler_params=_cparams(2, 56),
        name="ffn",
    )(x3, mod, g, win, wout, gf)


def _block_diag(blocks):
    h, r, c = blocks.shape
    eye = jnp.eye(h, dtype=blocks.dtype)
    return jnp.einsum('hrc,hg->hrgc', blocks, eye).reshape(h * r, h * c)


def _prep_layer_weights(w_in, w_qb, w_kvb, w_lru_gate_a, w_lru_gate_x):
    d = w_in.shape[0]
    c = BRANCH_W
    o = 0
    segs = {}
    for name, width in (('sc', 3 * c), ('qa', MLA_QLORA), ('ckv', MLA_KVLORA), ('kpe', MLA_ROPE), ('gqkv', 3 * c),
                        ('gz', c), ('ga', N_HEADS), ('gb', N_HEADS), ('lx', c), ('lg', c)):
        segs[name] = w_in[:, o:o + width]
        o += width
    half = MLA_ROPE // 2
    zpad = jnp.zeros((d, LANE - MLA_ROPE), F32)
    kpe = segs['kpe']
    kpa = jnp.concatenate([kpe, zpad], axis=1)
    kpb = jnp.concatenate([-kpe[:, half:], kpe[:, :half], zpad], axis=1)
    w_proj = jnp.concatenate(
        [segs['sc'], segs['gqkv'], segs['gz'], jnp.repeat(segs['ga'], HEAD_DIM, axis=1),
         jnp.repeat(segs['gb'], HEAD_DIM, axis=1), segs['lx'], segs['lg'], segs['qa'], segs['ckv'], kpa, kpb],
        axis=1).astype(BF16)

    ql = w_qb.shape[0]
    wq = w_qb.reshape(ql, N_HEADS, HEAD_DIM + MLA_ROPE)
    wn = wq[:, :, :HEAD_DIM].reshape(ql, c)
    pe = wq[:, :, HEAD_DIM:]
    zq = jnp.zeros((ql, N_HEADS, LANE - MLA_ROPE), F32)
    wpa = jnp.concatenate([pe, zq], axis=2).reshape(ql, N_HEADS * LANE)
    wpb = jnp.concatenate([-pe[:, :, half:], pe[:, :, :half], zq], axis=2).reshape(ql, N_HEADS * LANE)
    w_kb = w_kvb[:, :, :HEAD_DIM]
    w_vb = w_kvb[:, :, HEAD_DIM:]
    bdk = _block_diag(jnp.transpose(w_kb, (1, 2, 0)))
    bdv = _block_diag(jnp.transpose(w_vb, (1, 0, 2)))
    wlg = jnp.concatenate([_block_diag(w_lru_gate_a), _block_diag(w_lru_gate_x)], axis=1)
    wvt = jnp.transpose(w_vb, (1, 2, 0))
    return dict(w_proj=w_proj, wn=wn.astype(BF16), wpa=wpa.astype(BF16), wpb=wpb.astype(BF16),
                bdk=bdk.astype(BF16), bdv=bdv.astype(BF16), wvt=wvt.astype(BF16), wlg=wlg.astype(BF16))


def _rope_tables(pos):
    half = MLA_ROPE // 2
    inv = ROPE_THETA ** (-jnp.arange(half, dtype=F32) / half)
    ang = pos.astype(F32)[:, None] * inv[None, :]
    pad = jnp.zeros((pos.shape[0], LANE - MLA_ROPE), F32)
    cos = jnp.concatenate([jnp.cos(ang), jnp.cos(ang), pad], axis=1)
    sin = jnp.concatenate([jnp.sin(ang), jnp.sin(ang), pad], axis=1)
    return cos, sin


def _pad_state(st, k_w):
    return jnp.pad(st, ((0, 0), (SUBLANE - (k_w - 1), 0), (0, 0)))


def _tile(t, pref):
    tt = min(t, pref)
    while t % tt:
        tt //= 2
    return tt


def _group_layer(x3, mod, lw, p, st, rope, attend, cfg):
    nbat, t, d = x3.shape
    nb, tt = cfg['nb'], cfg['tt']
    proj = _inproj(x3, mod, p['g_norm_mix'], lw['w_proj'], cfg['nb_proj'], cfg['tt_proj'])
    y_a, sc8 = _sconv(proj, st['sconv'], p['w_sc_conv'], nbat, t, nb, tt)
    q_full, ckv, kpe, kf, *vt = _mla_pre(proj, rope[0], rope[1], p['g_q_norm'], p['g_kv_norm'], lw['wn'], lw['wpa'],
                                         lw['wpb'], lw['bdk'], nbat, t, nb, tt, cfg['q_dtype'], cfg['emit_vt'])
    y_b = attend(q_full, ckv, kpe, kf, vt, lw)
    gdn_w = (p['w_gdn_conv'], p['alog_b'], p['dtb_b'], p['gn_b'])
    if cfg['gdn_pad']:
        seg = proj[:, OFF_GQKV:OFF_GB + BRANCH_W].reshape(nbat, t, -1)
        seg = jnp.pad(seg, ((0, 0), (0, GDN_CHUNK - t), (0, 0))).reshape(nbat * GDN_CHUNK, -1)
        y_c, s_gdn, gc8 = _gdn(seg, 0, st['gdn_conv'], st['gdn'], *gdn_w, nbat, GDN_CHUNK, GDN_CHUNK, t)
        y_c = y_c.reshape(nbat, GDN_CHUNK, -1)[:, :t].reshape(nbat * t, -1)
    else:
        y_c, s_gdn, gc8 = _gdn(proj, OFF_GQKV, st['gdn_conv'], st['gdn'], *gdn_w, nbat, t, cfg['tc'], cfg['tc'])
    y_d, lc8, h_last = _lru(proj, st['lru_conv'], st['lru'], p['w_lru_conv'], p['b_lru_conv'], lw['wlg'],
                            p['b_lru_gates'], p['nsl'], nbat, t, cfg['nb_scan'], cfg['tt_scan'], cfg['pos0'])
    x1 = _merge(x3, mod, p['g_norm_mix'], (y_a, y_b, y_c, y_d), p['w_merge_gate'], p['w_branch_out'],
                p['w_mix_out'], cfg['nb_mm'], cfg['tt_mm'])
    x2 = _ffn(x1, mod, p['g_norm_ffn'], p['w_ffn_in'], p['w_ffn_out'], p['g_final'], cfg['nb_mm'], cfg['tt_mm'],
              cfg['final'])
    new_st = dict(ckv=ckv.reshape(nbat, t, -1), kpe=kpe.reshape(nbat, t, -1),
                  sconv=sc8[:, SUBLANE - (SC_K - 1):], gdn_conv=gc8[:, SUBLANE - (GDN_K - 1):], gdn=s_gdn,
                  lru_conv=lc8[:, SUBLANE - (LRU_K - 1):], lru=h_last[:, 0, :])
    return x2, new_st


STATE_KEYS = ('ckv', 'kpe', 'sconv', 'gdn_conv', 'gdn', 'lru_conv', 'lru')


def kernel(x_prompt, x_sample, c_prompt, c_sample, cache_mla_ckv, cache_mla_kpe, page_table, state_sconv, state_gdn_conv, state_gdn, state_lru_conv, state_lru, w_ada, b_ada, g_norm_mix, g_norm_ffn, w_in, w_sc_conv, g_q_norm, w_qb, g_kv_norm, w_kvb, w_gdn_conv, gdn_a_log, gdn_dt_bias, g_gdn_norm, w_lru_conv, b_lru_conv, w_lru_gate_a, b_lru_gate_a, w_lru_gate_x, b_lru_gate_x, lru_lambda, w_branch_out, w_merge_gate, w_mix_out, w_ffn_in, w_ffn_out, g_final):
    bp, tp, d = x_prompt.shape
    bs, ts, _ = x_sample.shape
    depth = w_in.shape[0]
    n_pages = page_table.shape[1]
    past_len = n_pages * cache_mla_ckv.shape[2]
    c = BRANCH_W
    assert ts == SUBLANE and tp % GDN_CHUNK == 0 and d % LANE == 0

    mod_all = _ada(jnp.concatenate([c_prompt, c_sample], axis=0), w_ada, b_ada).reshape(depth, bp + bs, 6, d)

    nb_s = _tile(bs, 32)
    cos_p, sin_p = _rope_tables(jnp.arange(tp, dtype=jnp.int32))
    cos_s, sin_s = _rope_tables(past_len + jnp.arange(ts, dtype=jnp.int32))
    rope_p = (cos_p, sin_p)
    rope_s = (jnp.tile(cos_s, (nb_s, 1)), jnp.tile(sin_s, (nb_s, 1)))

    tq = _tile(tp, 256)
    tk = _tile(tp, 256)
    cfg_p = dict(nb=1, tt=tk, nb_proj=1, tt_proj=_tile(tp, 1024), q_dtype=BF16, emit_vt=True, gdn_pad=False,
                 tc=_tile(tp, 256), pos0=0, nb_scan=1, tt_scan=_tile(tp, 256), nb_mm=1, tt_mm=_tile(tp, 512))
    cfg_s = dict(nb=nb_s, tt=ts, nb_proj=_tile(bs, 128), tt_proj=ts, q_dtype=F32, emit_vt=False, gdn_pad=True,
                 tc=GDN_CHUNK, pos0=past_len, nb_scan=nb_s, tt_scan=ts, nb_mm=_tile(bs, 64), tt_mm=ts)
    cache_kpe_t = jnp.swapaxes(cache_mla_kpe, 2, 3)

    xp, xs = x_prompt, x_sample
    out_p = {k: [] for k in STATE_KEYS}
    out_s = {k: [] for k in STATE_KEYS}
    for l in range(depth):
        lw = _prep_layer_weights(w_in[l], w_qb[l], w_kvb[l], w_lru_gate_a[l], w_lru_gate_x[l])
        p = dict(
            g_norm_mix=g_norm_mix[l][None], g_norm_ffn=g_norm_ffn[l][None], g_final=g_final[None],
            w_sc_conv=w_sc_conv[l], g_q_norm=g_q_norm[l][None], g_kv_norm=g_kv_norm[l][None],
            w_gdn_conv=w_gdn_conv[l],
            alog_b=jnp.repeat(gdn_a_log[l], HEAD_DIM)[None], dtb_b=jnp.repeat(gdn_dt_bias[l], HEAD_DIM)[None],
            gn_b=jnp.tile(g_gdn_norm[l], N_HEADS)[None],
            w_lru_conv=w_lru_conv[l], b_lru_conv=b_lru_conv[l][None],
            b_lru_gates=jnp.concatenate([b_lru_gate_a[l], b_lru_gate_x[l]])[None],
            nsl=lru_lambda[l][None],
            w_merge_gate=w_merge_gate[l].astype(BF16), w_branch_out=w_branch_out[l].astype(BF16),
            w_mix_out=w_mix_out[l].astype(BF16), w_ffn_in=w_ffn_in[l].astype(BF16),
            w_ffn_out=w_ffn_out[l].astype(BF16))
        final = l == depth - 1

        st_p = dict(sconv=jnp.zeros((bp, SUBLANE, c), F32), gdn_conv=jnp.zeros((bp, SUBLANE, 3 * c), F32),
                    gdn=jnp.zeros((bp, N_HEADS, HEAD_DIM, HEAD_DIM), F32),
                    lru_conv=jnp.zeros((bp, SUBLANE, c), F32), lru=jnp.zeros((bp, 1, c), F32))

        def attend_p(q_full, ckv, kpe, kf, vt, lw_):
            return _attn_prompt(q_full, kf, vt[0], lw_['wvt'], bp, tp, tq, tk)

        xp, nst_p = _group_layer(xp, mod_all[l, :bp], lw, p, st_p, rope_p, attend_p, dict(cfg_p, final=final))

        st_s = dict(sconv=_pad_state(state_sconv[l], SC_K), gdn_conv=_pad_state(state_gdn_conv[l], GDN_K),
                    gdn=state_gdn[l], lru_conv=_pad_state(state_lru_conv[l], LRU_K), lru=state_lru[l][:, None, :])

        def attend_s(q_full, ckv, kpe, kf, vt, lw_, layer=l):
            return _attn_sample(q_full, cache_mla_ckv, cache_kpe_t, page_table, layer, ckv, kpe, lw_['bdv'], bs, ts)

        xs, nst_s = _group_layer(xs, mod_all[l, bp:], lw, p, st_s, rope_s, attend_s, dict(cfg_s, final=final))
        for k in STATE_KEYS:
            out_p[k].append(nst_p[k])
            out_s[k].append(nst_s[k])

    return ((xp, xs) + tuple(jnp.stack(out_p[k]) for k in STATE_KEYS)
            + tuple(jnp.stack(out_s[k]) for k in STATE_KEYS))
```

```python
import functools
import math

import jax
import jax.numpy as jnp
from jax import lax
from jax.experimental import pallas as pl
from jax.experimental.pallas import tpu as pltpu

F32 = jnp.float32
BF16 = jnp.bfloat16

HEAD_DIM = 64
N_HEADS = 8
BRANCH_W = N_HEADS * HEAD_DIM
MLA_ROPE = HEAD_DIM // 2
MLA_QLORA = 256
MLA_KVLORA = 128
ROPE_THETA = 10000.0
SC_K = 3
GDN_K = 4
LRU_K = 4
LRU_C = 8.0
GDN_CHUNK = 64
EPS = 1e-6
NEG_BIG = -1e30
LANE = 128
SUBLANE = 8
QK_W = 2 * LANE
ATTN_GROUP_W = 256

OFF_SC = 0
OFF_GQKV = 1536
OFF_GZ = 3072
OFF_GA = 3584
OFF_GB = 4096
OFF_LX = 4608
OFF_LG = 5120
OFF_QA = 5632
OFF_CKV = 5888
OFF_KPA = 6016
OFF_KPB = 6144
PROJ_W = 6272
PROJ_TN = 896


def _cparams(n_axes, vmem_mib):
    return pltpu.CompilerParams(dimension_semantics=("arbitrary",) * n_axes,
                                vmem_limit_bytes=vmem_mib * 1024 * 1024)


def _const_spec(shape):
    nd = len(shape)
    return pl.BlockSpec(shape, lambda *_: (0,) * nd, pipeline_mode=pl.Buffered(1))


def _bdot(a, b):
    return jnp.dot(a.astype(BF16), b.astype(BF16), preferred_element_type=F32)


def _bdot_nt(a, b):
    return lax.dot_general(a.astype(BF16), b.astype(BF16), (((1,), (1,)), ((), ())),
                           preferred_element_type=F32)


def _bdot_tn(a, b):
    return lax.dot_general(a.astype(BF16), b.astype(BF16), (((0,), (0,)), ((), ())),
                           preferred_element_type=F32)


def _silu(x):
    return x * jax.nn.sigmoid(x)


def _softplus(x):
    return jnp.maximum(x, 0.0) + jnp.log(1.0 + jnp.exp(-jnp.abs(x)))


def _gelu_tanh(x):
    return 0.5 * x * (1.0 + jnp.tanh(math.sqrt(2.0 / math.pi) * (x + 0.044715 * (x * x * x))))


def _rms(x, g):
    return x * lax.rsqrt(jnp.mean(x * x, axis=-1, keepdims=True) + EPS) * g


def _modulated_norm(x3, g, mod, shift_row, scale_row):
    y = _rms(x3, g)
    return y * (1.0 + mod[:, scale_row:scale_row + 1, :]) + mod[:, shift_row:shift_row + 1, :]


def _ada_kernel(c_ref, w_ref, b_ref, o_ref):
    o_ref[0] = _bdot(_silu(c_ref[...]), w_ref[0]) + b_ref[0]


def _ada(c_all, w_ada, b_ada):
    depth, d, n = w_ada.shape
    nb = c_all.shape[0]
    tn = n // 4
    return pl.pallas_call(
        _ada_kernel,
        grid=(depth, n // tn),
        in_specs=[pl.BlockSpec((nb, d), lambda l, j: (0, 0)),
                  pl.BlockSpec((1, d, tn), lambda l, j: (l, 0, j)),
                  pl.BlockSpec((1, 1, tn), lambda l, j: (l, 0, j))],
        out_specs=pl.BlockSpec((1, nb, tn), lambda l, j: (l, 0, j)),
        out_shape=jax.ShapeDtypeStruct((depth, nb, n), F32),
        compiler_params=_cparams(2, 40),
        name="ada_mod",
    )(c_all, w_ada, b_ada.reshape(depth, 1, n))


def _inproj_kernel(x_ref, mod_ref, g_ref, w_ref, o_ref, h_scr):
    @pl.when(pl.program_id(1) == 0)
    def _():
        h = _modulated_norm(x_ref[...], g_ref[...], mod_ref[...], 0, 1)
        h_scr[...] = h.reshape(h_scr.shape).astype(BF16)

    o_ref[...] = jnp.dot(h_scr[...], w_ref[...], preferred_element_type=F32).astype(o_ref.dtype)


def _inproj(x3, mod, g, w, nb, tt):
    nbat, t, d = x3.shape
    tper = t // tt
    tm = nb * tt
    n_m = (nbat // nb) * tper
    return pl.pallas_call(
        _inproj_kernel,
        grid=(n_m, PROJ_W // PROJ_TN),
        in_specs=[pl.BlockSpec((nb, tt, d), lambda i, j: (i // tper, i % tper, 0)),
                  pl.BlockSpec((nb, 6, d), lambda i, j: (i // tper, 0, 0)),
                  _const_spec((1, d)),
                  pl.BlockSpec((d, PROJ_TN), lambda i, j: (0, j))],
        out_specs=pl.BlockSpec((tm, PROJ_TN), lambda i, j: (i, j)),
        out_shape=jax.ShapeDtypeStruct((nbat * t, PROJ_W), BF16),
        scratch_shapes=[pltpu.VMEM((tm, d), BF16)],
        compiler_params=_cparams(2, 40),
        name="in_proj",
    )(x3, mod, g, w)


def _causal_conv(u3, c8, w, k_w):
    nb, tt, c = u3.shape
    rows = lax.broadcasted_iota(jnp.int32, (nb, SUBLANE, c), 1)
    acc = None
    for j in range(k_w):
        s = k_w - 1 - j
        if s == 0:
            sh = u3
        else:
            full = pltpu.roll(u3, s, axis=1)
            top = jnp.where(rows < s, pltpu.roll(c8, s, axis=1), full[:, 0:SUBLANE, :])
            sh = top if tt == SUBLANE else jnp.concatenate([top, full[:, SUBLANE:, :]], axis=1)
        term = sh * w[j:j + 1, :]
        acc = term if acc is None else acc + term
    return acc


def _carry_in(t_idx, st_ref, c_scr):
    @pl.when(t_idx == 0)
    def _():
        c_scr[...] = st_ref[...]

    return c_scr[...]


def _sconv_kernel(bg_ref, cg_ref, xt_ref, st_ref, w_ref, y_ref, sto_ref, c_scr, *, nb, tt):
    c = bg_ref.shape[-1]
    c8 = _carry_in(pl.program_id(1), st_ref, c_scr)
    u3 = (cg_ref[...].astype(F32) * xt_ref[...].astype(F32)).reshape(nb, tt, c)
    conv = _causal_conv(u3, c8, w_ref[...], SC_K)
    y_ref[...] = bg_ref[...].astype(F32) * conv.reshape(nb * tt, c)
    last = u3[:, tt - SUBLANE:, :]
    c_scr[...] = last
    sto_ref[...] = last


def _tok_spec(tm, width, col_block, tper):
    return pl.BlockSpec((tm, width), lambda b, t: (b * tper + t, col_block))


def _state_spec(nb, c):
    return pl.BlockSpec((nb, SUBLANE, c), lambda b, t: (b, 0, 0))


def _sconv(proj, st8, w, nbat, t, nb, tt):
    tper = t // tt
    tm = nb * tt
    c = BRANCH_W
    kern = functools.partial(_sconv_kernel, nb=nb, tt=tt)
    return pl.pallas_call(
        kern,
        grid=(nbat // nb, tper),
        in_specs=[_tok_spec(tm, c, OFF_SC // c, tper), _tok_spec(tm, c, OFF_SC // c + 1, tper),
                  _tok_spec(tm, c, OFF_SC // c + 2, tper), _state_spec(nb, c), _const_spec((SC_K, c))],
        out_specs=[_tok_spec(tm, c, 0, tper), _state_spec(nb, c)],
        out_shape=[jax.ShapeDtypeStruct((nbat * t, c), F32), jax.ShapeDtypeStruct((nbat, SUBLANE, c), F32)],
        scratch_shapes=[pltpu.VMEM((nb, SUBLANE, c), F32)],
        compiler_params=_cparams(2, 40),
        name="short_conv",
    )(proj, proj, proj, st8, w)


def _lru_kernel(lx_ref, lg_ref, st_ref, h0_ref, w_ref, cb_ref, wg_ref, bg_ref, lam_ref,
                y_ref, sto_ref, hl_ref, c_scr, h_scr, *, nb, tt, pos0):
    c = lx_ref.shape[-1]
    t_idx = pl.program_id(1)
    c8 = _carry_in(t_idx, st_ref, c_scr)

    @pl.when(t_idx == 0)
    def _():
        h_scr[...] = h0_ref[...]

    u3 = lx_ref[...].astype(F32).reshape(nb, tt, c)
    u = (_causal_conv(u3, c8, w_ref[...], LRU_K) + cb_ref[...]).reshape(nb * tt, c)
    last = u3[:, tt - SUBLANE:, :]
    c_scr[...] = last
    sto_ref[...] = last
    gates = jax.nn.sigmoid(_bdot(u, wg_ref[...]) + bg_ref[...])
    r = gates[:, :c]
    ig = gates[:, c:]
    log_a = -LRU_C * r * _softplus(-lam_ref[...])
    rows = lax.broadcasted_iota(jnp.int32, (nb, tt, c), 1)
    mult = jnp.sqrt(1.0 - jnp.exp(2.0 * log_a)).reshape(nb, tt, c)
    mult = jnp.where(pos0 + t_idx * tt + rows == 0, 1.0, mult)
    a = jnp.exp(log_a).reshape(nb, tt, c)
    b = mult * (ig * u).reshape(nb, tt, c)
    d = 1
    while d < tt:
        keep = rows >= d
        a_s = jnp.where(keep, pltpu.roll(a, d, axis=1), 1.0)
        b_s = jnp.where(keep, pltpu.roll(b, d, axis=1), 0.0)
        b = a * b_s + b
        a = a * a_s
        d *= 2
    hs = b + a * h_scr[...]
    h_last = hs[:, tt - 1:tt, :]
    h_scr[...] = h_last
    hl_ref[...] = h_last
    y_ref[...] = hs.reshape(nb * tt, c) * _gelu_tanh(lg_ref[...].astype(F32))


def _lru(proj, st8, h0, w, cb, wg, bg, lam, nbat, t, nb, tt, pos0):
    tper = t // tt
    tm = nb * tt
    c = BRANCH_W
    kern = functools.partial(_lru_kernel, nb=nb, tt=tt, pos0=pos0)
    h_spec = pl.BlockSpec((nb, 1, c), lambda b, t_: (b, 0, 0))
    return pl.pallas_call(
        kern,
        grid=(nbat // nb, tper),
        in_specs=[_tok_spec(tm, c, OFF_LX // c, tper), _tok_spec(tm, c, OFF_LG // c, tper), _state_spec(nb, c),
                  h_spec, _const_spec((LRU_K, c)), _const_spec((1, c)), _const_spec((c, 2 * c)),
                  _const_spec((1, 2 * c)), _const_spec((1, c))],
        out_specs=[_tok_spec(tm, c, 0, tper), _state_spec(nb, c), h_spec],
        out_shape=[jax.ShapeDtypeStruct((nbat * t, c), F32), jax.ShapeDtypeStruct((nbat, SUBLANE, c), F32),
                   jax.ShapeDtypeStruct((nbat, 1, c), F32)],
        scratch_shapes=[pltpu.VMEM((nb, SUBLANE, c), F32), pltpu.VMEM((nb, 1, c), F32)],
        compiler_params=_cparams(2, 40),
        name="lru",
    )(proj, proj, st8, h0, w, cb, wg, bg, lam)


def _mla_pre_kernel(qa_ref, ckv_ref, kpa_ref, kpb_ref, cos_ref, sin_ref, gq_ref, gkv_ref,
                    wn_ref, wpa_ref, wpb_ref, bdk_ref, q_out, ckv_out, kpe_out, kf_out, *rest, nb, tt):
    cos = cos_ref[...]
    sin = sin_ref[...]
    cq = _rms(qa_ref[...].astype(F32), gq_ref[...]).astype(BF16)
    qn = jnp.dot(cq, wn_ref[...], preferred_element_type=F32)
    qabs = _bdot(qn, bdk_ref[...])
    pa = jnp.dot(cq, wpa_ref[...], preferred_element_type=F32)
    pb = jnp.dot(cq, wpb_ref[...], preferred_element_type=F32)
    for h in range(N_HEADS):
        sl = slice(h * LANE, (h + 1) * LANE)
        q_pe = pa[:, sl] * cos + pb[:, sl] * sin
        q_out[:, h, :, 0:LANE] = qabs[:, sl].reshape(nb, tt, LANE).astype(q_out.dtype)
        q_out[:, h, :, LANE:QK_W] = q_pe.reshape(nb, tt, LANE).astype(q_out.dtype)
    ckv = _rms(ckv_ref[...].astype(F32), gkv_ref[...])
    kpe = kpa_ref[...].astype(F32) * cos + kpb_ref[...].astype(F32) * sin
    ckv_out[...] = ckv
    kpe_out[...] = kpe[:, :MLA_ROPE]
    kf_out[...] = jnp.concatenate([ckv, kpe], axis=-1).astype(BF16)
    if rest:
        rest[0][0] = ckv.T.astype(BF16)


def _mla_pre(proj, cos, sin, gq, gkv, wn, wpa, wpb, bdk, nbat, t, nb, tt, q_dtype, emit_vt):
    tper = t // tt
    tm = nb * tt
    hw = N_HEADS * LANE
    kern = functools.partial(_mla_pre_kernel, nb=nb, tt=tt)
    tab_spec = pl.BlockSpec((tm, LANE), lambda b, t_: (t_, 0))
    vt_specs, vt_shapes = [], []
    if emit_vt:
        assert nb == 1
        vt_specs = [pl.BlockSpec((1, MLA_KVLORA, tt), lambda b, t_: (b * tper + t_, 0, 0))]
        vt_shapes = [jax.ShapeDtypeStruct((nbat * tper, MLA_KVLORA, tt), BF16)]
    return pl.pallas_call(
        kern,
        grid=(nbat // nb, tper),
        in_specs=[_tok_spec(tm, MLA_QLORA, OFF_QA // MLA_QLORA, tper), _tok_spec(tm, LANE, OFF_CKV // LANE, tper),
                  _tok_spec(tm, LANE, OFF_KPA // LANE, tper), _tok_spec(tm, LANE, OFF_KPB // LANE, tper),
                  tab_spec, tab_spec, _const_spec((1, MLA_QLORA)), _const_spec((1, MLA_KVLORA)),
                  _const_spec((MLA_QLORA, BRANCH_W)), _const_spec((MLA_QLORA, hw)), _const_spec((MLA_QLORA, hw)),
                  _const_spec((BRANCH_W, hw))],
        out_specs=[pl.BlockSpec((nb, N_HEADS, tt, QK_W), lambda b, t_: (b, 0, t_, 0)),
                   _tok_spec(tm, MLA_KVLORA, 0, tper), _tok_spec(tm, MLA_ROPE, 0, tper), _tok_spec(tm, QK_W, 0, tper)]
                  + vt_specs,
        out_shape=[jax.ShapeDtypeStruct((nbat, N_HEADS, t, QK_W), q_dtype),
                   jax.ShapeDtypeStruct((nbat * t, MLA_KVLORA), F32),
                   jax.ShapeDtypeStruct((nbat * t, MLA_ROPE), F32),
                   jax.ShapeDtypeStruct((nbat * t, QK_W), BF16)] + vt_shapes,
        compiler_params=_cparams(2, 48),
        name="mla_pre",
    )(proj, proj, proj, proj, cos, sin, gq, gkv, wn, wpa, wpb, bdk)


def _heads_to_lanes(o, rows_per_head):
    return jnp.concatenate([o[h * rows_per_head:(h + 1) * rows_per_head] for h in range(N_HEADS)], axis=-1)


def _attn_prompt_kernel(q_ref, kf_ref, vt_ref, wvt_ref, y_ref, m_scr, l_scr, acc_scr, *, tq, tk, scale):
    i = pl.program_id(1)
    n_full = (i * tq) // tk
    hpg = ATTN_GROUP_W // tq
    w2 = hpg * tq
    krow = lax.broadcasted_iota(jnp.int32, (tk, w2), 0)
    qcol = i * tq + (lax.broadcasted_iota(jnp.int32, (tk, w2), 1) & (tq - 1))
    heads = range(N_HEADS // hpg)
    m_scr[...] = jnp.full(m_scr.shape, NEG_BIG, F32)
    l_scr[...] = jnp.zeros(l_scr.shape, F32)
    acc_scr[...] = jnp.zeros(acc_scr.shape, F32)

    def step(j, masked):
        kblk = kf_ref[pl.ds(pl.multiple_of(j * tk, tk), tk), :]
        vt = vt_ref[j]
        s = [lax.dot_general(kblk, q_ref[0, hpg * h:hpg * (h + 1)].reshape(w2, QK_W), (((1,), (1,)), ((), ())),
                             preferred_element_type=F32) * scale
             for h in heads]
        if masked:
            keep = j * tk + krow <= qcol
            s = [jnp.where(keep, s[h], NEG_BIG) for h in heads]
        m_old = [m_scr[h] for h in heads]
        m_new = [jnp.maximum(m_old[h], jnp.max(s[h], axis=0, keepdims=True)) for h in heads]
        p = [jnp.exp(s[h] - m_new[h]) for h in heads]
        alpha = [jnp.exp(m_old[h] - m_new[h]) for h in heads]
        pv = [jnp.dot(vt, p[h].astype(BF16), preferred_element_type=F32) for h in heads]
        for h in heads:
            m_scr[h] = m_new[h]
            l_scr[h] = alpha[h] * l_scr[h] + jnp.sum(p[h], axis=0, keepdims=True)
            acc_scr[h] = alpha[h] * acc_scr[h] + pv[h]

    def body(j, carry):
        step(j, False)
        return carry

    lax.fori_loop(0, n_full, body, 0)
    step(n_full, True)
    o = [acc_scr[h] / l_scr[h] for h in heads]
    outs = [_bdot(wvt_ref[hh], o[hh // hpg][:, (hh % hpg) * tq:(hh % hpg + 1) * tq]) for hh in range(N_HEADS)]
    y_ref[...] = jnp.concatenate(outs, axis=0).T


def _attn_prompt(q_full, kf, vt, wvt, nbat, t, tq, tk):
    tper = t // tq
    kper = t // tk
    assert ATTN_GROUP_W % tq == 0
    ngrp = N_HEADS * tq // ATTN_GROUP_W
    scale = (HEAD_DIM + MLA_ROPE) ** -0.5
    kern = functools.partial(_attn_prompt_kernel, tq=tq, tk=tk, scale=scale)
    return pl.pallas_call(
        kern,
        grid=(nbat, tper),
        in_specs=[pl.BlockSpec((1, N_HEADS, tq, QK_W), lambda b, i: (b, 0, i, 0)),
                  pl.BlockSpec((t, QK_W), lambda b, i: (b, 0)),
                  pl.BlockSpec((kper, MLA_KVLORA, tk), lambda b, i: (b, 0, 0)),
                  _const_spec(wvt.shape)],
        out_specs=pl.BlockSpec((tq, BRANCH_W), lambda b, i: (b * tper + i, 0)),
        out_shape=jax.ShapeDtypeStruct((nbat * t, BRANCH_W), F32),
        scratch_shapes=[pltpu.VMEM((ngrp, 1, ATTN_GROUP_W), F32), pltpu.VMEM((ngrp, 1, ATTN_GROUP_W), F32),
                        pltpu.VMEM((ngrp, MLA_KVLORA, ATTN_GROUP_W), F32)],
        compiler_params=_cparams(2, 40),
        name="attn_prompt",
    )(q_full, kf, vt, wvt)


def _attn_sample_kernel(pt_ref, q_ref, *refs, n_pages, pg, ts, page, scale):
    ckv_pages = refs[:n_pages]
    kpe_pages = refs[n_pages:2 * n_pages]
    ckvn_ref, kpen_ref, bdv_ref, y_ref = refs[2 * n_pages:]
    r = N_HEADS * ts
    groups = range(n_pages // pg)

    q = q_ref[0].reshape(r, QK_W)
    q_lat = q[:, :MLA_KVLORA].astype(BF16)
    q_pe = q[:, MLA_KVLORA:MLA_KVLORA + MLA_ROPE].astype(BF16)

    c_all = [jnp.concatenate([ckv_pages[g * pg + i][0, 0].astype(BF16) for i in range(pg)], axis=0)
             for g in groups]
    kpt_all = [jnp.concatenate([kpe_pages[g * pg + i][0, 0].astype(BF16) for i in range(pg)], axis=1)
               for g in groups]
    s = [(lax.dot_general(q_lat, c_all[g], (((1,), (1,)), ((), ())), preferred_element_type=F32)
          + jnp.dot(q_pe, kpt_all[g], preferred_element_type=F32)) * scale for g in groups]
    pad = jnp.zeros((page - ts, MLA_KVLORA), F32)
    cn = jnp.concatenate([ckvn_ref[...], pad], axis=0).astype(BF16)
    kn = jnp.concatenate([kpen_ref[...], pad[:, :MLA_ROPE]], axis=0).astype(BF16)
    sn = (lax.dot_general(q_lat, cn, (((1,), (1,)), ((), ())), preferred_element_type=F32)
          + lax.dot_general(q_pe, kn, (((1,), (1,)), ((), ())), preferred_element_type=F32)) * scale
    qpos = lax.broadcasted_iota(jnp.int32, (r, page), 0) & (ts - 1)
    kpos = lax.broadcasted_iota(jnp.int32, (r, page), 1)
    s.append(jnp.where(kpos <= qpos, sn, NEG_BIG))
    vals = c_all + [cn]
    parts = range(len(s))
    m = [jnp.max(s[g], axis=-1, keepdims=True) for g in parts]
    p = [jnp.exp(s[g] - m[g]) for g in parts]
    l = [jnp.sum(p[g], axis=-1, keepdims=True) for g in parts]
    acc = [jnp.dot(p[g].astype(BF16), vals[g], preferred_element_type=F32) for g in parts]
    m_all = m[0]
    for g in parts[1:]:
        m_all = jnp.maximum(m_all, m[g])
    w = [jnp.exp(m[g] - m_all) for g in parts]
    l_all = w[0] * l[0]
    acc_all = w[0] * acc[0]
    for g in parts[1:]:
        l_all = l_all + w[g] * l[g]
        acc_all = acc_all + w[g] * acc[g]
    o = _heads_to_lanes(acc_all / l_all, ts)
    y_ref[...] = _bdot(o, bdv_ref[...])


def _attn_sample(q_full, cache_ckv, cache_kpe, page_table, layer, ckv_new, kpe_new, bdv, nbat, ts):
    n_pages = page_table.shape[1]
    page = cache_ckv.shape[2]
    pg = math.gcd(n_pages, 16)
    scale = (HEAD_DIM + MLA_ROPE) ** -0.5
    kern = functools.partial(_attn_sample_kernel, n_pages=n_pages, pg=pg, ts=ts, page=page, scale=scale)

    def page_spec(i, rows, width):
        return pl.BlockSpec((1, 1, rows, width), lambda b, pt: (layer, pt[b * n_pages + i], 0, 0))

    in_specs = ([pl.BlockSpec((1, N_HEADS, ts, QK_W), lambda b, pt: (b, 0, 0, 0))]
                + [page_spec(i, page, MLA_KVLORA) for i in range(n_pages)]
                + [page_spec(i, MLA_ROPE, page) for i in range(n_pages)]
                + [pl.BlockSpec((ts, MLA_KVLORA), lambda b, pt: (b, 0)),
                   pl.BlockSpec((ts, MLA_ROPE), lambda b, pt: (b, 0)),
                   pl.BlockSpec((N_HEADS * LANE, BRANCH_W), lambda b, pt: (0, 0))])
    grid_spec = pltpu.PrefetchScalarGridSpec(
        num_scalar_prefetch=1,
        grid=(nbat,),
        in_specs=in_specs,
        out_specs=pl.BlockSpec((ts, BRANCH_W), lambda b, pt: (b, 0)),
    )
    return pl.pallas_call(
        kern,
        grid_spec=grid_spec,
        out_shape=jax.ShapeDtypeStruct((nbat * ts, BRANCH_W), F32),
        compiler_params=_cparams(1, 40),
        name="attn_sample",
    )(page_table.reshape(-1), q_full, *([cache_ckv] * n_pages), *([cache_kpe] * n_pages), ckv_new, kpe_new, bdv)


def _cumsum_rows(x, n):
    rows = lax.broadcasted_iota(jnp.int32, x.shape, 0)
    d = 1
    while d < n:
        x = x + jnp.where(rows >= d, pltpu.roll(x, d, axis=0), 0.0)
        d *= 2
    return x


def _inverse_masks(n):
    rows = lax.broadcasted_iota(jnp.int32, (n, n), 0)
    cols = lax.broadcasted_iota(jnp.int32, (n, n), 1)
    masks = [rows == cols, (rows >> 1) == (cols >> 1)]
    k = 1
    while (1 << k) < n:
        masks.append(((rows >> k) ^ (cols >> k)) == 1)
        k += 1
    return masks


def _gdn_solve(q_raw, k_raw, v_raw, gcm, beta, tril, strict, masks):
    n = range(len(q_raw))
    eye = jnp.where(masks[0], 1.0, 0.0)
    q = [x * lax.rsqrt(jnp.sum(x * x, axis=-1, keepdims=True) + EPS) * (HEAD_DIM ** -0.5) for x in q_raw]
    k = [x * lax.rsqrt(jnp.sum(x * x, axis=-1, keepdims=True) + EPS) for x in k_raw]
    kb = [k[p] * beta[p] for p in n]
    decay = []
    for p in n:
        diff = gcm[p] - gcm[p].T
        decay.append(jnp.where(tril, jnp.exp(jnp.where(tril, diff, 0.0)), 0.0))
    kk = [_bdot_nt(kb[p], k[p]) for p in n]
    a_mat = [jnp.where(strict, kk[p] * decay[p], 0.0) for p in n]
    t = [eye - jnp.where(masks[1], a_mat[p], 0.0) for p in n]
    for m in masks[2:]:
        tl = [_bdot(t[p], jnp.where(m, a_mat[p], 0.0)) for p in n]
        t = [t[p] - _bdot(tl[p], t[p]) for p in n]
    egc = [jnp.exp(gcm[p]) for p in n]
    sol = [_bdot(t[p], jnp.concatenate([v_raw[p] * beta[p], kb[p] * egc[p]], axis=-1)) for p in n]
    qk = [_bdot_nt(q[p], k[p]) for p in n]
    intra = [jnp.where(tril, qk[p] * decay[p], 0.0) for p in n]
    qe = [q[p] * egc[p] for p in n]
    return sol, intra, qe, k


def _gdn_short_kernel(qkv_ref, ga_ref, gb_ref, z_ref, cst_ref, s0_ref, w_ref, alog_ref, dtb_ref, gn_ref,
                      y_ref, sf_ref, csto_ref, *, nseq, ts):
    cc = nseq * ts
    n2 = 2 * cc
    npair = N_HEADS // 2
    c3 = qkv_ref.shape[1]
    c = BRANCH_W
    u3 = qkv_ref[...].astype(F32).reshape(nseq, ts, c3)
    conv = _silu(_causal_conv(u3, cst_ref[...], w_ref[...], GDN_K)).reshape(cc, c3)
    csto_ref[...] = u3
    g3 = (-jnp.exp(alog_ref[...]) * _softplus(ga_ref[...].astype(F32) + dtb_ref[...])).reshape(nseq, ts, c)
    b_tile = jax.nn.sigmoid(gb_ref[...].astype(F32))
    tok = lax.broadcasted_iota(jnp.int32, (nseq, ts, c), 1)
    d = 1
    while d < ts:
        g3 = g3 + jnp.where(tok >= d, pltpu.roll(g3, d, axis=1), 0.0)
        d *= 2
    gc = g3.reshape(cc, c)
    g_tot = g3[:, ts - 1:ts, :]
    gl = jnp.broadcast_to(g_tot, (nseq, ts, c)).reshape(cc, c)

    rows = lax.broadcasted_iota(jnp.int32, (n2, n2), 0)
    cols = lax.broadcasted_iota(jnp.int32, (n2, n2), 1)
    same_head = (rows >> 6) == (cols >> 6)
    same_seq = (rows >> 3) == (cols >> 3)
    tril = same_seq & (rows >= cols)
    strict = same_seq & (rows > cols)
    masks = _inverse_masks(n2)[:4]
    gn = gn_ref[...]
    pairs = range(npair)

    def ls(pr):
        return slice(pr * LANE, (pr + 1) * LANE)

    def stack(x):
        return jnp.where(same_head, jnp.concatenate([x, x], axis=0), 0.0)

    def twice(x):
        return jnp.concatenate([x, x], axis=0)

    erow = (lax.broadcasted_iota(jnp.int32, (n2, nseq * n2), 0) >> 3) & (nseq - 1)
    eblk = lax.broadcasted_iota(jnp.int32, (n2, nseq * n2), 1) >> 7
    emask = erow == eblk

    def expand(x):
        return jnp.where(emask, jnp.concatenate([x] * nseq, axis=1), 0.0)

    gcm = [stack(gc[:, ls(pr)]) for pr in pairs]
    glm = [stack(gl[:, ls(pr)]) for pr in pairs]
    beta = [twice(b_tile[:, ls(pr)]) for pr in pairs]
    sol, intra, qe, k = _gdn_solve([stack(conv[:, ls(pr)]) for pr in pairs],
                                   [stack(conv[:, c + pr * LANE:c + (pr + 1) * LANE]) for pr in pairs],
                                   [stack(conv[:, 2 * c + pr * LANE:2 * c + (pr + 1) * LANE]) for pr in pairs],
                                   gcm, beta, tril, strict, masks)
    kdec = [k[pr] * jnp.exp(glm[pr] - gcm[pr]) for pr in pairs]
    zero_blk = jnp.zeros((HEAD_DIM, HEAD_DIM), F32)
    s_old = []
    sdec = []
    for pr in pairs:
        blocks = []
        decs = []
        for b in range(nseq):
            top = jnp.concatenate([s0_ref[b, 2 * pr], zero_blk], axis=-1)
            bot = jnp.concatenate([zero_blk, s0_ref[b, 2 * pr + 1]], axis=-1)
            blocks += [top, bot]
            decs.append(jnp.broadcast_to(jnp.exp(g_tot[b, :, ls(pr)]), (n2, LANE)))
        s_old.append(jnp.concatenate(blocks, axis=0))
        sdec.append(jnp.concatenate(decs, axis=0))
    v_new = [sol[pr][:, :n2] - _bdot(expand(sol[pr][:, n2:]), s_old[pr]) for pr in pairs]
    o_st = [_bdot(expand(qe[pr]), s_old[pr]) for pr in pairs]
    o_in = [_bdot(intra[pr], v_new[pr]) for pr in pairs]
    s_up = [_bdot_tn(expand(kdec[pr]), v_new[pr]) for pr in pairs]
    for pr in pairs:
        s_new = s_old[pr] * sdec[pr] + s_up[pr]
        for b in range(nseq):
            sf_ref[b, 2 * pr] = s_new[b * n2:b * n2 + HEAD_DIM, :HEAD_DIM]
            sf_ref[b, 2 * pr + 1] = s_new[b * n2 + HEAD_DIM:(b + 1) * n2, HEAD_DIM:]
        o = o_st[pr] + o_in[pr]
        o = o * lax.rsqrt(jnp.sum(o * o, axis=-1, keepdims=True) * (1.0 / HEAD_DIM) + EPS)
        o = o[:cc] + o[cc:]
        y_ref[:, ls(pr)] = o * gn[:, ls(pr)] * _silu(z_ref[:, ls(pr)].astype(F32))


def _gdn_short(proj, cst8, s0, w, alog_b, dtb_b, gn_b, nbat, ts):
    nseq = GDN_CHUNK // ts
    assert ts == SUBLANE and nbat % nseq == 0
    c = BRANCH_W
    c3 = 3 * c
    tm = nseq * ts
    kern = functools.partial(_gdn_short_kernel, nseq=nseq, ts=ts)
    s_spec = pl.BlockSpec((nseq, N_HEADS, HEAD_DIM, HEAD_DIM), lambda i: (i, 0, 0, 0))
    cst_spec = pl.BlockSpec((nseq, SUBLANE, c3), lambda i: (i, 0, 0))

    def tok(width, off):
        return pl.BlockSpec((tm, width), lambda i: (i, off // width))

    return pl.pallas_call(
        kern,
        grid=(nbat // nseq,),
        in_specs=[tok(c3, OFF_GQKV), tok(c, OFF_GA), tok(c, OFF_GB), tok(c, OFF_GZ), cst_spec, s_spec,
                  _const_spec((GDN_K, c3)), _const_spec((1, c)), _const_spec((1, c)), _const_spec((1, c))],
        out_specs=[pl.BlockSpec((tm, c), lambda i: (i, 0)), s_spec, cst_spec],
        out_shape=[jax.ShapeDtypeStruct((nbat * ts, c), F32),
                   jax.ShapeDtypeStruct((nbat, N_HEADS, HEAD_DIM, HEAD_DIM), F32),
                   jax.ShapeDtypeStruct((nbat, SUBLANE, c3), F32)],
        compiler_params=_cparams(1, 40),
        name="gdn_short",
    )(proj, proj, proj, proj, cst8, s0, w, alog_b, dtb_b, gn_b)


def _gdn_kernel(qkv_ref, ga_ref, gb_ref, z_ref, cst_ref, s0_ref, w_ref, alog_ref, dtb_ref, gn_ref,
                y_ref, sf_ref, csto_ref, s_scr, c_scr, qkv_s, g_ref, beta_ref, *, n_chunks):
    tc = qkv_ref.shape[0]
    c3 = qkv_ref.shape[1]
    c8 = _carry_in(pl.program_id(1), cst_ref, c_scr)
    u3 = qkv_ref[...].astype(F32).reshape(1, tc, c3)
    qkv_s[...] = _silu(_causal_conv(u3, c8, w_ref[...], GDN_K)).reshape(tc, c3)
    g_ref[...] = -jnp.exp(alog_ref[...]) * _softplus(ga_ref[...].astype(F32) + dtb_ref[...])
    beta_ref[...] = jax.nn.sigmoid(gb_ref[...].astype(F32))
    hist = u3[:, tc - SUBLANE:, :]
    c_scr[...] = hist
    csto_ref[...] = hist
    q_ref = qkv_s.at[:, 0:BRANCH_W]
    k_ref = qkv_s.at[:, BRANCH_W:2 * BRANCH_W]
    v_ref = qkv_s.at[:, 2 * BRANCH_W:3 * BRANCH_W]

    cc = GDN_CHUNK
    n2 = 2 * cc
    npair = N_HEADS // 2
    zero_blk = jnp.zeros((HEAD_DIM, HEAD_DIM), F32)

    @pl.when(pl.program_id(1) == 0)
    def _():
        for pr in range(npair):
            top = jnp.concatenate([s0_ref[0, 2 * pr], zero_blk], axis=-1)
            bot = jnp.concatenate([zero_blk, s0_ref[0, 2 * pr + 1]], axis=-1)
            s_scr[pr] = jnp.concatenate([top, bot], axis=0)

    rows = lax.broadcasted_iota(jnp.int32, (n2, n2), 0)
    cols = lax.broadcasted_iota(jnp.int32, (n2, n2), 1)
    same = (rows >> 6) == (cols >> 6)
    tril = same & (rows >= cols)
    strict = same & (rows > cols)
    masks = _inverse_masks(n2)[:7]
    gn = gn_ref[...]
    probs = [(ci, pr) for ci in range(n_chunks) for pr in range(npair)]

    def rs(ci):
        return slice(ci * cc, (ci + 1) * cc)

    def ls(pr):
        return slice(pr * LANE, (pr + 1) * LANE)

    def stack(x):
        return jnp.where(same, jnp.concatenate([x, x], axis=0), 0.0)

    def twice(x):
        return jnp.concatenate([x, x], axis=0)

    gc_all = [_cumsum_rows(g_ref[rs(ci), :], cc) for ci in range(n_chunks)]
    gcm = [stack(gc_all[ci][:, ls(pr)]) for ci, pr in probs]
    beta = [twice(beta_ref[rs(ci), ls(pr)]) for ci, pr in probs]
    sol, intra, qe, k = _gdn_solve([stack(q_ref[rs(ci), ls(pr)]) for ci, pr in probs],
                                   [stack(k_ref[rs(ci), ls(pr)]) for ci, pr in probs],
                                   [stack(v_ref[rs(ci), ls(pr)]) for ci, pr in probs],
                                   gcm, beta, tril, strict, masks)
    g_last = [gc_all[ci][cc - 1:cc, ls(pr)] for ci, pr in probs]
    kdec = [k[p] * jnp.exp(g_last[p] - gcm[p]) for p in range(len(probs))]
    sdec = [jnp.exp(g_last[p]) for p in range(len(probs))]

    for ci in range(n_chunks):
        ps = [ci * npair + pr for pr in range(npair)]
        s_old = [s_scr[pr] for pr in range(npair)]
        v_new = [sol[p][:, :n2] - _bdot(sol[p][:, n2:], s_old[pr]) for pr, p in enumerate(ps)]
        o_st = [_bdot(qe[p], s_old[pr]) for pr, p in enumerate(ps)]
        o_in = [_bdot(intra[p], v_new[pr]) for pr, p in enumerate(ps)]
        s_up = [_bdot_tn(kdec[p], v_new[pr]) for pr, p in enumerate(ps)]
        for pr, p in enumerate(ps):
            s_scr[pr] = s_old[pr] * sdec[p] + s_up[pr]
            o = o_st[pr] + o_in[pr]
            o = o * lax.rsqrt(jnp.sum(o * o, axis=-1, keepdims=True) * (1.0 / HEAD_DIM) + EPS)
            o = o[:cc] + o[cc:]
            y_ref[rs(ci), ls(pr)] = o * gn[:, ls(pr)] * _silu(z_ref[rs(ci), ls(pr)].astype(F32))

    @pl.when(pl.program_id(1) == pl.num_programs(1) - 1)
    def _():
        for pr in range(npair):
            s_pair = s_scr[pr]
            sf_ref[0, 2 * pr] = s_pair[:HEAD_DIM, :HEAD_DIM]
            sf_ref[0, 2 * pr + 1] = s_pair[HEAD_DIM:, HEAD_DIM:]


def _gdn(proj, cst8, s0, w, alog_b, dtb_b, gn_b, nbat, t, tc):
    tper = t // tc
    c = BRANCH_W
    c3 = 3 * c
    kern = functools.partial(_gdn_kernel, n_chunks=tc // GDN_CHUNK)
    s_spec = pl.BlockSpec((1, N_HEADS, HEAD_DIM, HEAD_DIM), lambda b, t_: (b, 0, 0, 0))
    cst_spec = pl.BlockSpec((1, SUBLANE, c3), lambda b, t_: (b, 0, 0))
    return pl.pallas_call(
        kern,
        grid=(nbat, tper),
        in_specs=[_tok_spec(tc, c3, OFF_GQKV // c3, tper), _tok_spec(tc, c, OFF_GA // c, tper),
                  _tok_spec(tc, c, OFF_GB // c, tper), _tok_spec(tc, c, OFF_GZ // c, tper),
                  cst_spec, s_spec, _const_spec((GDN_K, c3)), _const_spec((1, c)), _const_spec((1, c)),
                  _const_spec((1, c))],
        out_specs=[_tok_spec(tc, c, 0, tper), s_spec, cst_spec],
        out_shape=[jax.ShapeDtypeStruct((nbat * t, c), F32),
                   jax.ShapeDtypeStruct((nbat, N_HEADS, HEAD_DIM, HEAD_DIM), F32),
                   jax.ShapeDtypeStruct((nbat, SUBLANE, c3), F32)],
        scratch_shapes=[pltpu.VMEM((N_HEADS // 2, 2 * HEAD_DIM, 2 * HEAD_DIM), F32),
                        pltpu.VMEM((1, SUBLANE, c3), F32), pltpu.VMEM((tc, c3), F32),
                        pltpu.VMEM((tc, c), F32), pltpu.VMEM((tc, c), F32)],
        compiler_params=_cparams(2, 40),
        name="gdn_rule",
    )(proj, proj, proj, proj, cst8, s0, w, alog_b, dtb_b, gn_b)


def _merge_kernel(x_ref, mod_ref, g_ref, ya_ref, yb_ref, yc_ref, yd_ref, wg_ref, wbo_ref, wmix_ref, o_ref):
    x = x_ref[...]
    nb, tt, d = x.shape
    mod = mod_ref[...]
    h = _modulated_norm(x, g_ref[...], mod, 0, 1).reshape(nb * tt, d).astype(BF16)
    acc = None
    for n, y_ref in enumerate((ya_ref, yb_ref, yc_ref, yd_ref)):
        gate = jax.nn.sigmoid(jnp.dot(h, wg_ref[:, n * d:(n + 1) * d], preferred_element_type=F32))
        term = gate * _bdot(y_ref[...], wbo_ref[n])
        acc = term if acc is None else acc + term
    mix = _bdot(acc, wmix_ref[...])
    o_ref[...] = x + mod[:, 2:3, :] * mix.reshape(nb, tt, d)


def _merge(x3, mod, g, ys, wg, wbo, wmix, nb, tt):
    nbat, t, d = x3.shape
    tper = t // tt
    tm = nb * tt
    c = BRANCH_W
    x_spec = pl.BlockSpec((nb, tt, d), lambda b, t_: (b, t_, 0))
    return pl.pallas_call(
        _merge_kernel,
        grid=(nbat // nb, tper),
        in_specs=[x_spec, pl.BlockSpec((nb, 6, d), lambda b, t_: (b, 0, 0)), _const_spec((1, d))]
                 + [_tok_spec(tm, c, 0, tper)] * 4
                 + [_const_spec(wg.shape), _const_spec(wbo.shape), _const_spec(wmix.shape)],
        out_specs=x_spec,
        out_shape=jax.ShapeDtypeStruct(x3.shape, F32),
        compiler_params=_cparams(2, 48),
        name="merge",
    )(x3, mod, g, *ys, wg, wbo, wmix)


def _ffn_kernel(x_ref, mod_ref, g_ref, win_ref, wout_ref, gf_ref, o_ref, *, final):
    x = x_ref[...]
    nb, tt, d = x.shape
    mod = mod_ref[...]
    fh = wout_ref.shape[0]
    h = _modulated_norm(x, g_ref[...], mod, 3, 4).reshape(nb * tt, d).astype(BF16)
    gate = jnp.dot(h, win_ref[:, :fh], preferred_element_type=F32)
    up = jnp.dot(h, win_ref[:, fh:], preferred_element_type=F32)
    out = _bdot(_silu(gate) * up, wout_ref[...])
    xo = x + mod[:, 5:6, :] * out.reshape(nb, tt, d)
    o_ref[...] = _rms(xo, gf_ref[...]) if final else xo


def _ffn(x3, mod, g, win, wout, gf, nb, tt, final):
    nbat, t, d = x3.shape
    x_spec = pl.BlockSpec((nb, tt, d), lambda b, t_: (b, t_, 0))
    kern = functools.partial(_ffn_kernel, final=final)
    return pl.pallas_call(
        kern,
        grid=(nbat // nb, t // tt),
        in_specs=[x_spec, pl.BlockSpec((nb, 6, d), lambda b, t_: (b, 0, 0)), _const_spec((1, d)),
                  _const_spec(win.shape), _const_spec(wout.shape), _const_spec((1, d))],
        out_specs=x_spec,
        out_shape=jax.ShapeDtypeStruct(x3.shape, F32),
        compiler_params=_cparams(2, 56),
        name="ffn",
    )(x3, mod, g, win, wout, gf)


def _block_diag(blocks):
    h, r, c = blocks.shape
    eye = jnp.eye(h, dtype=blocks.dtype)
    return jnp.einsum('hrc,hg->hrgc', blocks, eye).reshape(h * r, h * c)


def _prep_layer_weights(w_in, w_qb, w_kvb, w_lru_gate_a, w_lru_gate_x):
    d = w_in.shape[0]
    c = BRANCH_W
    o = 0
    segs = {}
    for name, width in (('sc', 3 * c), ('qa', MLA_QLORA), ('ckv', MLA_KVLORA), ('kpe', MLA_ROPE), ('gqkv', 3 * c),
                        ('gz', c), ('ga', N_HEADS), ('gb', N_HEADS), ('lx', c), ('lg', c)):
        segs[name] = w_in[:, o:o + width]
        o += width
    half = MLA_ROPE // 2
    zpad = jnp.zeros((d, LANE - MLA_ROPE), F32)
    kpe = segs['kpe']
    kpa = jnp.concatenate([kpe, zpad], axis=1)
    kpb = jnp.concatenate([-kpe[:, half:], kpe[:, :half], zpad], axis=1)
    w_proj = jnp.concatenate(
        [segs['sc'], segs['gqkv'], segs['gz'], jnp.repeat(segs['ga'], HEAD_DIM, axis=1),
         jnp.repeat(segs['gb'], HEAD_DIM, axis=1), segs['lx'], segs['lg'], segs['qa'], segs['ckv'], kpa, kpb],
        axis=1).astype(BF16)

    ql = w_qb.shape[0]
    wq = w_qb.reshape(ql, N_HEADS, HEAD_DIM + MLA_ROPE)
    wn = wq[:, :, :HEAD_DIM].reshape(ql, c)
    pe = wq[:, :, HEAD_DIM:]
    zq = jnp.zeros((ql, N_HEADS, LANE - MLA_ROPE), F32)
    wpa = jnp.concatenate([pe, zq], axis=2).reshape(ql, N_HEADS * LANE)
    wpb = jnp.concatenate([-pe[:, :, half:], pe[:, :, :half], zq], axis=2).reshape(ql, N_HEADS * LANE)
    w_kb = w_kvb[:, :, :HEAD_DIM]
    w_vb = w_kvb[:, :, HEAD_DIM:]
    bdk = _block_diag(jnp.transpose(w_kb, (1, 2, 0)))
    bdv = _block_diag(jnp.transpose(w_vb, (1, 0, 2)))
    wlg = jnp.concatenate([_block_diag(w_lru_gate_a), _block_diag(w_lru_gate_x)], axis=1)
    wvt = jnp.transpose(w_vb, (1, 2, 0))
    return dict(w_proj=w_proj, wn=wn.astype(BF16), wpa=wpa.astype(BF16), wpb=wpb.astype(BF16),
                bdk=bdk.astype(BF16), bdv=bdv.astype(BF16), wvt=wvt.astype(BF16), wlg=wlg.astype(BF16))


def _rope_tables(pos):
    half = MLA_ROPE // 2
    inv = ROPE_THETA ** (-jnp.arange(half, dtype=F32) / half)
    ang = pos.astype(F32)[:, None] * inv[None, :]
    pad = jnp.zeros((pos.shape[0], LANE - MLA_ROPE), F32)
    cos = jnp.concatenate([jnp.cos(ang), jnp.cos(ang), pad], axis=1)
    sin = jnp.concatenate([jnp.sin(ang), jnp.sin(ang), pad], axis=1)
    return cos, sin


def _pad_state(st, k_w):
    return jnp.pad(st, ((0, 0), (SUBLANE - (k_w - 1), 0), (0, 0)))


def _tile(t, pref):
    tt = min(t, pref)
    while t % tt:
        tt //= 2
    return tt


def _group_layer(x3, mod, lw, p, st, rope, attend, cfg):
    nbat, t, d = x3.shape
    nb, tt = cfg['nb'], cfg['tt']
    proj = _inproj(x3, mod, p['g_norm_mix'], lw['w_proj'], cfg['nb_proj'], cfg['tt_proj'])
    y_a, sc8 = _sconv(proj, st['sconv'], p['w_sc_conv'], nbat, t, nb, tt)
    q_full, ckv, kpe, kf, *vt = _mla_pre(proj, rope[0], rope[1], p['g_q_norm'], p['g_kv_norm'], lw['wn'], lw['wpa'],
                                         lw['wpb'], lw['bdk'], nbat, t, nb, tt, cfg['q_dtype'], cfg['emit_vt'])
    y_b = attend(q_full, ckv, kpe, kf, vt, lw)
    gdn_w = (p['w_gdn_conv'], p['alog_b'], p['dtb_b'], p['gn_b'])
    if t < GDN_CHUNK:
        y_c, s_gdn, gc8 = _gdn_short(proj, st['gdn_conv'], st['gdn'], *gdn_w, nbat, t)
    else:
        y_c, s_gdn, gc8 = _gdn(proj, st['gdn_conv'], st['gdn'], *gdn_w, nbat, t, cfg['tc'])
    y_d, lc8, h_last = _lru(proj, st['lru_conv'], st['lru'], p['w_lru_conv'], p['b_lru_conv'], lw['wlg'],
                            p['b_lru_gates'], p['nsl'], nbat, t, cfg['nb_scan'], cfg['tt_scan'], cfg['pos0'])
    x1 = _merge(x3, mod, p['g_norm_mix'], (y_a, y_b, y_c, y_d), p['w_merge_gate'], p['w_branch_out'],
                p['w_mix_out'], cfg['nb_mm'], cfg['tt_mm'])
    x2 = _ffn(x1, mod, p['g_norm_ffn'], p['w_ffn_in'], p['w_ffn_out'], p['g_final'], cfg['nb_mm'], cfg['tt_mm'],
              cfg['final'])
    new_st = dict(ckv=ckv.reshape(nbat, t, -1), kpe=kpe.reshape(nbat, t, -1),
                  sconv=sc8[:, SUBLANE - (SC_K - 1):], gdn_conv=gc8[:, SUBLANE - (GDN_K - 1):], gdn=s_gdn,
                  lru_conv=lc8[:, SUBLANE - (LRU_K - 1):], lru=h_last[:, 0, :])
    return x2, new_st


STATE_KEYS = ('ckv', 'kpe', 'sconv', 'gdn_conv', 'gdn', 'lru_conv', 'lru')


def kernel(x_prompt, x_sample, c_prompt, c_sample, cache_mla_ckv, cache_mla_kpe, page_table, state_sconv, state_gdn_conv, state_gdn, state_lru_conv, state_lru, w_ada, b_ada, g_norm_mix, g_norm_ffn, w_in, w_sc_conv, g_q_norm, w_qb, g_kv_norm, w_kvb, w_gdn_conv, gdn_a_log, gdn_dt_bias, g_gdn_norm, w_lru_conv, b_lru_conv, w_lru_gate_a, b_lru_gate_a, w_lru_gate_x, b_lru_gate_x, lru_lambda, w_branch_out, w_merge_gate, w_mix_out, w_ffn_in, w_ffn_out, g_final):
    bp, tp, d = x_prompt.shape
    bs, ts, _ = x_sample.shape
    depth = w_in.shape[0]
    n_pages = page_table.shape[1]
    past_len = n_pages * cache_mla_ckv.shape[2]
    c = BRANCH_W
    assert ts == SUBLANE and tp % GDN_CHUNK == 0 and d % LANE == 0

    mod_all = _ada(jnp.concatenate([c_prompt, c_sample], axis=0), w_ada, b_ada).reshape(depth, bp + bs, 6, d)

    nb_s = _tile(bs, 32)
    cos_p, sin_p = _rope_tables(jnp.arange(tp, dtype=jnp.int32))
    cos_s, sin_s = _rope_tables(past_len + jnp.arange(ts, dtype=jnp.int32))
    rope_p = (cos_p, sin_p)
    rope_s = (jnp.tile(cos_s, (nb_s, 1)), jnp.tile(sin_s, (nb_s, 1)))

    tq = _tile(tp, 256)
    tk = _tile(tp, 256)
    cfg_p = dict(nb=1, tt=tk, nb_proj=1, tt_proj=_tile(tp, 1024), q_dtype=BF16, emit_vt=True,
                 tc=_tile(tp, 256), pos0=0, nb_scan=1, tt_scan=_tile(tp, 256), nb_mm=1, tt_mm=_tile(tp, 512))
    cfg_s = dict(nb=nb_s, tt=ts, nb_proj=_tile(bs, 128), tt_proj=ts, q_dtype=F32, emit_vt=False,
                 tc=GDN_CHUNK, pos0=past_len, nb_scan=nb_s, tt_scan=ts, nb_mm=_tile(bs, 64), tt_mm=ts)
    cache_kpe_t = jnp.swapaxes(cache_mla_kpe, 2, 3)

    xp, xs = x_prompt, x_sample
    out_p = {k: [] for k in STATE_KEYS}
    out_s = {k: [] for k in STATE_KEYS}
    for l in range(depth):
        lw = _prep_layer_weights(w_in[l], w_qb[l], w_kvb[l], w_lru_gate_a[l], w_lru_gate_x[l])
        p = dict(
            g_norm_mix=g_norm_mix[l][None], g_norm_ffn=g_norm_ffn[l][None], g_final=g_final[None],
            w_sc_conv=w_sc_conv[l], g_q_norm=g_q_norm[l][None], g_kv_norm=g_kv_norm[l][None],
            w_gdn_conv=w_gdn_conv[l],
            alog_b=jnp.repeat(gdn_a_log[l], HEAD_DIM)[None], dtb_b=jnp.repeat(gdn_dt_bias[l], HEAD_DIM)[None],
            gn_b=jnp.tile(g_gdn_norm[l], N_HEADS)[None],
            w_lru_conv=w_lru_conv[l], b_lru_conv=b_lru_conv[l][None],
            b_lru_gates=jnp.concatenate([b_lru_gate_a[l], b_lru_gate_x[l]])[None],
            nsl=lru_lambda[l][None],
            w_merge_gate=w_merge_gate[l].astype(BF16), w_branch_out=w_branch_out[l].astype(BF16),
            w_mix_out=w_mix_out[l].astype(BF16), w_ffn_in=w_ffn_in[l].astype(BF16),
            w_ffn_out=w_ffn_out[l].astype(BF16))
        final = l == depth - 1

        st_p = dict(sconv=jnp.zeros((bp, SUBLANE, c), F32), gdn_conv=jnp.zeros((bp, SUBLANE, 3 * c), F32),
                    gdn=jnp.zeros((bp, N_HEADS, HEAD_DIM, HEAD_DIM), F32),
                    lru_conv=jnp.zeros((bp, SUBLANE, c), F32), lru=jnp.zeros((bp, 1, c), F32))

        def attend_p(q_full, ckv, kpe, kf, vt, lw_):
            return _attn_prompt(q_full, kf, vt[0], lw_['wvt'], bp, tp, tq, tk)

        xp, nst_p = _group_layer(xp, mod_all[l, :bp], lw, p, st_p, rope_p, attend_p, dict(cfg_p, final=final))

        st_s = dict(sconv=_pad_state(state_sconv[l], SC_K), gdn_conv=_pad_state(state_gdn_conv[l], GDN_K),
                    gdn=state_gdn[l], lru_conv=_pad_state(state_lru_conv[l], LRU_K), lru=state_lru[l][:, None, :])

        def attend_s(q_full, ckv, kpe, kf, vt, lw_, layer=l):
            return _attn_sample(q_full, cache_mla_ckv, cache_kpe_t, page_table, layer, ckv, kpe, lw_['bdv'], bs, ts)

        xs, nst_s = _group_layer(xs, mod_all[l, bp:], lw, p, st_s, rope_s, attend_s, dict(cfg_s, final=final))
        for k in STATE_KEYS:
            out_p[k].append(nst_p[k])
            out_s[k].append(nst_s[k])

    return ((xp, xs) + tuple(jnp.stack(out_p[k]) for k in STATE_KEYS)
            + tuple(jnp.stack(out_s[k]) for k in STATE_KEYS))
```

```python
import functools
import math

import jax
import jax.numpy as jnp
from jax import lax
from jax.experimental import pallas as pl
from jax.experimental.pallas import tpu as pltpu

F32 = jnp.float32
BF16 = jnp.bfloat16

HEAD_DIM = 64
N_HEADS = 8
BRANCH_W = N_HEADS * HEAD_DIM
MLA_ROPE = HEAD_DIM // 2
MLA_QLORA = 256
MLA_KVLORA = 128
ROPE_THETA = 10000.0
SC_K = 3
GDN_K = 4
LRU_K = 4
LRU_C = 8.0
GDN_CHUNK = 64
EPS = 1e-6
NEG_BIG = -1e30
LANE = 128
SUBLANE = 8
QK_W = 2 * LANE
ATTN_GROUP_W = 256

OFF_SC = 0
OFF_GQKV = 1536
OFF_GZ = 3072
OFF_GA = 3584
OFF_GB = 4096
OFF_LX = 4608
OFF_LG = 5120
OFF_QA = 5632
OFF_CKV = 5888
OFF_KPA = 6016
OFF_KPB = 6144
PROJ_W = 6272
PROJ_TN = 896


def _cparams(n_axes, vmem_mib):
    return pltpu.CompilerParams(dimension_semantics=("arbitrary",) * n_axes,
                                vmem_limit_bytes=vmem_mib * 1024 * 1024)


def _const_spec(shape):
    nd = len(shape)
    return pl.BlockSpec(shape, lambda *_: (0,) * nd, pipeline_mode=pl.Buffered(1))


def _bdot(a, b):
    return jnp.dot(a.astype(BF16), b.astype(BF16), preferred_element_type=F32)


def _bdot_nt(a, b):
    return lax.dot_general(a.astype(BF16), b.astype(BF16), (((1,), (1,)), ((), ())),
                           preferred_element_type=F32)


def _bdot_tn(a, b):
    return lax.dot_general(a.astype(BF16), b.astype(BF16), (((0,), (0,)), ((), ())),
                           preferred_element_type=F32)


def _silu(x):
    return x * jax.nn.sigmoid(x)


def _softplus(x):
    return jnp.maximum(x, 0.0) + jnp.log(1.0 + jnp.exp(-jnp.abs(x)))


def _gelu_tanh(x):
    return 0.5 * x * (1.0 + jnp.tanh(math.sqrt(2.0 / math.pi) * (x + 0.044715 * (x * x * x))))


def _rms(x, g):
    return x * lax.rsqrt(jnp.mean(x * x, axis=-1, keepdims=True) + EPS) * g


def _modulated_norm(x3, g, mod, shift_row, scale_row):
    y = _rms(x3, g)
    return y * (1.0 + mod[:, scale_row:scale_row + 1, :]) + mod[:, shift_row:shift_row + 1, :]


def _ada_kernel(c_ref, w_ref, b_ref, o_ref):
    o_ref[0] = _bdot(_silu(c_ref[...]), w_ref[0]) + b_ref[0]


def _ada(c_all, w_ada, b_ada):
    depth, d, n = w_ada.shape
    nb = c_all.shape[0]
    tn = n // 4
    return pl.pallas_call(
        _ada_kernel,
        grid=(depth, n // tn),
        in_specs=[pl.BlockSpec((nb, d), lambda l, j: (0, 0)),
                  pl.BlockSpec((1, d, tn), lambda l, j: (l, 0, j)),
                  pl.BlockSpec((1, 1, tn), lambda l, j: (l, 0, j))],
        out_specs=pl.BlockSpec((1, nb, tn), lambda l, j: (l, 0, j)),
        out_shape=jax.ShapeDtypeStruct((depth, nb, n), F32),
        compiler_params=_cparams(2, 40),
        name="ada_mod",
    )(c_all, w_ada, b_ada.reshape(depth, 1, n))


def _inproj_kernel(x_ref, mod_ref, g_ref, w_ref, o_ref, h_scr):
    @pl.when(pl.program_id(1) == 0)
    def _():
        h = _modulated_norm(x_ref[...], g_ref[...], mod_ref[...], 0, 1)
        h_scr[...] = h.reshape(h_scr.shape).astype(BF16)

    o_ref[...] = jnp.dot(h_scr[...], w_ref[...], preferred_element_type=F32).astype(o_ref.dtype)


def _inproj(x3, mod, g, w, nb, tt):
    nbat, t, d = x3.shape
    tper = t // tt
    tm = nb * tt
    n_m = (nbat // nb) * tper
    return pl.pallas_call(
        _inproj_kernel,
        grid=(n_m, PROJ_W // PROJ_TN),
        in_specs=[pl.BlockSpec((nb, tt, d), lambda i, j: (i // tper, i % tper, 0)),
                  pl.BlockSpec((nb, 6, d), lambda i, j: (i // tper, 0, 0)),
                  _const_spec((1, d)),
                  pl.BlockSpec((d, PROJ_TN), lambda i, j: (0, j))],
        out_specs=pl.BlockSpec((tm, PROJ_TN), lambda i, j: (i, j)),
        out_shape=jax.ShapeDtypeStruct((nbat * t, PROJ_W), BF16),
        scratch_shapes=[pltpu.VMEM((tm, d), BF16)],
        compiler_params=_cparams(2, 40),
        name="in_proj",
    )(x3, mod, g, w)


def _causal_conv(u3, c8, w, k_w):
    nb, tt, c = u3.shape
    rows = lax.broadcasted_iota(jnp.int32, (nb, SUBLANE, c), 1)
    acc = None
    for j in range(k_w):
        s = k_w - 1 - j
        if s == 0:
            sh = u3
        else:
            full = pltpu.roll(u3, s, axis=1)
            top = jnp.where(rows < s, pltpu.roll(c8, s, axis=1), full[:, 0:SUBLANE, :])
            sh = top if tt == SUBLANE else jnp.concatenate([top, full[:, SUBLANE:, :]], axis=1)
        term = sh * w[j:j + 1, :]
        acc = term if acc is None else acc + term
    return acc


def _carry_in(t_idx, st_ref, c_scr):
    @pl.when(t_idx == 0)
    def _():
        c_scr[...] = st_ref[...]

    return c_scr[...]


def _sconv_kernel(bg_ref, cg_ref, xt_ref, st_ref, w_ref, y_ref, sto_ref, c_scr, *, nb, tt):
    c = bg_ref.shape[-1]
    c8 = _carry_in(pl.program_id(1), st_ref, c_scr)
    u3 = (cg_ref[...].astype(F32) * xt_ref[...].astype(F32)).reshape(nb, tt, c)
    conv = _causal_conv(u3, c8, w_ref[...], SC_K)
    y_ref[...] = bg_ref[...].astype(F32) * conv.reshape(nb * tt, c)
    last = u3[:, tt - SUBLANE:, :]
    c_scr[...] = last
    sto_ref[...] = last


def _tok_spec(tm, width, col_block, tper):
    return pl.BlockSpec((tm, width), lambda b, t: (b * tper + t, col_block))


def _state_spec(nb, c):
    return pl.BlockSpec((nb, SUBLANE, c), lambda b, t: (b, 0, 0))


def _sconv(proj, st8, w, nbat, t, nb, tt):
    tper = t // tt
    tm = nb * tt
    c = BRANCH_W
    kern = functools.partial(_sconv_kernel, nb=nb, tt=tt)
    return pl.pallas_call(
        kern,
        grid=(nbat // nb, tper),
        in_specs=[_tok_spec(tm, c, OFF_SC // c, tper), _tok_spec(tm, c, OFF_SC // c + 1, tper),
                  _tok_spec(tm, c, OFF_SC // c + 2, tper), _state_spec(nb, c), _const_spec((SC_K, c))],
        out_specs=[_tok_spec(tm, c, 0, tper), _state_spec(nb, c)],
        out_shape=[jax.ShapeDtypeStruct((nbat * t, c), F32), jax.ShapeDtypeStruct((nbat, SUBLANE, c), F32)],
        scratch_shapes=[pltpu.VMEM((nb, SUBLANE, c), F32)],
        compiler_params=_cparams(2, 40),
        name="short_conv",
    )(proj, proj, proj, st8, w)


def _lru_kernel(lx_ref, lg_ref, st_ref, h0_ref, w_ref, cb_ref, wg_ref, bg_ref, lam_ref,
                y_ref, sto_ref, hl_ref, c_scr, h_scr, *, nb, tt, pos0):
    c = lx_ref.shape[-1]
    t_idx = pl.program_id(1)
    c8 = _carry_in(t_idx, st_ref, c_scr)

    @pl.when(t_idx == 0)
    def _():
        h_scr[...] = h0_ref[...]

    u3 = lx_ref[...].astype(F32).reshape(nb, tt, c)
    u = (_causal_conv(u3, c8, w_ref[...], LRU_K) + cb_ref[...]).reshape(nb * tt, c)
    last = u3[:, tt - SUBLANE:, :]
    c_scr[...] = last
    sto_ref[...] = last
    gates = jax.nn.sigmoid(_bdot(u, wg_ref[...]) + bg_ref[...])
    r = gates[:, :c]
    ig = gates[:, c:]
    log_a = -LRU_C * r * _softplus(-lam_ref[...])
    rows = lax.broadcasted_iota(jnp.int32, (nb, tt, c), 1)
    mult = jnp.sqrt(1.0 - jnp.exp(2.0 * log_a)).reshape(nb, tt, c)
    mult = jnp.where(pos0 + t_idx * tt + rows == 0, 1.0, mult)
    grp = tt // SUBLANE
    a = jnp.exp(log_a).reshape(nb * grp, SUBLANE, c)
    b = (mult * (ig * u).reshape(nb, tt, c)).reshape(nb * grp, SUBLANE, c)
    tok = lax.broadcasted_iota(jnp.int32, (nb * grp, SUBLANE, c), 1)
    d = 1
    while d < SUBLANE:
        keep = tok >= d
        a_s = jnp.where(keep, pltpu.roll(a, d, axis=1), 1.0)
        b_s = jnp.where(keep, pltpu.roll(b, d, axis=1), 0.0)
        b = a * b_s + b
        a = a * a_s
        d *= 2
    a = a.reshape(nb, grp, SUBLANE, c)
    b = b.reshape(nb, grp, SUBLANE, c)
    h_in = h_scr[...]
    parts = []
    for g in range(grp):
        part = b[:, g] + a[:, g] * h_in
        h_in = part[:, SUBLANE - 1:SUBLANE, :]
        parts.append(part)
    hs = parts[0] if grp == 1 else jnp.concatenate(parts, axis=1)
    h_last = h_in
    h_scr[...] = h_last
    hl_ref[...] = h_last
    y_ref[...] = hs.reshape(nb * tt, c) * _gelu_tanh(lg_ref[...].astype(F32))


def _lru(proj, st8, h0, w, cb, wg, bg, lam, nbat, t, nb, tt, pos0):
    tper = t // tt
    tm = nb * tt
    c = BRANCH_W
    kern = functools.partial(_lru_kernel, nb=nb, tt=tt, pos0=pos0)
    h_spec = pl.BlockSpec((nb, 1, c), lambda b, t_: (b, 0, 0))
    return pl.pallas_call(
        kern,
        grid=(nbat // nb, tper),
        in_specs=[_tok_spec(tm, c, OFF_LX // c, tper), _tok_spec(tm, c, OFF_LG // c, tper), _state_spec(nb, c),
                  h_spec, _const_spec((LRU_K, c)), _const_spec((1, c)), _const_spec((c, 2 * c)),
                  _const_spec((1, 2 * c)), _const_spec((1, c))],
        out_specs=[_tok_spec(tm, c, 0, tper), _state_spec(nb, c), h_spec],
        out_shape=[jax.ShapeDtypeStruct((nbat * t, c), F32), jax.ShapeDtypeStruct((nbat, SUBLANE, c), F32),
                   jax.ShapeDtypeStruct((nbat, 1, c), F32)],
        scratch_shapes=[pltpu.VMEM((nb, SUBLANE, c), F32), pltpu.VMEM((nb, 1, c), F32)],
        compiler_params=_cparams(2, 40),
        name="lru",
    )(proj, proj, st8, h0, w, cb, wg, bg, lam)


def _mla_pre_kernel(qa_ref, ckv_ref, kpa_ref, kpb_ref, cos_ref, sin_ref, gq_ref, gkv_ref,
                    wn_ref, wpa_ref, wpb_ref, bdk_ref, q_out, ckv_out, kpe_out, kf_out, *rest, nb, tt):
    cos = cos_ref[...]
    sin = sin_ref[...]
    cq = _rms(qa_ref[...].astype(F32), gq_ref[...]).astype(BF16)
    qn = jnp.dot(cq, wn_ref[...], preferred_element_type=F32)
    qabs = _bdot(qn, bdk_ref[...])
    pa = jnp.dot(cq, wpa_ref[...], preferred_element_type=F32)
    pb = jnp.dot(cq, wpb_ref[...], preferred_element_type=F32)
    for h in range(N_HEADS):
        sl = slice(h * LANE, (h + 1) * LANE)
        q_pe = pa[:, sl] * cos + pb[:, sl] * sin
        q_out[:, h, :, 0:LANE] = qabs[:, sl].reshape(nb, tt, LANE).astype(q_out.dtype)
        q_out[:, h, :, LANE:QK_W] = q_pe.reshape(nb, tt, LANE).astype(q_out.dtype)
    ckv = _rms(ckv_ref[...].astype(F32), gkv_ref[...])
    kpe = kpa_ref[...].astype(F32) * cos + kpb_ref[...].astype(F32) * sin
    ckv_out[...] = ckv
    kpe_out[...] = kpe[:, :MLA_ROPE]
    kf_out[...] = jnp.concatenate([ckv, kpe], axis=-1).astype(BF16)
    if rest:
        rest[0][0] = ckv.T.astype(BF16)


def _mla_pre(proj, cos, sin, gq, gkv, wn, wpa, wpb, bdk, nbat, t, nb, tt, q_dtype, emit_vt):
    tper = t // tt
    tm = nb * tt
    hw = N_HEADS * LANE
    kern = functools.partial(_mla_pre_kernel, nb=nb, tt=tt)
    tab_spec = pl.BlockSpec((tm, LANE), lambda b, t_: (t_, 0))
    vt_specs, vt_shapes = [], []
    if emit_vt:
        assert nb == 1
        vt_specs = [pl.BlockSpec((1, MLA_KVLORA, tt), lambda b, t_: (b * tper + t_, 0, 0))]
        vt_shapes = [jax.ShapeDtypeStruct((nbat * tper, MLA_KVLORA, tt), BF16)]
    return pl.pallas_call(
        kern,
        grid=(nbat // nb, tper),
        in_specs=[_tok_spec(tm, MLA_QLORA, OFF_QA // MLA_QLORA, tper), _tok_spec(tm, LANE, OFF_CKV // LANE, tper),
                  _tok_spec(tm, LANE, OFF_KPA // LANE, tper), _tok_spec(tm, LANE, OFF_KPB // LANE, tper),
                  tab_spec, tab_spec, _const_spec((1, MLA_QLORA)), _const_spec((1, MLA_KVLORA)),
                  _const_spec((MLA_QLORA, BRANCH_W)), _const_spec((MLA_QLORA, hw)), _const_spec((MLA_QLORA, hw)),
                  _const_spec((BRANCH_W, hw))],
        out_specs=[pl.BlockSpec((nb, N_HEADS, tt, QK_W), lambda b, t_: (b, 0, t_, 0)),
                   _tok_spec(tm, MLA_KVLORA, 0, tper), _tok_spec(tm, MLA_ROPE, 0, tper), _tok_spec(tm, QK_W, 0, tper)]
                  + vt_specs,
        out_shape=[jax.ShapeDtypeStruct((nbat, N_HEADS, t, QK_W), q_dtype),
                   jax.ShapeDtypeStruct((nbat * t, MLA_KVLORA), F32),
                   jax.ShapeDtypeStruct((nbat * t, MLA_ROPE), F32),
                   jax.ShapeDtypeStruct((nbat * t, QK_W), BF16)] + vt_shapes,
        compiler_params=_cparams(2, 48),
        name="mla_pre",
    )(proj, proj, proj, proj, cos, sin, gq, gkv, wn, wpa, wpb, bdk)


def _heads_to_lanes(o, rows_per_head):
    return jnp.concatenate([o[h * rows_per_head:(h + 1) * rows_per_head] for h in range(N_HEADS)], axis=-1)


def _attn_prompt_kernel(q_ref, kf_ref, vt_ref, wvt_ref, y_ref, m_scr, l_scr, acc_scr, s_scr, *, tq, tk, scale):
    i = pl.program_id(1)
    n_full = (i * tq) // tk
    hpg = ATTN_GROUP_W // tq
    w2 = hpg * tq
    krow = lax.broadcasted_iota(jnp.int32, (tk, w2), 0)
    qcol = i * tq + (lax.broadcasted_iota(jnp.int32, (tk, w2), 1) & (tq - 1))
    heads = range(N_HEADS // hpg)
    m_scr[...] = jnp.full(m_scr.shape, NEG_BIG, F32)
    l_scr[...] = jnp.zeros(l_scr.shape, F32)
    acc_scr[...] = jnp.zeros(acc_scr.shape, F32)

    def scores(j):
        kblk = kf_ref[pl.ds(pl.multiple_of(j * tk, tk), tk), :]
        return [lax.dot_general(kblk, q_ref[0, hpg * h:hpg * (h + 1)].reshape(w2, QK_W), (((1,), (1,)), ((), ())),
                                preferred_element_type=F32)
                for h in heads]

    c2 = scale * math.log2(math.e)

    def step(j, masked):
        vt = vt_ref[j]
        s = [s_scr[h] for h in heads]
        if masked:
            keep = j * tk + krow <= qcol
            s = [jnp.where(keep, s[h], NEG_BIG) for h in heads]
        else:
            s_next = scores(j + 1)
            for h in heads:
                s_scr[h] = s_next[h]
        m_old = [m_scr[h] for h in heads]
        m_new = [jnp.maximum(m_old[h], jnp.max(s[h], axis=0, keepdims=True)) for h in heads]
        p = [jnp.exp2((s[h] - m_new[h]) * c2) for h in heads]
        alpha = [jnp.exp2((m_old[h] - m_new[h]) * c2) for h in heads]
        pv = [jnp.dot(vt, p[h].astype(BF16), preferred_element_type=F32) for h in heads]
        for h in heads:
            m_scr[h] = m_new[h]
            l_scr[h] = alpha[h] * l_scr[h] + jnp.sum(p[h], axis=0, keepdims=True)
            acc_scr[h] = alpha[h] * acc_scr[h] + pv[h]

    def body(j, carry):
        step(j, False)
        return carry

    s_first = scores(0)
    for h in heads:
        s_scr[h] = s_first[h]
    lax.fori_loop(0, n_full, body, 0)
    step(n_full, True)
    o = [acc_scr[h] / l_scr[h] for h in heads]
    outs = [_bdot(wvt_ref[hh], o[hh // hpg][:, (hh % hpg) * tq:(hh % hpg + 1) * tq]) for hh in range(N_HEADS)]
    y_ref[...] = jnp.concatenate(outs, axis=0).T


def _attn_prompt(q_full, kf, vt, wvt, nbat, t, tq, tk):
    tper = t // tq
    kper = t // tk
    assert ATTN_GROUP_W % tq == 0
    ngrp = N_HEADS * tq // ATTN_GROUP_W
    scale = (HEAD_DIM + MLA_ROPE) ** -0.5
    kern = functools.partial(_attn_prompt_kernel, tq=tq, tk=tk, scale=scale)
    return pl.pallas_call(
        kern,
        grid=(nbat, tper),
        in_specs=[pl.BlockSpec((1, N_HEADS, tq, QK_W), lambda b, i: (b, 0, i, 0)),
                  pl.BlockSpec((t, QK_W), lambda b, i: (b, 0)),
                  pl.BlockSpec((kper, MLA_KVLORA, tk), lambda b, i: (b, 0, 0)),
                  _const_spec(wvt.shape)],
        out_specs=pl.BlockSpec((tq, BRANCH_W), lambda b, i: (b * tper + i, 0)),
        out_shape=jax.ShapeDtypeStruct((nbat * t, BRANCH_W), F32),
        scratch_shapes=[pltpu.VMEM((ngrp, 1, ATTN_GROUP_W), F32), pltpu.VMEM((ngrp, 1, ATTN_GROUP_W), F32),
                        pltpu.VMEM((ngrp, MLA_KVLORA, ATTN_GROUP_W), F32),
                        pltpu.VMEM((ngrp, tk, ATTN_GROUP_W), F32)],
        compiler_params=_cparams(2, 40),
        name="attn_prompt",
    )(q_full, kf, vt, wvt)


def _attn_sample_kernel(pt_ref, q_ref, *refs, n_pages, pg, ts, page, scale):
    ckv_pages = refs[:n_pages]
    kpe_pages = refs[n_pages:2 * n_pages]
    ckvn_ref, kpen_ref, bdv_ref, y_ref = refs[2 * n_pages:]
    r = N_HEADS * ts
    groups = range(n_pages // pg)

    q = q_ref[0].reshape(r, QK_W)
    q_lat = q[:, :MLA_KVLORA].astype(BF16)
    q_pe = q[:, MLA_KVLORA:MLA_KVLORA + MLA_ROPE].astype(BF16)

    c_all = [jnp.concatenate([ckv_pages[g * pg + i][0, 0].astype(BF16) for i in range(pg)], axis=0)
             for g in groups]
    kpt_all = [jnp.concatenate([kpe_pages[g * pg + i][0, 0].astype(BF16) for i in range(pg)], axis=1)
               for g in groups]
    s = [(lax.dot_general(q_lat, c_all[g], (((1,), (1,)), ((), ())), preferred_element_type=F32)
          + jnp.dot(q_pe, kpt_all[g], preferred_element_type=F32)) * scale for g in groups]
    pad = jnp.zeros((page - ts, MLA_KVLORA), F32)
    cn = jnp.concatenate([ckvn_ref[...], pad], axis=0).astype(BF16)
    kn = jnp.concatenate([kpen_ref[...], pad[:, :MLA_ROPE]], axis=0).astype(BF16)
    sn = (lax.dot_general(q_lat, cn, (((1,), (1,)), ((), ())), preferred_element_type=F32)
          + lax.dot_general(q_pe, kn, (((1,), (1,)), ((), ())), preferred_element_type=F32)) * scale
    qpos = lax.broadcasted_iota(jnp.int32, (r, page), 0) & (ts - 1)
    kpos = lax.broadcasted_iota(jnp.int32, (r, page), 1)
    s.append(jnp.where(kpos <= qpos, sn, NEG_BIG))
    vals = c_all + [cn]
    parts = range(len(s))
    m = [jnp.max(s[g], axis=-1, keepdims=True) for g in parts]
    p = [jnp.exp(s[g] - m[g]) for g in parts]
    l = [jnp.sum(p[g], axis=-1, keepdims=True) for g in parts]
    acc = [jnp.dot(p[g].astype(BF16), vals[g], preferred_element_type=F32) for g in parts]
    m_all = m[0]
    for g in parts[1:]:
        m_all = jnp.maximum(m_all, m[g])
    w = [jnp.exp(m[g] - m_all) for g in parts]
    l_all = w[0] * l[0]
    acc_all = w[0] * acc[0]
    for g in parts[1:]:
        l_all = l_all + w[g] * l[g]
        acc_all = acc_all + w[g] * acc[g]
    o = _heads_to_lanes(acc_all / l_all, ts)
    y_ref[...] = _bdot(o, bdv_ref[...])


def _attn_sample(q_full, cache_ckv, cache_kpe, page_table, layer, ckv_new, kpe_new, bdv, nbat, ts):
    n_pages = page_table.shape[1]
    page = cache_ckv.shape[2]
    pg = math.gcd(n_pages, 16)
    scale = (HEAD_DIM + MLA_ROPE) ** -0.5
    kern = functools.partial(_attn_sample_kernel, n_pages=n_pages, pg=pg, ts=ts, page=page, scale=scale)

    def page_spec(i, rows, width):
        return pl.BlockSpec((1, 1, rows, width), lambda b, pt: (layer, pt[b * n_pages + i], 0, 0))

    in_specs = ([pl.BlockSpec((1, N_HEADS, ts, QK_W), lambda b, pt: (b, 0, 0, 0))]
                + [page_spec(i, page, MLA_KVLORA) for i in range(n_pages)]
                + [page_spec(i, MLA_ROPE, page) for i in range(n_pages)]
                + [pl.BlockSpec((ts, MLA_KVLORA), lambda b, pt: (b, 0)),
                   pl.BlockSpec((ts, MLA_ROPE), lambda b, pt: (b, 0)),
                   pl.BlockSpec((N_HEADS * LANE, BRANCH_W), lambda b, pt: (0, 0))])
    grid_spec = pltpu.PrefetchScalarGridSpec(
        num_scalar_prefetch=1,
        grid=(nbat,),
        in_specs=in_specs,
        out_specs=pl.BlockSpec((ts, BRANCH_W), lambda b, pt: (b, 0)),
    )
    return pl.pallas_call(
        kern,
        grid_spec=grid_spec,
        out_shape=jax.ShapeDtypeStruct((nbat * ts, BRANCH_W), F32),
        compiler_params=_cparams(1, 40),
        name="attn_sample",
    )(page_table.reshape(-1), q_full, *([cache_ckv] * n_pages), *([cache_kpe] * n_pages), ckv_new, kpe_new, bdv)


def _cumsum_rows(x, n):
    rows = lax.broadcasted_iota(jnp.int32, x.shape, 0)
    d = 1
    while d < n:
        x = x + jnp.where(rows >= d, pltpu.roll(x, d, axis=0), 0.0)
        d *= 2
    return x


def _inverse_masks(n):
    rows = lax.broadcasted_iota(jnp.int32, (n, n), 0)
    cols = lax.broadcasted_iota(jnp.int32, (n, n), 1)
    masks = [rows == cols, (rows >> 1) == (cols >> 1)]
    k = 1
    while (1 << k) < n:
        masks.append(((rows >> k) ^ (cols >> k)) == 1)
        k += 1
    return masks


def _gdn_solve(q_raw, k_raw, v_raw, gcm, beta, tril, strict, masks):
    n = range(len(q_raw))
    eye = jnp.where(masks[0], 1.0, 0.0)
    q = [x * lax.rsqrt(jnp.sum(x * x, axis=-1, keepdims=True) + EPS) * (HEAD_DIM ** -0.5) for x in q_raw]
    k = [x * lax.rsqrt(jnp.sum(x * x, axis=-1, keepdims=True) + EPS) for x in k_raw]
    kb = [k[p] * beta[p] for p in n]
    decay = []
    for p in n:
        diff = gcm[p] - gcm[p].T
        decay.append(jnp.where(tril, jnp.exp(jnp.where(tril, diff, 0.0)), 0.0))
    kk = [_bdot_nt(kb[p], k[p]) for p in n]
    a_mat = [jnp.where(strict, kk[p] * decay[p], 0.0) for p in n]
    t = [eye - jnp.where(masks[1], a_mat[p], 0.0) for p in n]
    for m in masks[2:]:
        tl = [_bdot(t[p], jnp.where(m, a_mat[p], 0.0)) for p in n]
        t = [t[p] - _bdot(tl[p], t[p]) for p in n]
    egc = [jnp.exp(gcm[p]) for p in n]
    sol = [_bdot(t[p], jnp.concatenate([v_raw[p] * beta[p], kb[p] * egc[p]], axis=-1)) for p in n]
    qk = [_bdot_nt(q[p], k[p]) for p in n]
    intra = [jnp.where(tril, qk[p] * decay[p], 0.0) for p in n]
    qe = [q[p] * egc[p] for p in n]
    return sol, intra, qe, k


def _gdn_short_kernel(qkv_ref, ga_ref, gb_ref, z_ref, cst_ref, s0_ref, w_ref, alog_ref, dtb_ref, gn_ref,
                      y_ref, sf_ref, csto_ref, *, nseq, ts):
    cc = nseq * ts
    n2 = 2 * cc
    npair = N_HEADS // 2
    c3 = qkv_ref.shape[1]
    c = BRANCH_W
    u3 = qkv_ref[...].astype(F32).reshape(nseq, ts, c3)
    conv = _silu(_causal_conv(u3, cst_ref[...], w_ref[...], GDN_K)).reshape(cc, c3)
    csto_ref[...] = u3
    g3 = (-jnp.exp(alog_ref[...]) * _softplus(ga_ref[...].astype(F32) + dtb_ref[...])).reshape(nseq, ts, c)
    b_tile = jax.nn.sigmoid(gb_ref[...].astype(F32))
    tok = lax.broadcasted_iota(jnp.int32, (nseq, ts, c), 1)
    d = 1
    while d < ts:
        g3 = g3 + jnp.where(tok >= d, pltpu.roll(g3, d, axis=1), 0.0)
        d *= 2
    gc = g3.reshape(cc, c)
    g_tot = g3[:, ts - 1:ts, :]
    gl = jnp.broadcast_to(g_tot, (nseq, ts, c)).reshape(cc, c)

    rows = lax.broadcasted_iota(jnp.int32, (n2, n2), 0)
    cols = lax.broadcasted_iota(jnp.int32, (n2, n2), 1)
    same_head = (rows >> 6) == (cols >> 6)
    same_seq = (rows >> 3) == (cols >> 3)
    tril = same_seq & (rows >= cols)
    strict = same_seq & (rows > cols)
    masks = _inverse_masks(n2)[:4]
    gn = gn_ref[...]
    pairs = range(npair)

    def ls(pr):
        return slice(pr * LANE, (pr + 1) * LANE)

    def stack(x):
        return jnp.where(same_head, jnp.concatenate([x, x], axis=0), 0.0)

    def twice(x):
        return jnp.concatenate([x, x], axis=0)

    erow = (lax.broadcasted_iota(jnp.int32, (n2, nseq * n2), 0) >> 3) & (nseq - 1)
    eblk = lax.broadcasted_iota(jnp.int32, (n2, nseq * n2), 1) >> 7
    emask = erow == eblk

    def expand(x):
        return jnp.where(emask, jnp.concatenate([x] * nseq, axis=1), 0.0)

    gcm = [stack(gc[:, ls(pr)]) for pr in pairs]
    glm = [stack(gl[:, ls(pr)]) for pr in pairs]
    beta = [twice(b_tile[:, ls(pr)]) for pr in pairs]
    sol, intra, qe, k = _gdn_solve([stack(conv[:, ls(pr)]) for pr in pairs],
                                   [stack(conv[:, c + pr * LANE:c + (pr + 1) * LANE]) for pr in pairs],
                                   [stack(conv[:, 2 * c + pr * LANE:2 * c + (pr + 1) * LANE]) for pr in pairs],
                                   gcm, beta, tril, strict, masks)
    kdec = [k[pr] * jnp.exp(glm[pr] - gcm[pr]) for pr in pairs]
    zero_blk = jnp.zeros((HEAD_DIM, HEAD_DIM), F32)
    s_old = []
    sdec = []
    for pr in pairs:
        blocks = []
        decs = []
        for b in range(nseq):
            top = jnp.concatenate([s0_ref[b, 2 * pr], zero_blk], axis=-1)
            bot = jnp.concatenate([zero_blk, s0_ref[b, 2 * pr + 1]], axis=-1)
            blocks += [top, bot]
            decs.append(jnp.broadcast_to(jnp.exp(g_tot[b, :, ls(pr)]), (n2, LANE)))
        s_old.append(jnp.concatenate(blocks, axis=0))
        sdec.append(jnp.concatenate(decs, axis=0))
    v_new = [sol[pr][:, :n2] - _bdot(expand(sol[pr][:, n2:]), s_old[pr]) for pr in pairs]
    o_st = [_bdot(expand(qe[pr]), s_old[pr]) for pr in pairs]
    o_in = [_bdot(intra[pr], v_new[pr]) for pr in pairs]
    s_up = [_bdot_tn(expand(kdec[pr]), v_new[pr]) for pr in pairs]
    for pr in pairs:
        s_new = s_old[pr] * sdec[pr] + s_up[pr]
        for b in range(nseq):
            sf_ref[b, 2 * pr] = s_new[b * n2:b * n2 + HEAD_DIM, :HEAD_DIM]
            sf_ref[b, 2 * pr + 1] = s_new[b * n2 + HEAD_DIM:(b + 1) * n2, HEAD_DIM:]
        o = o_st[pr] + o_in[pr]
        o = o * lax.rsqrt(jnp.sum(o * o, axis=-1, keepdims=True) * (1.0 / HEAD_DIM) + EPS)
        o = o[:cc] + o[cc:]
        y_ref[:, ls(pr)] = o * gn[:, ls(pr)] * _silu(z_ref[:, ls(pr)].astype(F32))


def _gdn_short(proj, cst8, s0, w, alog_b, dtb_b, gn_b, nbat, ts):
    nseq = GDN_CHUNK // ts
    assert ts == SUBLANE and nbat % nseq == 0
    c = BRANCH_W
    c3 = 3 * c
    tm = nseq * ts
    kern = functools.partial(_gdn_short_kernel, nseq=nseq, ts=ts)
    s_spec = pl.BlockSpec((nseq, N_HEADS, HEAD_DIM, HEAD_DIM), lambda i: (i, 0, 0, 0))
    cst_spec = pl.BlockSpec((nseq, SUBLANE, c3), lambda i: (i, 0, 0))

    def tok(width, off):
        return pl.BlockSpec((tm, width), lambda i: (i, off // width))

    return pl.pallas_call(
        kern,
        grid=(nbat // nseq,),
        in_specs=[tok(c3, OFF_GQKV), tok(c, OFF_GA), tok(c, OFF_GB), tok(c, OFF_GZ), cst_spec, s_spec,
                  _const_spec((GDN_K, c3)), _const_spec((1, c)), _const_spec((1, c)), _const_spec((1, c))],
        out_specs=[pl.BlockSpec((tm, c), lambda i: (i, 0)), s_spec, cst_spec],
        out_shape=[jax.ShapeDtypeStruct((nbat * ts, c), F32),
                   jax.ShapeDtypeStruct((nbat, N_HEADS, HEAD_DIM, HEAD_DIM), F32),
                   jax.ShapeDtypeStruct((nbat, SUBLANE, c3), F32)],
        compiler_params=_cparams(1, 40),
        name="gdn_short",
    )(proj, proj, proj, proj, cst8, s0, w, alog_b, dtb_b, gn_b)


def _gdn_kernel(qkv_ref, ga_ref, gb_ref, z_ref, cst_ref, s0_ref, w_ref, alog_ref, dtb_ref, gn_ref,
                y_ref, sf_ref, csto_ref, s_scr, c_scr, qkv_s, g_ref, beta_ref, *, n_chunks):
    tc = qkv_ref.shape[0]
    c3 = qkv_ref.shape[1]
    c8 = _carry_in(pl.program_id(1), cst_ref, c_scr)
    u3 = qkv_ref[...].astype(F32).reshape(1, tc, c3)
    qkv_s[...] = _silu(_causal_conv(u3, c8, w_ref[...], GDN_K)).reshape(tc, c3)
    g_ref[...] = -jnp.exp(alog_ref[...]) * _softplus(ga_ref[...].astype(F32) + dtb_ref[...])
    beta_ref[...] = jax.nn.sigmoid(gb_ref[...].astype(F32))
    hist = u3[:, tc - SUBLANE:, :]
    c_scr[...] = hist
    csto_ref[...] = hist
    q_ref = qkv_s.at[:, 0:BRANCH_W]
    k_ref = qkv_s.at[:, BRANCH_W:2 * BRANCH_W]
    v_ref = qkv_s.at[:, 2 * BRANCH_W:3 * BRANCH_W]

    cc = GDN_CHUNK
    n2 = 2 * cc
    npair = N_HEADS // 2
    zero_blk = jnp.zeros((HEAD_DIM, HEAD_DIM), F32)

    @pl.when(pl.program_id(1) == 0)
    def _():
        for pr in range(npair):
            top = jnp.concatenate([s0_ref[0, 2 * pr], zero_blk], axis=-1)
            bot = jnp.concatenate([zero_blk, s0_ref[0, 2 * pr + 1]], axis=-1)
            s_scr[pr] = jnp.concatenate([top, bot], axis=0)

    rows = lax.broadcasted_iota(jnp.int32, (n2, n2), 0)
    cols = lax.broadcasted_iota(jnp.int32, (n2, n2), 1)
    same = (rows >> 6) == (cols >> 6)
    tril = same & (rows >= cols)
    strict = same & (rows > cols)
    masks = _inverse_masks(n2)[:7]
    gn = gn_ref[...]
    probs = [(ci, pr) for ci in range(n_chunks) for pr in range(npair)]

    def rs(ci):
        return slice(ci * cc, (ci + 1) * cc)

    def ls(pr):
        return slice(pr * LANE, (pr + 1) * LANE)

    def stack(x):
        return jnp.where(same, jnp.concatenate([x, x], axis=0), 0.0)

    def twice(x):
        return jnp.concatenate([x, x], axis=0)

    gc_all = [_cumsum_rows(g_ref[rs(ci), :], cc) for ci in range(n_chunks)]
    gcm = [stack(gc_all[ci][:, ls(pr)]) for ci, pr in probs]
    beta = [twice(beta_ref[rs(ci), ls(pr)]) for ci, pr in probs]
    sol, intra, qe, k = _gdn_solve([stack(q_ref[rs(ci), ls(pr)]) for ci, pr in probs],
                                   [stack(k_ref[rs(ci), ls(pr)]) for ci, pr in probs],
                                   [stack(v_ref[rs(ci), ls(pr)]) for ci, pr in probs],
                                   gcm, beta, tril, strict, masks)
    g_last = [gc_all[ci][cc - 1:cc, ls(pr)] for ci, pr in probs]
    kdec = [k[p] * jnp.exp(g_last[p] - gcm[p]) for p in range(len(probs))]
    sdec = [jnp.exp(g_last[p]) for p in range(len(probs))]

    for ci in range(n_chunks):
        ps = [ci * npair + pr for pr in range(npair)]
        s_old = [s_scr[pr] for pr in range(npair)]
        v_new = [sol[p][:, :n2] - _bdot(sol[p][:, n2:], s_old[pr]) for pr, p in enumerate(ps)]
        o_st = [_bdot(qe[p], s_old[pr]) for pr, p in enumerate(ps)]
        o_in = [_bdot(intra[p], v_new[pr]) for pr, p in enumerate(ps)]
        s_up = [_bdot_tn(kdec[p], v_new[pr]) for pr, p in enumerate(ps)]
        for pr, p in enumerate(ps):
            s_scr[pr] = s_old[pr] * sdec[p] + s_up[pr]
            o = o_st[pr] + o_in[pr]
            o = o * lax.rsqrt(jnp.sum(o * o, axis=-1, keepdims=True) * (1.0 / HEAD_DIM) + EPS)
            o = o[:cc] + o[cc:]
            y_ref[rs(ci), ls(pr)] = o * gn[:, ls(pr)] * _silu(z_ref[rs(ci), ls(pr)].astype(F32))

    @pl.when(pl.program_id(1) == pl.num_programs(1) - 1)
    def _():
        for pr in range(npair):
            s_pair = s_scr[pr]
            sf_ref[0, 2 * pr] = s_pair[:HEAD_DIM, :HEAD_DIM]
            sf_ref[0, 2 * pr + 1] = s_pair[HEAD_DIM:, HEAD_DIM:]


def _gdn(proj, cst8, s0, w, alog_b, dtb_b, gn_b, nbat, t, tc):
    tper = t // tc
    c = BRANCH_W
    c3 = 3 * c
    kern = functools.partial(_gdn_kernel, n_chunks=tc // GDN_CHUNK)
    s_spec = pl.BlockSpec((1, N_HEADS, HEAD_DIM, HEAD_DIM), lambda b, t_: (b, 0, 0, 0))
    cst_spec = pl.BlockSpec((1, SUBLANE, c3), lambda b, t_: (b, 0, 0))
    return pl.pallas_call(
        kern,
        grid=(nbat, tper),
        in_specs=[_tok_spec(tc, c3, OFF_GQKV // c3, tper), _tok_spec(tc, c, OFF_GA // c, tper),
                  _tok_spec(tc, c, OFF_GB // c, tper), _tok_spec(tc, c, OFF_GZ // c, tper),
                  cst_spec, s_spec, _const_spec((GDN_K, c3)), _const_spec((1, c)), _const_spec((1, c)),
                  _const_spec((1, c))],
        out_specs=[_tok_spec(tc, c, 0, tper), s_spec, cst_spec],
        out_shape=[jax.ShapeDtypeStruct((nbat * t, c), F32),
                   jax.ShapeDtypeStruct((nbat, N_HEADS, HEAD_DIM, HEAD_DIM), F32),
                   jax.ShapeDtypeStruct((nbat, SUBLANE, c3), F32)],
        scratch_shapes=[pltpu.VMEM((N_HEADS // 2, 2 * HEAD_DIM, 2 * HEAD_DIM), F32),
                        pltpu.VMEM((1, SUBLANE, c3), F32), pltpu.VMEM((tc, c3), F32),
                        pltpu.VMEM((tc, c), F32), pltpu.VMEM((tc, c), F32)],
        compiler_params=_cparams(2, 40),
        name="gdn_rule",
    )(proj, proj, proj, proj, cst8, s0, w, alog_b, dtb_b, gn_b)


def _merge_kernel(x_ref, mod_ref, g_ref, ya_ref, yb_ref, yc_ref, yd_ref, wg_ref, wbo_ref, wmix_ref, o_ref):
    x = x_ref[...]
    nb, tt, d = x.shape
    mod = mod_ref[...]
    h = _modulated_norm(x, g_ref[...], mod, 0, 1).reshape(nb * tt, d).astype(BF16)
    acc = None
    for n, y_ref in enumerate((ya_ref, yb_ref, yc_ref, yd_ref)):
        gate = jax.nn.sigmoid(jnp.dot(h, wg_ref[:, n * d:(n + 1) * d], preferred_element_type=F32))
        term = gate * _bdot(y_ref[...], wbo_ref[n])
        acc = term if acc is None else acc + term
    mix = _bdot(acc, wmix_ref[...])
    o_ref[...] = x + mod[:, 2:3, :] * mix.reshape(nb, tt, d)


def _merge(x3, mod, g, ys, wg, wbo, wmix, nb, tt):
    nbat, t, d = x3.shape
    tper = t // tt
    tm = nb * tt
    c = BRANCH_W
    x_spec = pl.BlockSpec((nb, tt, d), lambda b, t_: (b, t_, 0))
    return pl.pallas_call(
        _merge_kernel,
        grid=(nbat // nb, tper),
        in_specs=[x_spec, pl.BlockSpec((nb, 6, d), lambda b, t_: (b, 0, 0)), _const_spec((1, d))]
                 + [_tok_spec(tm, c, 0, tper)] * 4
                 + [_const_spec(wg.shape), _const_spec(wbo.shape), _const_spec(wmix.shape)],
        out_specs=x_spec,
        out_shape=jax.ShapeDtypeStruct(x3.shape, F32),
        compiler_params=_cparams(2, 48),
        name="merge",
    )(x3, mod, g, *ys, wg, wbo, wmix)


def _ffn_kernel(x_ref, mod_ref, g_ref, win_ref, wout_ref, gf_ref, o_ref, *, final):
    x = x_ref[...]
    nb, tt, d = x.shape
    mod = mod_ref[...]
    fh = wout_ref.shape[0]
    h = _modulated_norm(x, g_ref[...], mod, 3, 4).reshape(nb * tt, d).astype(BF16)
    gate = jnp.dot(h, win_ref[:, :fh], preferred_element_type=F32)
    up = jnp.dot(h, win_ref[:, fh:], preferred_element_type=F32)
    out = _bdot(_silu(gate) * up, wout_ref[...])
    xo = x + mod[:, 5:6, :] * out.reshape(nb, tt, d)
    o_ref[...] = _rms(xo, gf_ref[...]) if final else xo


def _ffn(x3, mod, g, win, wout, gf, nb, tt, final):
    nbat, t, d = x3.shape
    x_spec = pl.BlockSpec((nb, tt, d), lambda b, t_: (b, t_, 0))
    kern = functools.partial(_ffn_kernel, final=final)
    return pl.pallas_call(
        kern,
        grid=(nbat // nb, t // tt),
        in_specs=[x_spec, pl.BlockSpec((nb, 6, d), lambda b, t_: (b, 0, 0)), _const_spec((1, d)),
                  _const_spec(win.shape), _const_spec(wout.shape), _const_spec((1, d))],
        out_specs=x_spec,
        out_shape=jax.ShapeDtypeStruct(x3.shape, F32),
        compiler_params=_cparams(2, 56),
        name="ffn",
    )(x3, mod, g, win, wout, gf)


def _block_diag(blocks):
    h, r, c = blocks.shape
    eye = jnp.eye(h, dtype=blocks.dtype)
    return jnp.einsum('hrc,hg->hrgc', blocks, eye).reshape(h * r, h * c)


def _prep_layer_weights(w_in, w_qb, w_kvb, w_lru_gate_a, w_lru_gate_x):
    d = w_in.shape[0]
    c = BRANCH_W
    o = 0
    segs = {}
    for name, width in (('sc', 3 * c), ('qa', MLA_QLORA), ('ckv', MLA_KVLORA), ('kpe', MLA_ROPE), ('gqkv', 3 * c),
                        ('gz', c), ('ga', N_HEADS), ('gb', N_HEADS), ('lx', c), ('lg', c)):
        segs[name] = w_in[:, o:o + width]
        o += width
    half = MLA_ROPE // 2
    zpad = jnp.zeros((d, LANE - MLA_ROPE), F32)
    kpe = segs['kpe']
    kpa = jnp.concatenate([kpe, zpad], axis=1)
    kpb = jnp.concatenate([-kpe[:, half:], kpe[:, :half], zpad], axis=1)
    w_proj = jnp.concatenate(
        [segs['sc'], segs['gqkv'], segs['gz'], jnp.repeat(segs['ga'], HEAD_DIM, axis=1),
         jnp.repeat(segs['gb'], HEAD_DIM, axis=1), segs['lx'], segs['lg'], segs['qa'], segs['ckv'], kpa, kpb],
        axis=1).astype(BF16)

    ql = w_qb.shape[0]
    wq = w_qb.reshape(ql, N_HEADS, HEAD_DIM + MLA_ROPE)
    wn = wq[:, :, :HEAD_DIM].reshape(ql, c)
    pe = wq[:, :, HEAD_DIM:]
    zq = jnp.zeros((ql, N_HEADS, LANE - MLA_ROPE), F32)
    wpa = jnp.concatenate([pe, zq], axis=2).reshape(ql, N_HEADS * LANE)
    wpb = jnp.concatenate([-pe[:, :, half:], pe[:, :, :half], zq], axis=2).reshape(ql, N_HEADS * LANE)
    w_kb = w_kvb[:, :, :HEAD_DIM]
    w_vb = w_kvb[:, :, HEAD_DIM:]
    bdk = _block_diag(jnp.transpose(w_kb, (1, 2, 0)))
    bdv = _block_diag(jnp.transpose(w_vb, (1, 0, 2)))
    wlg = jnp.concatenate([_block_diag(w_lru_gate_a), _block_diag(w_lru_gate_x)], axis=1)
    wvt = jnp.transpose(w_vb, (1, 2, 0))
    return dict(w_proj=w_proj, wn=wn.astype(BF16), wpa=wpa.astype(BF16), wpb=wpb.astype(BF16),
                bdk=bdk.astype(BF16), bdv=bdv.astype(BF16), wvt=wvt.astype(BF16), wlg=wlg.astype(BF16))


def _rope_tables(pos):
    half = MLA_ROPE // 2
    inv = ROPE_THETA ** (-jnp.arange(half, dtype=F32) / half)
    ang = pos.astype(F32)[:, None] * inv[None, :]
    pad = jnp.zeros((pos.shape[0], LANE - MLA_ROPE), F32)
    cos = jnp.concatenate([jnp.cos(ang), jnp.cos(ang), pad], axis=1)
    sin = jnp.concatenate([jnp.sin(ang), jnp.sin(ang), pad], axis=1)
    return cos, sin


def _pad_state(st, k_w):
    return jnp.pad(st, ((0, 0), (SUBLANE - (k_w - 1), 0), (0, 0)))


def _tile(t, pref):
    tt = min(t, pref)
    while t % tt:
        tt //= 2
    return tt


def _group_layer(x3, mod, lw, p, st, rope, attend, cfg):
    nbat, t, d = x3.shape
    nb, tt = cfg['nb'], cfg['tt']
    proj = _inproj(x3, mod, p['g_norm_mix'], lw['w_proj'], cfg['nb_proj'], cfg['tt_proj'])
    y_a, sc8 = _sconv(proj, st['sconv'], p['w_sc_conv'], nbat, t, nb, tt)
    q_full, ckv, kpe, kf, *vt = _mla_pre(proj, rope[0], rope[1], p['g_q_norm'], p['g_kv_norm'], lw['wn'], lw['wpa'],
                                         lw['wpb'], lw['bdk'], nbat, t, nb, tt, cfg['q_dtype'], cfg['emit_vt'])
    y_b = attend(q_full, ckv, kpe, kf, vt, lw)
    gdn_w = (p['w_gdn_conv'], p['alog_b'], p['dtb_b'], p['gn_b'])
    if t < GDN_CHUNK:
        y_c, s_gdn, gc8 = _gdn_short(proj, st['gdn_conv'], st['gdn'], *gdn_w, nbat, t)
    else:
        y_c, s_gdn, gc8 = _gdn(proj, st['gdn_conv'], st['gdn'], *gdn_w, nbat, t, cfg['tc'])
    y_d, lc8, h_last = _lru(proj, st['lru_conv'], st['lru'], p['w_lru_conv'], p['b_lru_conv'], lw['wlg'],
                            p['b_lru_gates'], p['nsl'], nbat, t, cfg['nb_scan'], cfg['tt_scan'], cfg['pos0'])
    x1 = _merge(x3, mod, p['g_norm_mix'], (y_a, y_b, y_c, y_d), p['w_merge_gate'], p['w_branch_out'],
                p['w_mix_out'], cfg['nb_mm'], cfg['tt_mm'])
    x2 = _ffn(x1, mod, p['g_norm_ffn'], p['w_ffn_in'], p['w_ffn_out'], p['g_final'], cfg['nb_mm'], cfg['tt_mm'],
              cfg['final'])
    new_st = dict(ckv=ckv.reshape(nbat, t, -1), kpe=kpe.reshape(nbat, t, -1),
                  sconv=sc8[:, SUBLANE - (SC_K - 1):], gdn_conv=gc8[:, SUBLANE - (GDN_K - 1):], gdn=s_gdn,
                  lru_conv=lc8[:, SUBLANE - (LRU_K - 1):], lru=h_last[:, 0, :])
    return x2, new_st


STATE_KEYS = ('ckv', 'kpe', 'sconv', 'gdn_conv', 'gdn', 'lru_conv', 'lru')


def kernel(x_prompt, x_sample, c_prompt, c_sample, cache_mla_ckv, cache_mla_kpe, page_table, state_sconv, state_gdn_conv, state_gdn, state_lru_conv, state_lru, w_ada, b_ada, g_norm_mix, g_norm_ffn, w_in, w_sc_conv, g_q_norm, w_qb, g_kv_norm, w_kvb, w_gdn_conv, gdn_a_log, gdn_dt_bias, g_gdn_norm, w_lru_conv, b_lru_conv, w_lru_gate_a, b_lru_gate_a, w_lru_gate_x, b_lru_gate_x, lru_lambda, w_branch_out, w_merge_gate, w_mix_out, w_ffn_in, w_ffn_out, g_final):
    bp, tp, d = x_prompt.shape
    bs, ts, _ = x_sample.shape
    depth = w_in.shape[0]
    n_pages = page_table.shape[1]
    past_len = n_pages * cache_mla_ckv.shape[2]
    c = BRANCH_W
    assert ts == SUBLANE and tp % GDN_CHUNK == 0 and d % LANE == 0

    mod_all = _ada(jnp.concatenate([c_prompt, c_sample], axis=0), w_ada, b_ada).reshape(depth, bp + bs, 6, d)

    nb_s = _tile(bs, 32)
    cos_p, sin_p = _rope_tables(jnp.arange(tp, dtype=jnp.int32))
    cos_s, sin_s = _rope_tables(past_len + jnp.arange(ts, dtype=jnp.int32))
    rope_p = (cos_p, sin_p)
    rope_s = (jnp.tile(cos_s, (nb_s, 1)), jnp.tile(sin_s, (nb_s, 1)))

    tq = _tile(tp, 256)
    tk = _tile(tp, 256)
    cfg_p = dict(nb=1, tt=tk, nb_proj=1, tt_proj=_tile(tp, 1024), q_dtype=BF16, emit_vt=True,
                 tc=_tile(tp, 256), pos0=0, nb_scan=1, tt_scan=_tile(tp, 256), nb_mm=1, tt_mm=_tile(tp, 512))
    cfg_s = dict(nb=nb_s, tt=ts, nb_proj=_tile(bs, 128), tt_proj=ts, q_dtype=F32, emit_vt=False,
                 tc=GDN_CHUNK, pos0=past_len, nb_scan=nb_s, tt_scan=ts, nb_mm=_tile(bs, 64), tt_mm=ts)
    cache_kpe_t = jnp.swapaxes(cache_mla_kpe, 2, 3)

    xp, xs = x_prompt, x_sample
    out_p = {k: [] for k in STATE_KEYS}
    out_s = {k: [] for k in STATE_KEYS}
    for l in range(depth):
        lw = _prep_layer_weights(w_in[l], w_qb[l], w_kvb[l], w_lru_gate_a[l], w_lru_gate_x[l])
        p = dict(
            g_norm_mix=g_norm_mix[l][None], g_norm_ffn=g_norm_ffn[l][None], g_final=g_final[None],
            w_sc_conv=w_sc_conv[l], g_q_norm=g_q_norm[l][None], g_kv_norm=g_kv_norm[l][None],
            w_gdn_conv=w_gdn_conv[l],
            alog_b=jnp.repeat(gdn_a_log[l], HEAD_DIM)[None], dtb_b=jnp.repeat(gdn_dt_bias[l], HEAD_DIM)[None],
            gn_b=jnp.tile(g_gdn_norm[l], N_HEADS)[None],
            w_lru_conv=w_lru_conv[l], b_lru_conv=b_lru_conv[l][None],
            b_lru_gates=jnp.concatenate([b_lru_gate_a[l], b_lru_gate_x[l]])[None],
            nsl=lru_lambda[l][None],
            w_merge_gate=w_merge_gate[l].astype(BF16), w_branch_out=w_branch_out[l].astype(BF16),
            w_mix_out=w_mix_out[l].astype(BF16), w_ffn_in=w_ffn_in[l].astype(BF16),
            w_ffn_out=w_ffn_out[l].astype(BF16))
        final = l == depth - 1

        st_p = dict(sconv=jnp.zeros((bp, SUBLANE, c), F32), gdn_conv=jnp.zeros((bp, SUBLANE, 3 * c), F32),
                    gdn=jnp.zeros((bp, N_HEADS, HEAD_DIM, HEAD_DIM), F32),
                    lru_conv=jnp.zeros((bp, SUBLANE, c), F32), lru=jnp.zeros((bp, 1, c), F32))

        def attend_p(q_full, ckv, kpe, kf, vt, lw_):
            return _attn_prompt(q_full, kf, vt[0], lw_['wvt'], bp, tp, tq, tk)

        xp, nst_p = _group_layer(xp, mod_all[l, :bp], lw, p, st_p, rope_p, attend_p, dict(cfg_p, final=final))

        st_s = dict(sconv=_pad_state(state_sconv[l], SC_K), gdn_conv=_pad_state(state_gdn_conv[l], GDN_K),
                    gdn=state_gdn[l], lru_conv=_pad_state(state_lru_conv[l], LRU_K), lru=state_lru[l][:, None, :])

        def attend_s(q_full, ckv, kpe, kf, vt, lw_, layer=l):
            return _attn_sample(q_full, cache_mla_ckv, cache_kpe_t, page_table, layer, ckv, kpe, lw_['bdv'], bs, ts)

        xs, nst_s = _group_layer(xs, mod_all[l, bp:], lw, p, st_s, rope_s, attend_s, dict(cfg_s, final=final))
        for k in STATE_KEYS:
            out_p[k].append(nst_p[k])
            out_s[k].append(nst_s[k])

    return ((xp, xs) + tuple(jnp.stack(out_p[k]) for k in STATE_KEYS)
            + tuple(jnp.stack(out_s[k]) for k in STATE_KEYS))
```

```python
import functools
import math

import jax
import jax.numpy as jnp
from jax import lax
from jax.experimental import pallas as pl
from jax.experimental.pallas import tpu as pltpu

F32 = jnp.float32
BF16 = jnp.bfloat16

HEAD_DIM = 64
N_HEADS = 8
BRANCH_W = N_HEADS * HEAD_DIM
MLA_ROPE = HEAD_DIM // 2
MLA_QLORA = 256
MLA_KVLORA = 128
ROPE_THETA = 10000.0
SC_K = 3
GDN_K = 4
LRU_K = 4
LRU_C = 8.0
GDN_CHUNK = 64
EPS = 1e-6
NEG_BIG = -1e30
LANE = 128
SUBLANE = 8
QK_W = 2 * LANE
ATTN_GROUP_W = 256

OFF_SC = 0
OFF_GQKV = 1536
OFF_GZ = 3072
OFF_LX = 3584
OFF_LG = 4096
OFF_QA = 4608
OFF_CKV = 4864
OFF_KPA = 4992
OFF_KPB = 5120
OFF_GAB = 5248
PROJ_W = 5376
PROJ_TN = 896


def _cparams(n_axes, vmem_mib):
    return pltpu.CompilerParams(dimension_semantics=("arbitrary",) * n_axes,
                                vmem_limit_bytes=vmem_mib * 1024 * 1024)


def _const_spec(shape):
    nd = len(shape)
    return pl.BlockSpec(shape, lambda *_: (0,) * nd, pipeline_mode=pl.Buffered(1))


def _bdot(a, b):
    return jnp.dot(a.astype(BF16), b.astype(BF16), preferred_element_type=F32)


def _bdot_nt(a, b):
    return lax.dot_general(a.astype(BF16), b.astype(BF16), (((1,), (1,)), ((), ())),
                           preferred_element_type=F32)


def _bdot_tn(a, b):
    return lax.dot_general(a.astype(BF16), b.astype(BF16), (((0,), (0,)), ((), ())),
                           preferred_element_type=F32)


def _silu(x):
    return x * jax.nn.sigmoid(x)


def _softplus(x):
    return jnp.maximum(x, 0.0) + jnp.log(1.0 + jnp.exp(-jnp.abs(x)))


def _gelu_tanh(x):
    return 0.5 * x * (1.0 + jnp.tanh(math.sqrt(2.0 / math.pi) * (x + 0.044715 * (x * x * x))))


def _rms(x, g):
    return x * lax.rsqrt(jnp.mean(x * x, axis=-1, keepdims=True) + EPS) * g


def _modulated_norm(x3, g, mod, shift_row, scale_row):
    y = _rms(x3, g)
    return y * (1.0 + mod[:, scale_row:scale_row + 1, :]) + mod[:, shift_row:shift_row + 1, :]


def _ada_kernel(c_ref, w_ref, b_ref, o_ref):
    o_ref[0] = _bdot(_silu(c_ref[...]), w_ref[0]) + b_ref[0]


def _ada(c_all, w_ada, b_ada):
    depth, d, n = w_ada.shape
    nb = c_all.shape[0]
    tn = n // 4
    return pl.pallas_call(
        _ada_kernel,
        grid=(depth, n // tn),
        in_specs=[pl.BlockSpec((nb, d), lambda l, j: (0, 0)),
                  pl.BlockSpec((1, d, tn), lambda l, j: (l, 0, j)),
                  pl.BlockSpec((1, 1, tn), lambda l, j: (l, 0, j))],
        out_specs=pl.BlockSpec((1, nb, tn), lambda l, j: (l, 0, j)),
        out_shape=jax.ShapeDtypeStruct((depth, nb, n), F32),
        compiler_params=_cparams(2, 40),
        name="ada_mod",
    )(c_all, w_ada, b_ada.reshape(depth, 1, n))


def _inproj_kernel(x_ref, mod_ref, g_ref, w_ref, o_ref, h_scr):
    @pl.when(pl.program_id(1) == 0)
    def _():
        h = _modulated_norm(x_ref[...], g_ref[...], mod_ref[...], 0, 1)
        h_scr[...] = h.reshape(h_scr.shape).astype(BF16)

    o_ref[...] = jnp.dot(h_scr[...], w_ref[...], preferred_element_type=F32).astype(o_ref.dtype)


def _inproj(x3, mod, g, w, nb, tt):
    nbat, t, d = x3.shape
    tper = t // tt
    tm = nb * tt
    n_m = (nbat // nb) * tper
    return pl.pallas_call(
        _inproj_kernel,
        grid=(n_m, PROJ_W // PROJ_TN),
        in_specs=[pl.BlockSpec((nb, tt, d), lambda i, j: (i // tper, i % tper, 0)),
                  pl.BlockSpec((nb, 6, d), lambda i, j: (i // tper, 0, 0)),
                  _const_spec((1, d)),
                  pl.BlockSpec((d, PROJ_TN), lambda i, j: (0, j))],
        out_specs=pl.BlockSpec((tm, PROJ_TN), lambda i, j: (i, j)),
        out_shape=jax.ShapeDtypeStruct((nbat * t, PROJ_W), BF16),
        scratch_shapes=[pltpu.VMEM((tm, d), BF16)],
        compiler_params=_cparams(2, 40),
        name="in_proj",
    )(x3, mod, g, w)


def _causal_conv(u3, c8, w, k_w):
    nb, tt, c = u3.shape
    rows = lax.broadcasted_iota(jnp.int32, (nb, SUBLANE, c), 1)
    acc = None
    for j in range(k_w):
        s = k_w - 1 - j
        if s == 0:
            sh = u3
        else:
            full = pltpu.roll(u3, s, axis=1)
            top = jnp.where(rows < s, pltpu.roll(c8, s, axis=1), full[:, 0:SUBLANE, :])
            sh = top if tt == SUBLANE else jnp.concatenate([top, full[:, SUBLANE:, :]], axis=1)
        term = sh * w[j:j + 1, :]
        acc = term if acc is None else acc + term
    return acc


def _carry_in(t_idx, st_ref, c_scr):
    @pl.when(t_idx == 0)
    def _():
        c_scr[...] = st_ref[...]

    return c_scr[...]


def _tok_spec(tm, width, col_block, tper):
    return pl.BlockSpec((tm, width), lambda b, t: (b * tper + t, col_block))


def _state_spec(nb, c):
    return pl.BlockSpec((nb, SUBLANE, c), lambda b, t: (b, 0, 0))


def _lru_kernel(lx_ref, lg_ref, st_ref, h0_ref, w_ref, cb_ref, wg_ref, bg_ref, lam_ref,
                y_ref, sto_ref, hl_ref, c_scr, h_scr, *, nb, tt, pos0):
    c = lx_ref.shape[-1]
    t_idx = pl.program_id(1)
    c8 = _carry_in(t_idx, st_ref, c_scr)

    @pl.when(t_idx == 0)
    def _():
        h_scr[...] = h0_ref[...]

    u3 = lx_ref[...].astype(F32).reshape(nb, tt, c)
    u = (_causal_conv(u3, c8, w_ref[...], LRU_K) + cb_ref[...]).reshape(nb * tt, c)
    last = u3[:, tt - SUBLANE:, :]
    c_scr[...] = last
    sto_ref[...] = last
    gates = jax.nn.sigmoid(_bdot(u, wg_ref[...]) + bg_ref[...])
    r = gates[:, :c]
    ig = gates[:, c:]
    log_a = -LRU_C * r * _softplus(-lam_ref[...])
    rows = lax.broadcasted_iota(jnp.int32, (nb, tt, c), 1)
    mult = jnp.sqrt(1.0 - jnp.exp(2.0 * log_a)).reshape(nb, tt, c)
    mult = jnp.where(pos0 + t_idx * tt + rows == 0, 1.0, mult)
    grp = tt // SUBLANE
    a = jnp.exp(log_a).reshape(nb * grp, SUBLANE, c)
    b = (mult * (ig * u).reshape(nb, tt, c)).reshape(nb * grp, SUBLANE, c)
    tok = lax.broadcasted_iota(jnp.int32, (nb * grp, SUBLANE, c), 1)
    d = 1
    while d < SUBLANE:
        keep = tok >= d
        a_s = jnp.where(keep, pltpu.roll(a, d, axis=1), 1.0)
        b_s = jnp.where(keep, pltpu.roll(b, d, axis=1), 0.0)
        b = a * b_s + b
        a = a * a_s
        d *= 2
    a = a.reshape(nb, grp, SUBLANE, c)
    b = b.reshape(nb, grp, SUBLANE, c)
    h_in = h_scr[...]
    parts = []
    for g in range(grp):
        part = b[:, g] + a[:, g] * h_in
        h_in = part[:, SUBLANE - 1:SUBLANE, :]
        parts.append(part)
    hs = parts[0] if grp == 1 else jnp.concatenate(parts, axis=1)
    h_last = h_in
    h_scr[...] = h_last
    hl_ref[...] = h_last
    y_ref[...] = hs.reshape(nb * tt, c) * _gelu_tanh(lg_ref[...].astype(F32))


def _lru(proj, st8, h0, w, cb, wg, bg, lam, nbat, t, nb, tt, pos0):
    tper = t // tt
    tm = nb * tt
    c = BRANCH_W
    kern = functools.partial(_lru_kernel, nb=nb, tt=tt, pos0=pos0)
    h_spec = pl.BlockSpec((nb, 1, c), lambda b, t_: (b, 0, 0))
    return pl.pallas_call(
        kern,
        grid=(nbat // nb, tper),
        in_specs=[_tok_spec(tm, c, OFF_LX // c, tper), _tok_spec(tm, c, OFF_LG // c, tper), _state_spec(nb, c),
                  h_spec, _const_spec((LRU_K, c)), _const_spec((1, c)), _const_spec((c, 2 * c)),
                  _const_spec((1, 2 * c)), _const_spec((1, c))],
        out_specs=[_tok_spec(tm, c, 0, tper), _state_spec(nb, c), h_spec],
        out_shape=[jax.ShapeDtypeStruct((nbat * t, c), F32), jax.ShapeDtypeStruct((nbat, SUBLANE, c), F32),
                   jax.ShapeDtypeStruct((nbat, 1, c), F32)],
        scratch_shapes=[pltpu.VMEM((nb, SUBLANE, c), F32), pltpu.VMEM((nb, 1, c), F32)],
        compiler_params=_cparams(2, 40),
        name="lru",
    )(proj, proj, st8, h0, w, cb, wg, bg, lam)


def _mla_pre_kernel(qa_ref, ckv_ref, kpa_ref, kpb_ref, cos_ref, sin_ref, gq_ref, gkv_ref,
                    wn_ref, wpa_ref, wpb_ref, bdk_ref, q_out, ckv_out, kpe_out, kf_out, *rest, nb, tt):
    cos = cos_ref[...]
    sin = sin_ref[...]
    cq = _rms(qa_ref[...].astype(F32), gq_ref[...]).astype(BF16)
    qn = jnp.dot(cq, wn_ref[...], preferred_element_type=F32)
    qabs = _bdot(qn, bdk_ref[...])
    pa = jnp.dot(cq, wpa_ref[...], preferred_element_type=F32)
    pb = jnp.dot(cq, wpb_ref[...], preferred_element_type=F32)
    for h in range(N_HEADS):
        sl = slice(h * LANE, (h + 1) * LANE)
        q_pe = pa[:, sl] * cos + pb[:, sl] * sin
        q_out[:, h, :, 0:LANE] = qabs[:, sl].reshape(nb, tt, LANE).astype(q_out.dtype)
        q_out[:, h, :, LANE:QK_W] = q_pe.reshape(nb, tt, LANE).astype(q_out.dtype)
    ckv = _rms(ckv_ref[...].astype(F32), gkv_ref[...])
    kpe = kpa_ref[...].astype(F32) * cos + kpb_ref[...].astype(F32) * sin
    ckv_out[...] = ckv
    kpe_out[...] = kpe[:, :MLA_ROPE]
    kf_out[...] = jnp.concatenate([ckv, kpe], axis=-1).astype(BF16)
    if rest:
        rest[0][0] = ckv.T.astype(BF16)


def _mla_pre(proj, cos, sin, gq, gkv, wn, wpa, wpb, bdk, nbat, t, nb, tt, q_dtype, emit_vt):
    tper = t // tt
    tm = nb * tt
    hw = N_HEADS * LANE
    kern = functools.partial(_mla_pre_kernel, nb=nb, tt=tt)
    tab_spec = pl.BlockSpec((tm, LANE), lambda b, t_: (t_, 0))
    vt_specs, vt_shapes = [], []
    if emit_vt:
        assert nb == 1
        vt_specs = [pl.BlockSpec((1, MLA_KVLORA, tt), lambda b, t_: (b * tper + t_, 0, 0))]
        vt_shapes = [jax.ShapeDtypeStruct((nbat * tper, MLA_KVLORA, tt), BF16)]
    return pl.pallas_call(
        kern,
        grid=(nbat // nb, tper),
        in_specs=[_tok_spec(tm, MLA_QLORA, OFF_QA // MLA_QLORA, tper), _tok_spec(tm, LANE, OFF_CKV // LANE, tper),
                  _tok_spec(tm, LANE, OFF_KPA // LANE, tper), _tok_spec(tm, LANE, OFF_KPB // LANE, tper),
                  tab_spec, tab_spec, _const_spec((1, MLA_QLORA)), _const_spec((1, MLA_KVLORA)),
                  _const_spec((MLA_QLORA, BRANCH_W)), _const_spec((MLA_QLORA, hw)), _const_spec((MLA_QLORA, hw)),
                  _const_spec((BRANCH_W, hw))],
        out_specs=[pl.BlockSpec((nb, N_HEADS, tt, QK_W), lambda b, t_: (b, 0, t_, 0)),
                   _tok_spec(tm, MLA_KVLORA, 0, tper), _tok_spec(tm, MLA_ROPE, 0, tper), _tok_spec(tm, QK_W, 0, tper)]
                  + vt_specs,
        out_shape=[jax.ShapeDtypeStruct((nbat, N_HEADS, t, QK_W), q_dtype),
                   jax.ShapeDtypeStruct((nbat * t, MLA_KVLORA), F32),
                   jax.ShapeDtypeStruct((nbat * t, MLA_ROPE), F32),
                   jax.ShapeDtypeStruct((nbat * t, QK_W), BF16)] + vt_shapes,
        compiler_params=_cparams(2, 48),
        name="mla_pre",
    )(proj, proj, proj, proj, cos, sin, gq, gkv, wn, wpa, wpb, bdk)


def _heads_to_lanes(o, rows_per_head):
    return jnp.concatenate([o[h * rows_per_head:(h + 1) * rows_per_head] for h in range(N_HEADS)], axis=-1)


def _attn_prompt_kernel(q_ref, kf_ref, vt_ref, wvt_ref, y_ref, m_scr, l_scr, acc_scr, s_scr, *, tq, tk, scale):
    i = pl.program_id(1)
    n_full = (i * tq) // tk
    hpg = ATTN_GROUP_W // tq
    w2 = hpg * tq
    krow = lax.broadcasted_iota(jnp.int32, (tk, w2), 0)
    qcol = i * tq + (lax.broadcasted_iota(jnp.int32, (tk, w2), 1) & (tq - 1))
    heads = range(N_HEADS // hpg)
    m_scr[...] = jnp.full(m_scr.shape, NEG_BIG, F32)
    l_scr[...] = jnp.zeros(l_scr.shape, F32)
    acc_scr[...] = jnp.zeros(acc_scr.shape, F32)

    def scores(j):
        kblk = kf_ref[pl.ds(pl.multiple_of(j * tk, tk), tk), :]
        return [lax.dot_general(kblk, q_ref[0, hpg * h:hpg * (h + 1)].reshape(w2, QK_W), (((1,), (1,)), ((), ())),
                                preferred_element_type=F32)
                for h in heads]

    c2 = scale * math.log2(math.e)

    def step(j, masked):
        vt = vt_ref[j]
        s = [s_scr[h] for h in heads]
        if masked:
            keep = j * tk + krow <= qcol
            s = [jnp.where(keep, s[h], NEG_BIG) for h in heads]
        else:
            s_next = scores(j + 1)
            for h in heads:
                s_scr[h] = s_next[h]
        m_old = [m_scr[h] for h in heads]
        m_new = [jnp.maximum(m_old[h], jnp.max(s[h], axis=0, keepdims=True)) for h in heads]
        p = [jnp.exp2((s[h] - m_new[h]) * c2) for h in heads]
        alpha = [jnp.exp2((m_old[h] - m_new[h]) * c2) for h in heads]
        pv = [jnp.dot(vt, p[h].astype(BF16), preferred_element_type=F32) for h in heads]
        for h in heads:
            m_scr[h] = m_new[h]
            l_scr[h] = alpha[h] * l_scr[h] + jnp.sum(p[h], axis=0, keepdims=True)
            acc_scr[h] = alpha[h] * acc_scr[h] + pv[h]

    def body(j, carry):
        step(j, False)
        return carry

    s_first = scores(0)
    for h in heads:
        s_scr[h] = s_first[h]
    lax.fori_loop(0, n_full, body, 0)
    step(n_full, True)
    o = [acc_scr[h] / l_scr[h] for h in heads]
    outs = [_bdot(wvt_ref[hh], o[hh // hpg][:, (hh % hpg) * tq:(hh % hpg + 1) * tq]) for hh in range(N_HEADS)]
    y_ref[...] = jnp.concatenate(outs, axis=0).T


def _attn_prompt(q_full, kf, vt, wvt, nbat, t, tq, tk):
    tper = t // tq
    kper = t // tk
    assert ATTN_GROUP_W % tq == 0
    ngrp = N_HEADS * tq // ATTN_GROUP_W
    scale = (HEAD_DIM + MLA_ROPE) ** -0.5
    kern = functools.partial(_attn_prompt_kernel, tq=tq, tk=tk, scale=scale)
    return pl.pallas_call(
        kern,
        grid=(nbat, tper),
        in_specs=[pl.BlockSpec((1, N_HEADS, tq, QK_W), lambda b, i: (b, 0, i, 0)),
                  pl.BlockSpec((t, QK_W), lambda b, i: (b, 0)),
                  pl.BlockSpec((kper, MLA_KVLORA, tk), lambda b, i: (b, 0, 0)),
                  _const_spec(wvt.shape)],
        out_specs=pl.BlockSpec((tq, BRANCH_W), lambda b, i: (b * tper + i, 0)),
        out_shape=jax.ShapeDtypeStruct((nbat * t, BRANCH_W), F32),
        scratch_shapes=[pltpu.VMEM((ngrp, 1, ATTN_GROUP_W), F32), pltpu.VMEM((ngrp, 1, ATTN_GROUP_W), F32),
                        pltpu.VMEM((ngrp, MLA_KVLORA, ATTN_GROUP_W), F32),
                        pltpu.VMEM((ngrp, tk, ATTN_GROUP_W), F32)],
        compiler_params=_cparams(2, 40),
        name="attn_prompt",
    )(q_full, kf, vt, wvt)


def _attn_sample_kernel(pt_ref, q_ref, *refs, n_pages, pg, ts, page, scale):
    ckv_pages = refs[:n_pages]
    kpe_pages = refs[n_pages:2 * n_pages]
    ckvn_ref, kpen_ref, bdv_ref, y_ref = refs[2 * n_pages:]
    r = N_HEADS * ts
    groups = range(n_pages // pg)

    q = q_ref[0].reshape(r, QK_W)
    q_lat = q[:, :MLA_KVLORA].astype(BF16)
    q_pe = q[:, MLA_KVLORA:MLA_KVLORA + MLA_ROPE].astype(BF16)

    c_all = [jnp.concatenate([ckv_pages[g * pg + i][0, 0].astype(BF16) for i in range(pg)], axis=0)
             for g in groups]
    kpt_all = [jnp.concatenate([kpe_pages[g * pg + i][0, 0].astype(BF16) for i in range(pg)], axis=1)
               for g in groups]
    s = [(lax.dot_general(q_lat, c_all[g], (((1,), (1,)), ((), ())), preferred_element_type=F32)
          + jnp.dot(q_pe, kpt_all[g], preferred_element_type=F32)) * scale for g in groups]
    pad = jnp.zeros((page - ts, MLA_KVLORA), F32)
    cn = jnp.concatenate([ckvn_ref[...], pad], axis=0).astype(BF16)
    kn = jnp.concatenate([kpen_ref[...], pad[:, :MLA_ROPE]], axis=0).astype(BF16)
    sn = (lax.dot_general(q_lat, cn, (((1,), (1,)), ((), ())), preferred_element_type=F32)
          + lax.dot_general(q_pe, kn, (((1,), (1,)), ((), ())), preferred_element_type=F32)) * scale
    qpos = lax.broadcasted_iota(jnp.int32, (r, page), 0) & (ts - 1)
    kpos = lax.broadcasted_iota(jnp.int32, (r, page), 1)
    s.append(jnp.where(kpos <= qpos, sn, NEG_BIG))
    vals = c_all + [cn]
    parts = range(len(s))
    m = [jnp.max(s[g], axis=-1, keepdims=True) for g in parts]
    p = [jnp.exp(s[g] - m[g]) for g in parts]
    l = [jnp.sum(p[g], axis=-1, keepdims=True) for g in parts]
    acc = [jnp.dot(p[g].astype(BF16), vals[g], preferred_element_type=F32) for g in parts]
    m_all = m[0]
    for g in parts[1:]:
        m_all = jnp.maximum(m_all, m[g])
    w = [jnp.exp(m[g] - m_all) for g in parts]
    l_all = w[0] * l[0]
    acc_all = w[0] * acc[0]
    for g in parts[1:]:
        l_all = l_all + w[g] * l[g]
        acc_all = acc_all + w[g] * acc[g]
    o = _heads_to_lanes(acc_all / l_all, ts)
    y_ref[...] = _bdot(o, bdv_ref[...])


def _attn_sample(q_full, cache_ckv, cache_kpe, page_table, layer, ckv_new, kpe_new, bdv, nbat, ts):
    n_pages = page_table.shape[1]
    page = cache_ckv.shape[2]
    pg = math.gcd(n_pages, 16)
    scale = (HEAD_DIM + MLA_ROPE) ** -0.5
    kern = functools.partial(_attn_sample_kernel, n_pages=n_pages, pg=pg, ts=ts, page=page, scale=scale)

    def page_spec(i, rows, width):
        return pl.BlockSpec((1, 1, rows, width), lambda b, pt: (layer, pt[b * n_pages + i], 0, 0))

    in_specs = ([pl.BlockSpec((1, N_HEADS, ts, QK_W), lambda b, pt: (b, 0, 0, 0))]
                + [page_spec(i, page, MLA_KVLORA) for i in range(n_pages)]
                + [page_spec(i, MLA_ROPE, page) for i in range(n_pages)]
                + [pl.BlockSpec((ts, MLA_KVLORA), lambda b, pt: (b, 0)),
                   pl.BlockSpec((ts, MLA_ROPE), lambda b, pt: (b, 0)),
                   pl.BlockSpec((N_HEADS * LANE, BRANCH_W), lambda b, pt: (0, 0))])
    grid_spec = pltpu.PrefetchScalarGridSpec(
        num_scalar_prefetch=1,
        grid=(nbat,),
        in_specs=in_specs,
        out_specs=pl.BlockSpec((ts, BRANCH_W), lambda b, pt: (b, 0)),
    )
    return pl.pallas_call(
        kern,
        grid_spec=grid_spec,
        out_shape=jax.ShapeDtypeStruct((nbat * ts, BRANCH_W), F32),
        compiler_params=_cparams(1, 40),
        name="attn_sample",
    )(page_table.reshape(-1), q_full, *([cache_ckv] * n_pages), *([cache_kpe] * n_pages), ckv_new, kpe_new, bdv)


def _cumsum_rows(x, n):
    rows = lax.broadcasted_iota(jnp.int32, x.shape, 0)
    d = 1
    while d < n:
        x = x + jnp.where(rows >= d, pltpu.roll(x, d, axis=0), 0.0)
        d *= 2
    return x


def _inverse_masks(n):
    rows = lax.broadcasted_iota(jnp.int32, (n, n), 0)
    cols = lax.broadcasted_iota(jnp.int32, (n, n), 1)
    masks = [rows == cols, (rows >> 1) == (cols >> 1)]
    k = 1
    while (1 << k) < n:
        masks.append(((rows >> k) ^ (cols >> k)) == 1)
        k += 1
    return masks


def _gdn_solve(q_raw, k_raw, v_raw, gcm, beta, tril, strict, masks):
    n = range(len(q_raw))
    eye = jnp.where(masks[0], 1.0, 0.0)
    q = [x * lax.rsqrt(jnp.sum(x * x, axis=-1, keepdims=True) + EPS) * (HEAD_DIM ** -0.5) for x in q_raw]
    k = [x * lax.rsqrt(jnp.sum(x * x, axis=-1, keepdims=True) + EPS) for x in k_raw]
    kb = [k[p] * beta[p] for p in n]
    decay = []
    for p in n:
        diff = gcm[p] - gcm[p].T
        decay.append(jnp.where(tril, jnp.exp(jnp.where(tril, diff, 0.0)), 0.0))
    kk = [_bdot_nt(kb[p], k[p]) for p in n]
    a_mat = [jnp.where(strict, kk[p] * decay[p], 0.0) for p in n]
    t = [eye - jnp.where(masks[1], a_mat[p], 0.0) for p in n]
    for m in masks[2:]:
        tl = [_bdot(t[p], jnp.where(m, a_mat[p], 0.0)) for p in n]
        t = [t[p] - _bdot(tl[p], t[p]) for p in n]
    egc = [jnp.exp(gcm[p]) for p in n]
    sol = [_bdot(t[p], jnp.concatenate([v_raw[p] * beta[p], kb[p] * egc[p]], axis=-1)) for p in n]
    qk = [_bdot_nt(q[p], k[p]) for p in n]
    intra = [jnp.where(tril, qk[p] * decay[p], 0.0) for p in n]
    qe = [q[p] * egc[p] for p in n]
    return sol, intra, qe, k


def _head_inputs(gab_ref, exp_ref):
    ab = jnp.dot(gab_ref[...], exp_ref[...], preferred_element_type=F32)
    return ab[:, :BRANCH_W], ab[:, BRANCH_W:]


def _gdn_short_kernel(qkv_ref, gab_ref, z_ref, cst_ref, s0_ref, w_ref, alog_ref, dtb_ref, gn_ref, exp_ref,
                      y_ref, sf_ref, csto_ref, *, nseq, ts):
    cc = nseq * ts
    n2 = 2 * cc
    npair = N_HEADS // 2
    c3 = qkv_ref.shape[1]
    c = BRANCH_W
    u3 = qkv_ref[...].astype(F32).reshape(nseq, ts, c3)
    conv = _silu(_causal_conv(u3, cst_ref[...], w_ref[...], GDN_K)).reshape(cc, c3)
    csto_ref[...] = u3
    ga, gb = _head_inputs(gab_ref, exp_ref)
    g3 = (-jnp.exp(alog_ref[...]) * _softplus(ga + dtb_ref[...])).reshape(nseq, ts, c)
    b_tile = jax.nn.sigmoid(gb)
    tok = lax.broadcasted_iota(jnp.int32, (nseq, ts, c), 1)
    d = 1
    while d < ts:
        g3 = g3 + jnp.where(tok >= d, pltpu.roll(g3, d, axis=1), 0.0)
        d *= 2
    gc = g3.reshape(cc, c)
    g_tot = g3[:, ts - 1:ts, :]
    gl = jnp.broadcast_to(g_tot, (nseq, ts, c)).reshape(cc, c)

    rows = lax.broadcasted_iota(jnp.int32, (n2, n2), 0)
    cols = lax.broadcasted_iota(jnp.int32, (n2, n2), 1)
    same_head = (rows >> 6) == (cols >> 6)
    same_seq = (rows >> 3) == (cols >> 3)
    tril = same_seq & (rows >= cols)
    strict = same_seq & (rows > cols)
    masks = _inverse_masks(n2)[:4]
    gn = gn_ref[...]
    pairs = range(npair)

    def ls(pr):
        return slice(pr * LANE, (pr + 1) * LANE)

    def stack(x):
        return jnp.where(same_head, jnp.concatenate([x, x], axis=0), 0.0)

    def twice(x):
        return jnp.concatenate([x, x], axis=0)

    erow = (lax.broadcasted_iota(jnp.int32, (n2, nseq * n2), 0) >> 3) & (nseq - 1)
    eblk = lax.broadcasted_iota(jnp.int32, (n2, nseq * n2), 1) >> 7
    emask = erow == eblk

    def expand(x):
        return jnp.where(emask, jnp.concatenate([x] * nseq, axis=1), 0.0)

    gcm = [stack(gc[:, ls(pr)]) for pr in pairs]
    glm = [stack(gl[:, ls(pr)]) for pr in pairs]
    beta = [twice(b_tile[:, ls(pr)]) for pr in pairs]
    sol, intra, qe, k = _gdn_solve([stack(conv[:, ls(pr)]) for pr in pairs],
                                   [stack(conv[:, c + pr * LANE:c + (pr + 1) * LANE]) for pr in pairs],
                                   [stack(conv[:, 2 * c + pr * LANE:2 * c + (pr + 1) * LANE]) for pr in pairs],
                                   gcm, beta, tril, strict, masks)
    kdec = [k[pr] * jnp.exp(glm[pr] - gcm[pr]) for pr in pairs]
    zero_blk = jnp.zeros((HEAD_DIM, HEAD_DIM), F32)
    s_old = []
    sdec = []
    for pr in pairs:
        blocks = []
        decs = []
        for b in range(nseq):
            top = jnp.concatenate([s0_ref[b, 2 * pr], zero_blk], axis=-1)
            bot = jnp.concatenate([zero_blk, s0_ref[b, 2 * pr + 1]], axis=-1)
            blocks += [top, bot]
            decs.append(jnp.broadcast_to(jnp.exp(g_tot[b, :, ls(pr)]), (n2, LANE)))
        s_old.append(jnp.concatenate(blocks, axis=0))
        sdec.append(jnp.concatenate(decs, axis=0))
    v_new = [sol[pr][:, :n2] - _bdot(expand(sol[pr][:, n2:]), s_old[pr]) for pr in pairs]
    o_st = [_bdot(expand(qe[pr]), s_old[pr]) for pr in pairs]
    o_in = [_bdot(intra[pr], v_new[pr]) for pr in pairs]
    s_up = [_bdot_tn(expand(kdec[pr]), v_new[pr]) for pr in pairs]
    for pr in pairs:
        s_new = s_old[pr] * sdec[pr] + s_up[pr]
        for b in range(nseq):
            sf_ref[b, 2 * pr] = s_new[b * n2:b * n2 + HEAD_DIM, :HEAD_DIM]
            sf_ref[b, 2 * pr + 1] = s_new[b * n2 + HEAD_DIM:(b + 1) * n2, HEAD_DIM:]
        o = o_st[pr] + o_in[pr]
        o = o * lax.rsqrt(jnp.sum(o * o, axis=-1, keepdims=True) * (1.0 / HEAD_DIM) + EPS)
        o = o[:cc] + o[cc:]
        y_ref[:, ls(pr)] = o * gn[:, ls(pr)] * _silu(z_ref[:, ls(pr)].astype(F32))


def _gdn_short(proj, cst8, s0, w, alog_b, dtb_b, gn_b, expand, nbat, ts):
    nseq = GDN_CHUNK // ts
    assert ts == SUBLANE and nbat % nseq == 0
    c = BRANCH_W
    c3 = 3 * c
    tm = nseq * ts
    kern = functools.partial(_gdn_short_kernel, nseq=nseq, ts=ts)
    s_spec = pl.BlockSpec((nseq, N_HEADS, HEAD_DIM, HEAD_DIM), lambda i: (i, 0, 0, 0))
    cst_spec = pl.BlockSpec((nseq, SUBLANE, c3), lambda i: (i, 0, 0))

    def tok(width, off):
        return pl.BlockSpec((tm, width), lambda i: (i, off // width))

    return pl.pallas_call(
        kern,
        grid=(nbat // nseq,),
        in_specs=[tok(c3, OFF_GQKV), tok(LANE, OFF_GAB), tok(c, OFF_GZ), cst_spec, s_spec,
                  _const_spec((GDN_K, c3)), _const_spec((1, c)), _const_spec((1, c)), _const_spec((1, c)),
                  _const_spec((LANE, 2 * c))],
        out_specs=[pl.BlockSpec((tm, c), lambda i: (i, 0)), s_spec, cst_spec],
        out_shape=[jax.ShapeDtypeStruct((nbat * ts, c), F32),
                   jax.ShapeDtypeStruct((nbat, N_HEADS, HEAD_DIM, HEAD_DIM), F32),
                   jax.ShapeDtypeStruct((nbat, SUBLANE, c3), F32)],
        compiler_params=_cparams(1, 40),
        name="gdn_short",
    )(proj, proj, proj, cst8, s0, w, alog_b, dtb_b, gn_b, expand)


def _gdn_kernel(qkv_ref, gab_ref, z_ref, cst_ref, s0_ref, w_ref, alog_ref, dtb_ref, gn_ref, exp_ref,
                y_ref, sf_ref, csto_ref, s_scr, c_scr, qkv_s, g_ref, beta_ref, *, n_chunks):
    tc = qkv_ref.shape[0]
    c3 = qkv_ref.shape[1]
    ga, gb = _head_inputs(gab_ref, exp_ref)
    c8 = _carry_in(pl.program_id(1), cst_ref, c_scr)
    u3 = qkv_ref[...].astype(F32).reshape(1, tc, c3)
    qkv_s[...] = _silu(_causal_conv(u3, c8, w_ref[...], GDN_K)).reshape(tc, c3)
    g_ref[...] = -jnp.exp(alog_ref[...]) * _softplus(ga + dtb_ref[...])
    beta_ref[...] = jax.nn.sigmoid(gb)
    hist = u3[:, tc - SUBLANE:, :]
    c_scr[...] = hist
    csto_ref[...] = hist
    q_ref = qkv_s.at[:, 0:BRANCH_W]
    k_ref = qkv_s.at[:, BRANCH_W:2 * BRANCH_W]
    v_ref = qkv_s.at[:, 2 * BRANCH_W:3 * BRANCH_W]

    cc = GDN_CHUNK
    n2 = 2 * cc
    npair = N_HEADS // 2
    zero_blk = jnp.zeros((HEAD_DIM, HEAD_DIM), F32)

    @pl.when(pl.program_id(1) == 0)
    def _():
        for pr in range(npair):
            top = jnp.concatenate([s0_ref[0, 2 * pr], zero_blk], axis=-1)
            bot = jnp.concatenate([zero_blk, s0_ref[0, 2 * pr + 1]], axis=-1)
            s_scr[pr] = jnp.concatenate([top, bot], axis=0)

    rows = lax.broadcasted_iota(jnp.int32, (n2, n2), 0)
    cols = lax.broadcasted_iota(jnp.int32, (n2, n2), 1)
    same = (rows >> 6) == (cols >> 6)
    tril = same & (rows >= cols)
    strict = same & (rows > cols)
    masks = _inverse_masks(n2)[:7]
    gn = gn_ref[...]
    probs = [(ci, pr) for ci in range(n_chunks) for pr in range(npair)]

    def rs(ci):
        return slice(ci * cc, (ci + 1) * cc)

    def ls(pr):
        return slice(pr * LANE, (pr + 1) * LANE)

    def stack(x):
        return jnp.where(same, jnp.concatenate([x, x], axis=0), 0.0)

    def twice(x):
        return jnp.concatenate([x, x], axis=0)

    gc_all = [_cumsum_rows(g_ref[rs(ci), :], cc) for ci in range(n_chunks)]
    gcm = [stack(gc_all[ci][:, ls(pr)]) for ci, pr in probs]
    beta = [twice(beta_ref[rs(ci), ls(pr)]) for ci, pr in probs]
    sol, intra, qe, k = _gdn_solve([stack(q_ref[rs(ci), ls(pr)]) for ci, pr in probs],
                                   [stack(k_ref[rs(ci), ls(pr)]) for ci, pr in probs],
                                   [stack(v_ref[rs(ci), ls(pr)]) for ci, pr in probs],
                                   gcm, beta, tril, strict, masks)
    g_last = [gc_all[ci][cc - 1:cc, ls(pr)] for ci, pr in probs]
    kdec = [k[p] * jnp.exp(g_last[p] - gcm[p]) for p in range(len(probs))]
    sdec = [jnp.exp(g_last[p]) for p in range(len(probs))]

    for ci in range(n_chunks):
        ps = [ci * npair + pr for pr in range(npair)]
        s_old = [s_scr[pr] for pr in range(npair)]
        v_new = [sol[p][:, :n2] - _bdot(sol[p][:, n2:], s_old[pr]) for pr, p in enumerate(ps)]
        o_st = [_bdot(qe[p], s_old[pr]) for pr, p in enumerate(ps)]
        o_in = [_bdot(intra[p], v_new[pr]) for pr, p in enumerate(ps)]
        s_up = [_bdot_tn(kdec[p], v_new[pr]) for pr, p in enumerate(ps)]
        for pr, p in enumerate(ps):
            s_scr[pr] = s_old[pr] * sdec[p] + s_up[pr]
            o = o_st[pr] + o_in[pr]
            o = o * lax.rsqrt(jnp.sum(o * o, axis=-1, keepdims=True) * (1.0 / HEAD_DIM) + EPS)
            o = o[:cc] + o[cc:]
            y_ref[rs(ci), ls(pr)] = o * gn[:, ls(pr)] * _silu(z_ref[rs(ci), ls(pr)].astype(F32))

    @pl.when(pl.program_id(1) == pl.num_programs(1) - 1)
    def _():
        for pr in range(npair):
            s_pair = s_scr[pr]
            sf_ref[0, 2 * pr] = s_pair[:HEAD_DIM, :HEAD_DIM]
            sf_ref[0, 2 * pr + 1] = s_pair[HEAD_DIM:, HEAD_DIM:]


def _gdn(proj, cst8, s0, w, alog_b, dtb_b, gn_b, expand, nbat, t, tc):
    tper = t // tc
    c = BRANCH_W
    c3 = 3 * c
    kern = functools.partial(_gdn_kernel, n_chunks=tc // GDN_CHUNK)
    s_spec = pl.BlockSpec((1, N_HEADS, HEAD_DIM, HEAD_DIM), lambda b, t_: (b, 0, 0, 0))
    cst_spec = pl.BlockSpec((1, SUBLANE, c3), lambda b, t_: (b, 0, 0))
    return pl.pallas_call(
        kern,
        grid=(nbat, tper),
        in_specs=[_tok_spec(tc, c3, OFF_GQKV // c3, tper), _tok_spec(tc, LANE, OFF_GAB // LANE, tper),
                  _tok_spec(tc, c, OFF_GZ // c, tper),
                  cst_spec, s_spec, _const_spec((GDN_K, c3)), _const_spec((1, c)), _const_spec((1, c)),
                  _const_spec((1, c)), _const_spec((LANE, 2 * c))],
        out_specs=[_tok_spec(tc, c, 0, tper), s_spec, cst_spec],
        out_shape=[jax.ShapeDtypeStruct((nbat * t, c), F32),
                   jax.ShapeDtypeStruct((nbat, N_HEADS, HEAD_DIM, HEAD_DIM), F32),
                   jax.ShapeDtypeStruct((nbat, SUBLANE, c3), F32)],
        scratch_shapes=[pltpu.VMEM((N_HEADS // 2, 2 * HEAD_DIM, 2 * HEAD_DIM), F32),
                        pltpu.VMEM((1, SUBLANE, c3), F32), pltpu.VMEM((tc, c3), F32),
                        pltpu.VMEM((tc, c), F32), pltpu.VMEM((tc, c), F32)],
        compiler_params=_cparams(2, 40),
        name="gdn_rule",
    )(proj, proj, proj, cst8, s0, w, alog_b, dtb_b, gn_b, expand)


def _merge_kernel(x_ref, mod_ref, g_ref, bg_ref, cg_ref, xt_ref, st_ref, wsc_ref, yb_ref, yc_ref, yd_ref,
                  wg_ref, wbo_ref, wmix_ref, o_ref, sto_ref, c_scr):
    x = x_ref[...]
    nb, tt, d = x.shape
    c = bg_ref.shape[-1]
    mod = mod_ref[...]
    h = _modulated_norm(x, g_ref[...], mod, 0, 1).reshape(nb * tt, d).astype(BF16)
    c8 = _carry_in(pl.program_id(1), st_ref, c_scr)
    u3 = (cg_ref[...].astype(F32) * xt_ref[...].astype(F32)).reshape(nb, tt, c)
    y_a = bg_ref[...].astype(F32) * _causal_conv(u3, c8, wsc_ref[...], SC_K).reshape(nb * tt, c)
    hist = u3[:, tt - SUBLANE:, :]
    c_scr[...] = hist
    sto_ref[...] = hist
    acc = None
    for n, y in enumerate((y_a, yb_ref, yc_ref, yd_ref)):
        gate = jax.nn.sigmoid(jnp.dot(h, wg_ref[:, n * d:(n + 1) * d], preferred_element_type=F32))
        term = gate * _bdot(y if n == 0 else y[...], wbo_ref[n])
        acc = term if acc is None else acc + term
    mix = _bdot(acc, wmix_ref[...])
    o_ref[...] = x + mod[:, 2:3, :] * mix.reshape(nb, tt, d)


def _merge(x3, mod, g, proj, st8, wsc, ys, wg, wbo, wmix, nb, tt):
    nbat, t, d = x3.shape
    tper = t // tt
    tm = nb * tt
    c = BRANCH_W
    x_spec = pl.BlockSpec((nb, tt, d), lambda b, t_: (b, t_, 0))
    return pl.pallas_call(
        _merge_kernel,
        grid=(nbat // nb, tper),
        in_specs=[x_spec, pl.BlockSpec((nb, 6, d), lambda b, t_: (b, 0, 0)), _const_spec((1, d)),
                  _tok_spec(tm, c, OFF_SC // c, tper), _tok_spec(tm, c, OFF_SC // c + 1, tper),
                  _tok_spec(tm, c, OFF_SC // c + 2, tper), _state_spec(nb, c), _const_spec((SC_K, c))]
                 + [_tok_spec(tm, c, 0, tper)] * 3
                 + [_const_spec(wg.shape), _const_spec(wbo.shape), _const_spec(wmix.shape)],
        out_specs=[x_spec, _state_spec(nb, c)],
        out_shape=[jax.ShapeDtypeStruct(x3.shape, F32), jax.ShapeDtypeStruct((nbat, SUBLANE, c), F32)],
        scratch_shapes=[pltpu.VMEM((nb, SUBLANE, c), F32)],
        compiler_params=_cparams(2, 48),
        name="merge",
    )(x3, mod, g, proj, proj, proj, st8, wsc, *ys, wg, wbo, wmix)


def _ffn_kernel(x_ref, mod_ref, g_ref, win_ref, wout_ref, gf_ref, o_ref, *, final):
    x = x_ref[...]
    nb, tt, d = x.shape
    mod = mod_ref[...]
    fh = wout_ref.shape[0]
    h = _modulated_norm(x, g_ref[...], mod, 3, 4).reshape(nb * tt, d).astype(BF16)
    gate = jnp.dot(h, win_ref[:, :fh], preferred_element_type=F32)
    up = jnp.dot(h, win_ref[:, fh:], preferred_element_type=F32)
    out = _bdot(_silu(gate) * up, wout_ref[...])
    xo = x + mod[:, 5:6, :] * out.reshape(nb, tt, d)
    o_ref[...] = _rms(xo, gf_ref[...]) if final else xo


def _ffn(x3, mod, g, win, wout, gf, nb, tt, final):
    nbat, t, d = x3.shape
    x_spec = pl.BlockSpec((nb, tt, d), lambda b, t_: (b, t_, 0))
    kern = functools.partial(_ffn_kernel, final=final)
    return pl.pallas_call(
        kern,
        grid=(nbat // nb, t // tt),
        in_specs=[x_spec, pl.BlockSpec((nb, 6, d), lambda b, t_: (b, 0, 0)), _const_spec((1, d)),
                  _const_spec(win.shape), _const_spec(wout.shape), _const_spec((1, d))],
        out_specs=x_spec,
        out_shape=jax.ShapeDtypeStruct(x3.shape, F32),
        compiler_params=_cparams(2, 56),
        name="ffn",
    )(x3, mod, g, win, wout, gf)


def _block_diag(blocks):
    h, r, c = blocks.shape
    eye = jnp.eye(h, dtype=blocks.dtype)
    return jnp.einsum('hrc,hg->hrgc', blocks, eye).reshape(h * r, h * c)


def _prep_layer_weights(w_in, w_qb, w_kvb, w_lru_gate_a, w_lru_gate_x):
    d = w_in.shape[0]
    c = BRANCH_W
    o = 0
    segs = {}
    for name, width in (('sc', 3 * c), ('qa', MLA_QLORA), ('ckv', MLA_KVLORA), ('kpe', MLA_ROPE), ('gqkv', 3 * c),
                        ('gz', c), ('ga', N_HEADS), ('gb', N_HEADS), ('lx', c), ('lg', c)):
        segs[name] = w_in[:, o:o + width]
        o += width
    half = MLA_ROPE // 2
    zpad = jnp.zeros((d, LANE - MLA_ROPE), F32)
    kpe = segs['kpe']
    kpa = jnp.concatenate([kpe, zpad], axis=1)
    kpb = jnp.concatenate([-kpe[:, half:], kpe[:, :half], zpad], axis=1)
    w_proj = jnp.concatenate(
        [segs['sc'], segs['gqkv'], segs['gz'], segs['lx'], segs['lg'], segs['qa'], segs['ckv'], kpa, kpb,
         segs['ga'], segs['gb'], jnp.zeros((d, LANE - 2 * N_HEADS), F32)],
        axis=1).astype(BF16)

    ql = w_qb.shape[0]
    wq = w_qb.reshape(ql, N_HEADS, HEAD_DIM + MLA_ROPE)
    wn = wq[:, :, :HEAD_DIM].reshape(ql, c)
    pe = wq[:, :, HEAD_DIM:]
    zq = jnp.zeros((ql, N_HEADS, LANE - MLA_ROPE), F32)
    wpa = jnp.concatenate([pe, zq], axis=2).reshape(ql, N_HEADS * LANE)
    wpb = jnp.concatenate([-pe[:, :, half:], pe[:, :, :half], zq], axis=2).reshape(ql, N_HEADS * LANE)
    w_kb = w_kvb[:, :, :HEAD_DIM]
    w_vb = w_kvb[:, :, HEAD_DIM:]
    bdk = _block_diag(jnp.transpose(w_kb, (1, 2, 0)))
    bdv = _block_diag(jnp.transpose(w_vb, (1, 0, 2)))
    wlg = jnp.concatenate([_block_diag(w_lru_gate_a), _block_diag(w_lru_gate_x)], axis=1)
    wvt = jnp.transpose(w_vb, (1, 2, 0))
    return dict(w_proj=w_proj, wn=wn.astype(BF16), wpa=wpa.astype(BF16), wpb=wpb.astype(BF16),
                bdk=bdk.astype(BF16), bdv=bdv.astype(BF16), wvt=wvt.astype(BF16), wlg=wlg.astype(BF16))


def _rope_tables(pos):
    half = MLA_ROPE // 2
    inv = ROPE_THETA ** (-jnp.arange(half, dtype=F32) / half)
    ang = pos.astype(F32)[:, None] * inv[None, :]
    pad = jnp.zeros((pos.shape[0], LANE - MLA_ROPE), F32)
    cos = jnp.concatenate([jnp.cos(ang), jnp.cos(ang), pad], axis=1)
    sin = jnp.concatenate([jnp.sin(ang), jnp.sin(ang), pad], axis=1)
    return cos, sin


def _pad_state(st, k_w):
    return jnp.pad(st, ((0, 0), (SUBLANE - (k_w - 1), 0), (0, 0)))


def _tile(t, pref):
    tt = min(t, pref)
    while t % tt:
        tt //= 2
    return tt


def _group_layer(x3, mod, lw, p, st, rope, attend, cfg):
    nbat, t, d = x3.shape
    nb, tt = cfg['nb'], cfg['tt']
    proj = _inproj(x3, mod, p['g_norm_mix'], lw['w_proj'], cfg['nb_proj'], cfg['tt_proj'])
    q_full, ckv, kpe, kf, *vt = _mla_pre(proj, rope[0], rope[1], p['g_q_norm'], p['g_kv_norm'], lw['wn'], lw['wpa'],
                                         lw['wpb'], lw['bdk'], nbat, t, nb, tt, cfg['q_dtype'], cfg['emit_vt'])
    y_b = attend(q_full, ckv, kpe, kf, vt, lw)
    gdn_w = (p['w_gdn_conv'], p['alog_b'], p['dtb_b'], p['gn_b'], p['head_expand'])
    if t < GDN_CHUNK:
        y_c, s_gdn, gc8 = _gdn_short(proj, st['gdn_conv'], st['gdn'], *gdn_w, nbat, t)
    else:
        y_c, s_gdn, gc8 = _gdn(proj, st['gdn_conv'], st['gdn'], *gdn_w, nbat, t, cfg['tc'])
    y_d, lc8, h_last = _lru(proj, st['lru_conv'], st['lru'], p['w_lru_conv'], p['b_lru_conv'], lw['wlg'],
                            p['b_lru_gates'], p['nsl'], nbat, t, cfg['nb_scan'], cfg['tt_scan'], cfg['pos0'])
    x1, sc8 = _merge(x3, mod, p['g_norm_mix'], proj, st['sconv'], p['w_sc_conv'], (y_b, y_c, y_d),
                     p['w_merge_gate'], p['w_branch_out'], p['w_mix_out'], cfg['nb_mm'], cfg['tt_mm'])
    x2 = _ffn(x1, mod, p['g_norm_ffn'], p['w_ffn_in'], p['w_ffn_out'], p['g_final'], cfg['nb_mm'], cfg['tt_mm'],
              cfg['final'])
    new_st = dict(ckv=ckv.reshape(nbat, t, -1), kpe=kpe.reshape(nbat, t, -1),
                  sconv=sc8[:, SUBLANE - (SC_K - 1):], gdn_conv=gc8[:, SUBLANE - (GDN_K - 1):], gdn=s_gdn,
                  lru_conv=lc8[:, SUBLANE - (LRU_K - 1):], lru=h_last[:, 0, :])
    return x2, new_st


STATE_KEYS = ('ckv', 'kpe', 'sconv', 'gdn_conv', 'gdn', 'lru_conv', 'lru')


def kernel(x_prompt, x_sample, c_prompt, c_sample, cache_mla_ckv, cache_mla_kpe, page_table, state_sconv, state_gdn_conv, state_gdn, state_lru_conv, state_lru, w_ada, b_ada, g_norm_mix, g_norm_ffn, w_in, w_sc_conv, g_q_norm, w_qb, g_kv_norm, w_kvb, w_gdn_conv, gdn_a_log, gdn_dt_bias, g_gdn_norm, w_lru_conv, b_lru_conv, w_lru_gate_a, b_lru_gate_a, w_lru_gate_x, b_lru_gate_x, lru_lambda, w_branch_out, w_merge_gate, w_mix_out, w_ffn_in, w_ffn_out, g_final):
    bp, tp, d = x_prompt.shape
    bs, ts, _ = x_sample.shape
    depth = w_in.shape[0]
    n_pages = page_table.shape[1]
    past_len = n_pages * cache_mla_ckv.shape[2]
    c = BRANCH_W
    assert ts == SUBLANE and tp % GDN_CHUNK == 0 and d % LANE == 0

    mod_all = _ada(jnp.concatenate([c_prompt, c_sample], axis=0), w_ada, b_ada).reshape(depth, bp + bs, 6, d)

    nb_s = _tile(bs, 32)
    cos_p, sin_p = _rope_tables(jnp.arange(tp, dtype=jnp.int32))
    cos_s, sin_s = _rope_tables(past_len + jnp.arange(ts, dtype=jnp.int32))
    rope_p = (cos_p, sin_p)
    rope_s = (jnp.tile(cos_s, (nb_s, 1)), jnp.tile(sin_s, (nb_s, 1)))

    tq = _tile(tp, 256)
    tk = _tile(tp, 256)
    cfg_p = dict(nb=1, tt=tk, nb_proj=1, tt_proj=_tile(tp, 1024), q_dtype=BF16, emit_vt=True,
                 tc=_tile(tp, 256), pos0=0, nb_scan=1, tt_scan=_tile(tp, 256), nb_mm=1, tt_mm=_tile(tp, 512))
    cfg_s = dict(nb=nb_s, tt=ts, nb_proj=_tile(bs, 128), tt_proj=ts, q_dtype=F32, emit_vt=False,
                 tc=GDN_CHUNK, pos0=past_len, nb_scan=nb_s, tt_scan=ts, nb_mm=_tile(bs, 64), tt_mm=ts)
    cache_kpe_t = jnp.swapaxes(cache_mla_kpe, 2, 3)
    head_expand = (jnp.arange(LANE)[:, None] == jnp.arange(2 * c)[None, :] // HEAD_DIM).astype(BF16)

    xp, xs = x_prompt, x_sample
    out_p = {k: [] for k in STATE_KEYS}
    out_s = {k: [] for k in STATE_KEYS}
    for l in range(depth):
        lw = _prep_layer_weights(w_in[l], w_qb[l], w_kvb[l], w_lru_gate_a[l], w_lru_gate_x[l])
        p = dict(
            g_norm_mix=g_norm_mix[l][None], g_norm_ffn=g_norm_ffn[l][None], g_final=g_final[None],
            w_sc_conv=w_sc_conv[l], g_q_norm=g_q_norm[l][None], g_kv_norm=g_kv_norm[l][None],
            w_gdn_conv=w_gdn_conv[l],
            alog_b=jnp.repeat(gdn_a_log[l], HEAD_DIM)[None], dtb_b=jnp.repeat(gdn_dt_bias[l], HEAD_DIM)[None],
            gn_b=jnp.tile(g_gdn_norm[l], N_HEADS)[None], head_expand=head_expand,
            w_lru_conv=w_lru_conv[l], b_lru_conv=b_lru_conv[l][None],
            b_lru_gates=jnp.concatenate([b_lru_gate_a[l], b_lru_gate_x[l]])[None],
            nsl=lru_lambda[l][None],
            w_merge_gate=w_merge_gate[l].astype(BF16), w_branch_out=w_branch_out[l].astype(BF16),
            w_mix_out=w_mix_out[l].astype(BF16), w_ffn_in=w_ffn_in[l].astype(BF16),
            w_ffn_out=w_ffn_out[l].astype(BF16))
        final = l == depth - 1

        st_p = dict(sconv=jnp.zeros((bp, SUBLANE, c), F32), gdn_conv=jnp.zeros((bp, SUBLANE, 3 * c), F32),
                    gdn=jnp.zeros((bp, N_HEADS, HEAD_DIM, HEAD_DIM), F32),
                    lru_conv=jnp.zeros((bp, SUBLANE, c), F32), lru=jnp.zeros((bp, 1, c), F32))

        def attend_p(q_full, ckv, kpe, kf, vt, lw_):
            return _attn_prompt(q_full, kf, vt[0], lw_['wvt'], bp, tp, tq, tk)

        xp, nst_p = _group_layer(xp, mod_all[l, :bp], lw, p, st_p, rope_p, attend_p, dict(cfg_p, final=final))

        st_s = dict(sconv=_pad_state(state_sconv[l], SC_K), gdn_conv=_pad_state(state_gdn_conv[l], GDN_K),
                    gdn=state_gdn[l], lru_conv=_pad_state(state_lru_conv[l], LRU_K), lru=state_lru[l][:, None, :])

        def attend_s(q_full, ckv, kpe, kf, vt, lw_, layer=l):
            return _attn_sample(q_full, cache_mla_ckv, cache_kpe_t, page_table, layer, ckv, kpe, lw_['bdv'], bs, ts)

        xs, nst_s = _group_layer(xs, mod_all[l, bp:], lw, p, st_s, rope_s, attend_s, dict(cfg_s, final=final))
        for k in STATE_KEYS:
            out_p[k].append(nst_p[k])
            out_s[k].append(nst_s[k])

    return ((xp, xs) + tuple(jnp.stack(out_p[k]) for k in STATE_KEYS)
            + tuple(jnp.stack(out_s[k]) for k in STATE_KEYS))
```

```python
import functools
import math

import jax
import jax.numpy as jnp
from jax import lax
from jax.experimental import pallas as pl
from jax.experimental.pallas import tpu as pltpu

F32 = jnp.float32
BF16 = jnp.bfloat16

HEAD_DIM = 64
N_HEADS = 8
BRANCH_W = N_HEADS * HEAD_DIM
MLA_ROPE = HEAD_DIM // 2
MLA_QLORA = 256
MLA_KVLORA = 128
ROPE_THETA = 10000.0
SC_K = 3
GDN_K = 4
LRU_K = 4
LRU_C = 8.0
GDN_CHUNK = 64
EPS = 1e-6
NEG_BIG = -1e30
LANE = 128
SUBLANE = 8
QK_W = 2 * LANE
ATTN_GROUP_W = 256

OFF_SC = 0
OFF_GQKV = 1536
OFF_GZ = 3072
OFF_LX = 3584
OFF_LG = 4096
OFF_QA = 4608
OFF_CKV = 4864
OFF_KPA = 4992
OFF_KPB = 5120
OFF_GAB = 5248
PROJ_W = 5376
PROJ_TN = 896


def _cparams(n_axes, vmem_mib):
    return pltpu.CompilerParams(dimension_semantics=("arbitrary",) * n_axes,
                                vmem_limit_bytes=vmem_mib * 1024 * 1024)


def _const_spec(shape):
    nd = len(shape)
    return pl.BlockSpec(shape, lambda *_: (0,) * nd, pipeline_mode=pl.Buffered(1))


def _bdot(a, b):
    return jnp.dot(a.astype(BF16), b.astype(BF16), preferred_element_type=F32)


def _bdot_nt(a, b):
    return lax.dot_general(a.astype(BF16), b.astype(BF16), (((1,), (1,)), ((), ())),
                           preferred_element_type=F32)


def _bdot_tn(a, b):
    return lax.dot_general(a.astype(BF16), b.astype(BF16), (((0,), (0,)), ((), ())),
                           preferred_element_type=F32)


def _silu(x):
    return x * jax.nn.sigmoid(x)


def _softplus(x):
    return jnp.maximum(x, 0.0) + jnp.log(1.0 + jnp.exp(-jnp.abs(x)))


def _gelu_tanh(x):
    return 0.5 * x * (1.0 + jnp.tanh(math.sqrt(2.0 / math.pi) * (x + 0.044715 * (x * x * x))))


def _rms(x, g):
    return x * lax.rsqrt(jnp.mean(x * x, axis=-1, keepdims=True) + EPS) * g


def _modulated_norm(x3, g, mod, shift_row, scale_row):
    y = _rms(x3, g)
    return y * (1.0 + mod[:, scale_row:scale_row + 1, :]) + mod[:, shift_row:shift_row + 1, :]


def _ada_kernel(c_ref, w_ref, b_ref, o_ref):
    o_ref[0] = _bdot(_silu(c_ref[...]), w_ref[0]) + b_ref[0]


def _ada(c_all, w_ada, b_ada):
    depth, d, n = w_ada.shape
    nb = c_all.shape[0]
    tn = n // 4
    return pl.pallas_call(
        _ada_kernel,
        grid=(depth, n // tn),
        in_specs=[pl.BlockSpec((nb, d), lambda l, j: (0, 0)),
                  pl.BlockSpec((1, d, tn), lambda l, j: (l, 0, j)),
                  pl.BlockSpec((1, 1, tn), lambda l, j: (l, 0, j))],
        out_specs=pl.BlockSpec((1, nb, tn), lambda l, j: (l, 0, j)),
        out_shape=jax.ShapeDtypeStruct((depth, nb, n), F32),
        compiler_params=_cparams(2, 40),
        name="ada_mod",
    )(c_all, w_ada, b_ada.reshape(depth, 1, n))


def _inproj_kernel(x_ref, mod_ref, g_ref, w_ref, o_ref, h_scr):
    @pl.when(pl.program_id(1) == 0)
    def _():
        h = _modulated_norm(x_ref[...], g_ref[...], mod_ref[...], 0, 1)
        h_scr[...] = h.reshape(h_scr.shape).astype(BF16)

    o_ref[...] = jnp.dot(h_scr[...], w_ref[...], preferred_element_type=F32).astype(o_ref.dtype)


def _inproj(x3, mod, g, w, nb, tt):
    nbat, t, d = x3.shape
    tper = t // tt
    tm = nb * tt
    n_m = (nbat // nb) * tper
    return pl.pallas_call(
        _inproj_kernel,
        grid=(n_m, PROJ_W // PROJ_TN),
        in_specs=[pl.BlockSpec((nb, tt, d), lambda i, j: (i // tper, i % tper, 0)),
                  pl.BlockSpec((nb, 6, d), lambda i, j: (i // tper, 0, 0)),
                  _const_spec((1, d)),
                  pl.BlockSpec((d, PROJ_TN), lambda i, j: (0, j))],
        out_specs=pl.BlockSpec((tm, PROJ_TN), lambda i, j: (i, j)),
        out_shape=jax.ShapeDtypeStruct((nbat * t, PROJ_W), BF16),
        scratch_shapes=[pltpu.VMEM((tm, d), BF16)],
        compiler_params=_cparams(2, 40),
        name="in_proj",
    )(x3, mod, g, w)


def _causal_conv(u3, c8, w, k_w):
    nb, tt, c = u3.shape
    rows = lax.broadcasted_iota(jnp.int32, (nb, SUBLANE, c), 1)
    acc = None
    for j in range(k_w):
        s = k_w - 1 - j
        if s == 0:
            sh = u3
        else:
            full = pltpu.roll(u3, s, axis=1)
            top = jnp.where(rows < s, pltpu.roll(c8, s, axis=1), full[:, 0:SUBLANE, :])
            sh = top if tt == SUBLANE else jnp.concatenate([top, full[:, SUBLANE:, :]], axis=1)
        term = sh * w[j:j + 1, :]
        acc = term if acc is None else acc + term
    return acc


def _carry_in(t_idx, st_ref, c_scr):
    @pl.when(t_idx == 0)
    def _():
        c_scr[...] = st_ref[...]

    return c_scr[...]


def _tok_spec(tm, width, col_block, tper):
    return pl.BlockSpec((tm, width), lambda b, t: (b * tper + t, col_block))


def _state_spec(nb, c):
    return pl.BlockSpec((nb, SUBLANE, c), lambda b, t: (b, 0, 0))


def _lru_kernel(lx_ref, lg_ref, st_ref, h0_ref, w_ref, cb_ref, wg_ref, bg_ref, lam_ref,
                y_ref, sto_ref, hl_ref, c_scr, h_scr, *, nb, tt, pos0):
    c = lx_ref.shape[-1]
    t_idx = pl.program_id(1)
    c8 = _carry_in(t_idx, st_ref, c_scr)

    @pl.when(t_idx == 0)
    def _():
        h_scr[...] = h0_ref[...]

    u3 = lx_ref[...].astype(F32).reshape(nb, tt, c)
    u = (_causal_conv(u3, c8, w_ref[...], LRU_K) + cb_ref[...]).reshape(nb * tt, c)
    last = u3[:, tt - SUBLANE:, :]
    c_scr[...] = last
    sto_ref[...] = last
    gates = jax.nn.sigmoid(_bdot(u, wg_ref[...]) + bg_ref[...])
    r = gates[:, :c]
    ig = gates[:, c:]
    log_a = -LRU_C * r * _softplus(-lam_ref[...])
    rows = lax.broadcasted_iota(jnp.int32, (nb, tt, c), 1)
    mult = jnp.sqrt(1.0 - jnp.exp(2.0 * log_a)).reshape(nb, tt, c)
    mult = jnp.where(pos0 + t_idx * tt + rows == 0, 1.0, mult)
    grp = tt // SUBLANE
    a = jnp.exp(log_a).reshape(nb * grp, SUBLANE, c)
    b = (mult * (ig * u).reshape(nb, tt, c)).reshape(nb * grp, SUBLANE, c)
    tok = lax.broadcasted_iota(jnp.int32, (nb * grp, SUBLANE, c), 1)
    d = 1
    while d < SUBLANE:
        keep = tok >= d
        a_s = jnp.where(keep, pltpu.roll(a, d, axis=1), 1.0)
        b_s = jnp.where(keep, pltpu.roll(b, d, axis=1), 0.0)
        b = a * b_s + b
        a = a * a_s
        d *= 2
    a = a.reshape(nb, grp, SUBLANE, c)
    b = b.reshape(nb, grp, SUBLANE, c)
    h_in = h_scr[...]
    parts = []
    for g in range(grp):
        part = b[:, g] + a[:, g] * h_in
        h_in = part[:, SUBLANE - 1:SUBLANE, :]
        parts.append(part)
    hs = parts[0] if grp == 1 else jnp.concatenate(parts, axis=1)
    h_last = h_in
    h_scr[...] = h_last
    hl_ref[...] = h_last
    y_ref[...] = hs.reshape(nb * tt, c) * _gelu_tanh(lg_ref[...].astype(F32))


def _lru(proj, st8, h0, w, cb, wg, bg, lam, nbat, t, nb, tt, pos0):
    tper = t // tt
    tm = nb * tt
    c = BRANCH_W
    kern = functools.partial(_lru_kernel, nb=nb, tt=tt, pos0=pos0)
    h_spec = pl.BlockSpec((nb, 1, c), lambda b, t_: (b, 0, 0))
    return pl.pallas_call(
        kern,
        grid=(nbat // nb, tper),
        in_specs=[_tok_spec(tm, c, OFF_LX // c, tper), _tok_spec(tm, c, OFF_LG // c, tper), _state_spec(nb, c),
                  h_spec, _const_spec((LRU_K, c)), _const_spec((1, c)), _const_spec((c, 2 * c)),
                  _const_spec((1, 2 * c)), _const_spec((1, c))],
        out_specs=[_tok_spec(tm, c, 0, tper), _state_spec(nb, c), h_spec],
        out_shape=[jax.ShapeDtypeStruct((nbat * t, c), F32), jax.ShapeDtypeStruct((nbat, SUBLANE, c), F32),
                   jax.ShapeDtypeStruct((nbat, 1, c), F32)],
        scratch_shapes=[pltpu.VMEM((nb, SUBLANE, c), F32), pltpu.VMEM((nb, 1, c), F32)],
        compiler_params=_cparams(2, 40),
        name="lru",
    )(proj, proj, st8, h0, w, cb, wg, bg, lam)


def _mla_pre_kernel(qa_ref, ckv_ref, kpa_ref, kpb_ref, cos_ref, sin_ref, gq_ref, gkv_ref,
                    wn_ref, wpa_ref, wpb_ref, bdk_ref, q_out, ckv_out, kpe_out, kf_out, *rest, nb, tt):
    cos = cos_ref[...]
    sin = sin_ref[...]
    cq = _rms(qa_ref[...].astype(F32), gq_ref[...]).astype(BF16)
    qn = jnp.dot(cq, wn_ref[...], preferred_element_type=F32)
    qabs = _bdot(qn, bdk_ref[...])
    pa = jnp.dot(cq, wpa_ref[...], preferred_element_type=F32)
    pb = jnp.dot(cq, wpb_ref[...], preferred_element_type=F32)
    for h in range(N_HEADS):
        sl = slice(h * LANE, (h + 1) * LANE)
        q_pe = pa[:, sl] * cos + pb[:, sl] * sin
        q_out[:, h, :, 0:LANE] = qabs[:, sl].reshape(nb, tt, LANE).astype(q_out.dtype)
        q_out[:, h, :, LANE:QK_W] = q_pe.reshape(nb, tt, LANE).astype(q_out.dtype)
    ckv = _rms(ckv_ref[...].astype(F32), gkv_ref[...])
    kpe = kpa_ref[...].astype(F32) * cos + kpb_ref[...].astype(F32) * sin
    ckv_out[...] = ckv
    kpe_out[...] = kpe[:, :MLA_ROPE]
    kf_out[...] = jnp.concatenate([ckv, kpe], axis=-1).astype(BF16)
    if rest:
        rest[0][0] = ckv.T.astype(BF16)


def _mla_pre(proj, cos, sin, gq, gkv, wn, wpa, wpb, bdk, nbat, t, nb, tt, q_dtype, emit_vt):
    tper = t // tt
    tm = nb * tt
    hw = N_HEADS * LANE
    kern = functools.partial(_mla_pre_kernel, nb=nb, tt=tt)
    tab_spec = pl.BlockSpec((tm, LANE), lambda b, t_: (t_, 0))
    vt_specs, vt_shapes = [], []
    if emit_vt:
        assert nb == 1
        vt_specs = [pl.BlockSpec((1, MLA_KVLORA, tt), lambda b, t_: (b * tper + t_, 0, 0))]
        vt_shapes = [jax.ShapeDtypeStruct((nbat * tper, MLA_KVLORA, tt), BF16)]
    return pl.pallas_call(
        kern,
        grid=(nbat // nb, tper),
        in_specs=[_tok_spec(tm, MLA_QLORA, OFF_QA // MLA_QLORA, tper), _tok_spec(tm, LANE, OFF_CKV // LANE, tper),
                  _tok_spec(tm, LANE, OFF_KPA // LANE, tper), _tok_spec(tm, LANE, OFF_KPB // LANE, tper),
                  tab_spec, tab_spec, _const_spec((1, MLA_QLORA)), _const_spec((1, MLA_KVLORA)),
                  _const_spec((MLA_QLORA, BRANCH_W)), _const_spec((MLA_QLORA, hw)), _const_spec((MLA_QLORA, hw)),
                  _const_spec((BRANCH_W, hw))],
        out_specs=[pl.BlockSpec((nb, N_HEADS, tt, QK_W), lambda b, t_: (b, 0, t_, 0)),
                   _tok_spec(tm, MLA_KVLORA, 0, tper), _tok_spec(tm, MLA_ROPE, 0, tper), _tok_spec(tm, QK_W, 0, tper)]
                  + vt_specs,
        out_shape=[jax.ShapeDtypeStruct((nbat, N_HEADS, t, QK_W), q_dtype),
                   jax.ShapeDtypeStruct((nbat * t, MLA_KVLORA), F32),
                   jax.ShapeDtypeStruct((nbat * t, MLA_ROPE), F32),
                   jax.ShapeDtypeStruct((nbat * t, QK_W), BF16)] + vt_shapes,
        compiler_params=_cparams(2, 48),
        name="mla_pre",
    )(proj, proj, proj, proj, cos, sin, gq, gkv, wn, wpa, wpb, bdk)


def _heads_to_lanes(o, rows_per_head):
    return jnp.concatenate([o[h * rows_per_head:(h + 1) * rows_per_head] for h in range(N_HEADS)], axis=-1)


def _attn_prompt_kernel(q_ref, kf_ref, vt_ref, wvt_ref, y_ref, m_scr, l_scr, acc_scr, s_scr, *, tq, tk, scale):
    i = pl.program_id(1)
    n_full = (i * tq) // tk
    hpg = ATTN_GROUP_W // tq
    w2 = hpg * tq
    krow = lax.broadcasted_iota(jnp.int32, (tk, w2), 0)
    qcol = i * tq + (lax.broadcasted_iota(jnp.int32, (tk, w2), 1) & (tq - 1))
    heads = range(N_HEADS // hpg)
    m_scr[...] = jnp.full(m_scr.shape, NEG_BIG, F32)
    l_scr[...] = jnp.zeros(l_scr.shape, F32)
    acc_scr[...] = jnp.zeros(acc_scr.shape, F32)

    def scores(j):
        kblk = kf_ref[pl.ds(pl.multiple_of(j * tk, tk), tk), :]
        return [lax.dot_general(kblk, q_ref[0, hpg * h:hpg * (h + 1)].reshape(w2, QK_W), (((1,), (1,)), ((), ())),
                                preferred_element_type=F32)
                for h in heads]

    c2 = scale * math.log2(math.e)

    def step(j, masked):
        vt = vt_ref[j]
        s = [s_scr[h] for h in heads]
        if masked:
            keep = j * tk + krow <= qcol
            s = [jnp.where(keep, s[h], NEG_BIG) for h in heads]
        else:
            s_next = scores(j + 1)
            for h in heads:
                s_scr[h] = s_next[h]
        m_old = [m_scr[h] for h in heads]
        m_new = [jnp.maximum(m_old[h], jnp.max(s[h], axis=0, keepdims=True)) for h in heads]
        p = [jnp.exp2((s[h] - m_new[h]) * c2) for h in heads]
        alpha = [jnp.exp2((m_old[h] - m_new[h]) * c2) for h in heads]
        pv = [jnp.dot(vt, p[h].astype(BF16), preferred_element_type=F32) for h in heads]
        for h in heads:
            m_scr[h] = m_new[h]
            l_scr[h] = alpha[h] * l_scr[h] + jnp.sum(p[h], axis=0, keepdims=True)
            acc_scr[h] = alpha[h] * acc_scr[h] + pv[h]

    def body(j, carry):
        step(j, False)
        return carry

    s_first = scores(0)
    for h in heads:
        s_scr[h] = s_first[h]
    lax.fori_loop(0, n_full, body, 0)
    step(n_full, True)
    o = [acc_scr[h] / l_scr[h] for h in heads]
    outs = [_bdot(wvt_ref[hh], o[hh // hpg][:, (hh % hpg) * tq:(hh % hpg + 1) * tq]) for hh in range(N_HEADS)]
    y_ref[...] = jnp.concatenate(outs, axis=0).T


def _attn_prompt(q_full, kf, vt, wvt, nbat, t, tq, tk):
    tper = t // tq
    kper = t // tk
    assert ATTN_GROUP_W % tq == 0
    ngrp = N_HEADS * tq // ATTN_GROUP_W
    scale = (HEAD_DIM + MLA_ROPE) ** -0.5
    kern = functools.partial(_attn_prompt_kernel, tq=tq, tk=tk, scale=scale)
    return pl.pallas_call(
        kern,
        grid=(nbat, tper),
        in_specs=[pl.BlockSpec((1, N_HEADS, tq, QK_W), lambda b, i: (b, 0, i, 0)),
                  pl.BlockSpec((t, QK_W), lambda b, i: (b, 0)),
                  pl.BlockSpec((kper, MLA_KVLORA, tk), lambda b, i: (b, 0, 0)),
                  _const_spec(wvt.shape)],
        out_specs=pl.BlockSpec((tq, BRANCH_W), lambda b, i: (b * tper + i, 0)),
        out_shape=jax.ShapeDtypeStruct((nbat * t, BRANCH_W), F32),
        scratch_shapes=[pltpu.VMEM((ngrp, 1, ATTN_GROUP_W), F32), pltpu.VMEM((ngrp, 1, ATTN_GROUP_W), F32),
                        pltpu.VMEM((ngrp, MLA_KVLORA, ATTN_GROUP_W), F32),
                        pltpu.VMEM((ngrp, tk, ATTN_GROUP_W), F32)],
        compiler_params=_cparams(2, 40),
        name="attn_prompt",
    )(q_full, kf, vt, wvt)


def _attn_sample_kernel(pt_ref, q_ref, *refs, n_pages, pg, ts, page, scale):
    ckv_pages = refs[:n_pages]
    kpe_pages = refs[n_pages:2 * n_pages]
    ckvn_ref, kpen_ref, bdv_ref, y_ref = refs[2 * n_pages:]
    r = N_HEADS * ts
    groups = range(n_pages // pg)

    q = q_ref[0].reshape(r, QK_W)
    q_lat = q[:, :MLA_KVLORA].astype(BF16)
    q_pe = q[:, MLA_KVLORA:MLA_KVLORA + MLA_ROPE].astype(BF16)

    c_all = [jnp.concatenate([ckv_pages[g * pg + i][0, 0].astype(BF16) for i in range(pg)], axis=0)
             for g in groups]
    kpt_all = [jnp.concatenate([kpe_pages[g * pg + i][0, 0].astype(BF16) for i in range(pg)], axis=1)
               for g in groups]
    s = [(lax.dot_general(q_lat, c_all[g], (((1,), (1,)), ((), ())), preferred_element_type=F32)
          + jnp.dot(q_pe, kpt_all[g], preferred_element_type=F32)) * scale for g in groups]
    pad = jnp.zeros((page - ts, MLA_KVLORA), F32)
    cn = jnp.concatenate([ckvn_ref[...], pad], axis=0).astype(BF16)
    kn = jnp.concatenate([kpen_ref[...], pad[:, :MLA_ROPE]], axis=0).astype(BF16)
    sn = (lax.dot_general(q_lat, cn, (((1,), (1,)), ((), ())), preferred_element_type=F32)
          + lax.dot_general(q_pe, kn, (((1,), (1,)), ((), ())), preferred_element_type=F32)) * scale
    qpos = lax.broadcasted_iota(jnp.int32, (r, page), 0) & (ts - 1)
    kpos = lax.broadcasted_iota(jnp.int32, (r, page), 1)
    s.append(jnp.where(kpos <= qpos, sn, NEG_BIG))
    vals = c_all + [cn]
    parts = range(len(s))
    m = [jnp.max(s[g], axis=-1, keepdims=True) for g in parts]
    p = [jnp.exp(s[g] - m[g]) for g in parts]
    l = [jnp.sum(p[g], axis=-1, keepdims=True) for g in parts]
    acc = [jnp.dot(p[g].astype(BF16), vals[g], preferred_element_type=F32) for g in parts]
    m_all = m[0]
    for g in parts[1:]:
        m_all = jnp.maximum(m_all, m[g])
    w = [jnp.exp(m[g] - m_all) for g in parts]
    l_all = w[0] * l[0]
    acc_all = w[0] * acc[0]
    for g in parts[1:]:
        l_all = l_all + w[g] * l[g]
        acc_all = acc_all + w[g] * acc[g]
    o = _heads_to_lanes(acc_all / l_all, ts)
    y_ref[...] = _bdot(o, bdv_ref[...])


def _attn_sample(q_full, cache_ckv, cache_kpe, page_table, layer, ckv_new, kpe_new, bdv, nbat, ts):
    n_pages = page_table.shape[1]
    page = cache_ckv.shape[2]
    pg = math.gcd(n_pages, 8)
    scale = (HEAD_DIM + MLA_ROPE) ** -0.5
    kern = functools.partial(_attn_sample_kernel, n_pages=n_pages, pg=pg, ts=ts, page=page, scale=scale)

    def page_spec(i, rows, width):
        return pl.BlockSpec((1, 1, rows, width), lambda b, pt: (layer, pt[b * n_pages + i], 0, 0))

    in_specs = ([pl.BlockSpec((1, N_HEADS, ts, QK_W), lambda b, pt: (b, 0, 0, 0))]
                + [page_spec(i, page, MLA_KVLORA) for i in range(n_pages)]
                + [page_spec(i, MLA_ROPE, page) for i in range(n_pages)]
                + [pl.BlockSpec((ts, MLA_KVLORA), lambda b, pt: (b, 0)),
                   pl.BlockSpec((ts, MLA_ROPE), lambda b, pt: (b, 0)),
                   pl.BlockSpec((N_HEADS * LANE, BRANCH_W), lambda b, pt: (0, 0))])
    grid_spec = pltpu.PrefetchScalarGridSpec(
        num_scalar_prefetch=1,
        grid=(nbat,),
        in_specs=in_specs,
        out_specs=pl.BlockSpec((ts, BRANCH_W), lambda b, pt: (b, 0)),
    )
    return pl.pallas_call(
        kern,
        grid_spec=grid_spec,
        out_shape=jax.ShapeDtypeStruct((nbat * ts, BRANCH_W), F32),
        compiler_params=_cparams(1, 40),
        name="attn_sample",
    )(page_table.reshape(-1), q_full, *([cache_ckv] * n_pages), *([cache_kpe] * n_pages), ckv_new, kpe_new, bdv)


def _cumsum_rows(x, n):
    rows = lax.broadcasted_iota(jnp.int32, x.shape, 0)
    d = 1
    while d < n:
        x = x + jnp.where(rows >= d, pltpu.roll(x, d, axis=0), 0.0)
        d *= 2
    return x


def _inverse_masks(n):
    rows = lax.broadcasted_iota(jnp.int32, (n, n), 0)
    cols = lax.broadcasted_iota(jnp.int32, (n, n), 1)
    masks = [rows == cols, (rows >> 1) == (cols >> 1)]
    k = 1
    while (1 << k) < n:
        masks.append(((rows >> k) ^ (cols >> k)) == 1)
        k += 1
    return masks


def _gdn_solve(q_raw, k_raw, v_raw, gcm, beta, tril, strict, masks):
    n = range(len(q_raw))
    eye = jnp.where(masks[0], 1.0, 0.0)
    q = [x * lax.rsqrt(jnp.sum(x * x, axis=-1, keepdims=True) + EPS) * (HEAD_DIM ** -0.5) for x in q_raw]
    k = [x * lax.rsqrt(jnp.sum(x * x, axis=-1, keepdims=True) + EPS) for x in k_raw]
    kb = [k[p] * beta[p] for p in n]
    decay = []
    for p in n:
        diff = gcm[p] - gcm[p].T
        decay.append(jnp.where(tril, jnp.exp(jnp.where(tril, diff, 0.0)), 0.0))
    kk = [_bdot_nt(kb[p], k[p]) for p in n]
    a_mat = [jnp.where(strict, kk[p] * decay[p], 0.0) for p in n]
    t = [eye - jnp.where(masks[1], a_mat[p], 0.0) for p in n]
    for m in masks[2:]:
        tl = [_bdot(t[p], jnp.where(m, a_mat[p], 0.0)) for p in n]
        t = [t[p] - _bdot(tl[p], t[p]) for p in n]
    egc = [jnp.exp(gcm[p]) for p in n]
    sol = [_bdot(t[p], jnp.concatenate([v_raw[p] * beta[p], kb[p] * egc[p]], axis=-1)) for p in n]
    qk = [_bdot_nt(q[p], k[p]) for p in n]
    intra = [jnp.where(tril, qk[p] * decay[p], 0.0) for p in n]
    qe = [q[p] * egc[p] for p in n]
    return sol, intra, qe, k


def _head_inputs(gab_ref, exp_ref):
    ab = jnp.dot(gab_ref[...], exp_ref[...], preferred_element_type=F32)
    return ab[:, :BRANCH_W], ab[:, BRANCH_W:]


def _gdn_short_kernel(qkv_ref, gab_ref, z_ref, cst_ref, s0_ref, w_ref, alog_ref, dtb_ref, gn_ref, exp_ref,
                      y_ref, sf_ref, csto_ref, *, nseq, ts):
    cc = nseq * ts
    n2 = 2 * cc
    npair = N_HEADS // 2
    c3 = qkv_ref.shape[1]
    c = BRANCH_W
    u3 = qkv_ref[...].astype(F32).reshape(nseq, ts, c3)
    conv = _silu(_causal_conv(u3, cst_ref[...], w_ref[...], GDN_K)).reshape(cc, c3)
    csto_ref[...] = u3
    ga, gb = _head_inputs(gab_ref, exp_ref)
    g3 = (-jnp.exp(alog_ref[...]) * _softplus(ga + dtb_ref[...])).reshape(nseq, ts, c)
    b_tile = jax.nn.sigmoid(gb)
    tok = lax.broadcasted_iota(jnp.int32, (nseq, ts, c), 1)
    d = 1
    while d < ts:
        g3 = g3 + jnp.where(tok >= d, pltpu.roll(g3, d, axis=1), 0.0)
        d *= 2
    gc = g3.reshape(cc, c)
    g_tot = g3[:, ts - 1:ts, :]
    gl = jnp.broadcast_to(g_tot, (nseq, ts, c)).reshape(cc, c)

    rows = lax.broadcasted_iota(jnp.int32, (n2, n2), 0)
    cols = lax.broadcasted_iota(jnp.int32, (n2, n2), 1)
    same_head = (rows >> 6) == (cols >> 6)
    same_seq = (rows >> 3) == (cols >> 3)
    tril = same_seq & (rows >= cols)
    strict = same_seq & (rows > cols)
    masks = _inverse_masks(n2)[:4]
    gn = gn_ref[...]
    pairs = range(npair)

    def ls(pr):
        return slice(pr * LANE, (pr + 1) * LANE)

    def stack(x):
        return jnp.where(same_head, jnp.concatenate([x, x], axis=0), 0.0)

    def twice(x):
        return jnp.concatenate([x, x], axis=0)

    erow = (lax.broadcasted_iota(jnp.int32, (n2, nseq * n2), 0) >> 3) & (nseq - 1)
    eblk = lax.broadcasted_iota(jnp.int32, (n2, nseq * n2), 1) >> 7
    emask = erow == eblk

    def expand(x):
        return jnp.where(emask, jnp.concatenate([x] * nseq, axis=1), 0.0)

    gcm = [stack(gc[:, ls(pr)]) for pr in pairs]
    glm = [stack(gl[:, ls(pr)]) for pr in pairs]
    beta = [twice(b_tile[:, ls(pr)]) for pr in pairs]
    sol, intra, qe, k = _gdn_solve([stack(conv[:, ls(pr)]) for pr in pairs],
                                   [stack(conv[:, c + pr * LANE:c + (pr + 1) * LANE]) for pr in pairs],
                                   [stack(conv[:, 2 * c + pr * LANE:2 * c + (pr + 1) * LANE]) for pr in pairs],
                                   gcm, beta, tril, strict, masks)
    kdec = [k[pr] * jnp.exp(glm[pr] - gcm[pr]) for pr in pairs]
    zero_blk = jnp.zeros((HEAD_DIM, HEAD_DIM), F32)
    s_old = []
    sdec = []
    for pr in pairs:
        blocks = []
        decs = []
        for b in range(nseq):
            top = jnp.concatenate([s0_ref[b, 2 * pr], zero_blk], axis=-1)
            bot = jnp.concatenate([zero_blk, s0_ref[b, 2 * pr + 1]], axis=-1)
            blocks += [top, bot]
            decs.append(jnp.broadcast_to(jnp.exp(g_tot[b, :, ls(pr)]), (n2, LANE)))
        s_old.append(jnp.concatenate(blocks, axis=0))
        sdec.append(jnp.concatenate(decs, axis=0))
    v_new = [sol[pr][:, :n2] - _bdot(expand(sol[pr][:, n2:]), s_old[pr]) for pr in pairs]
    o_st = [_bdot(expand(qe[pr]), s_old[pr]) for pr in pairs]
    o_in = [_bdot(intra[pr], v_new[pr]) for pr in pairs]
    s_up = [_bdot_tn(expand(kdec[pr]), v_new[pr]) for pr in pairs]
    for pr in pairs:
        s_new = s_old[pr] * sdec[pr] + s_up[pr]
        for b in range(nseq):
            sf_ref[b, 2 * pr] = s_new[b * n2:b * n2 + HEAD_DIM, :HEAD_DIM]
            sf_ref[b, 2 * pr + 1] = s_new[b * n2 + HEAD_DIM:(b + 1) * n2, HEAD_DIM:]
        o = o_st[pr] + o_in[pr]
        o = o * lax.rsqrt(jnp.sum(o * o, axis=-1, keepdims=True) * (1.0 / HEAD_DIM) + EPS)
        o = o[:cc] + o[cc:]
        y_ref[:, ls(pr)] = o * gn[:, ls(pr)] * _silu(z_ref[:, ls(pr)].astype(F32))


def _gdn_short(proj, cst8, s0, w, alog_b, dtb_b, gn_b, expand, nbat, ts):
    nseq = GDN_CHUNK // ts
    assert ts == SUBLANE and nbat % nseq == 0
    c = BRANCH_W
    c3 = 3 * c
    tm = nseq * ts
    kern = functools.partial(_gdn_short_kernel, nseq=nseq, ts=ts)
    s_spec = pl.BlockSpec((nseq, N_HEADS, HEAD_DIM, HEAD_DIM), lambda i: (i, 0, 0, 0))
    cst_spec = pl.BlockSpec((nseq, SUBLANE, c3), lambda i: (i, 0, 0))

    def tok(width, off):
        return pl.BlockSpec((tm, width), lambda i: (i, off // width))

    return pl.pallas_call(
        kern,
        grid=(nbat // nseq,),
        in_specs=[tok(c3, OFF_GQKV), tok(LANE, OFF_GAB), tok(c, OFF_GZ), cst_spec, s_spec,
                  _const_spec((GDN_K, c3)), _const_spec((1, c)), _const_spec((1, c)), _const_spec((1, c)),
                  _const_spec((LANE, 2 * c))],
        out_specs=[pl.BlockSpec((tm, c), lambda i: (i, 0)), s_spec, cst_spec],
        out_shape=[jax.ShapeDtypeStruct((nbat * ts, c), F32),
                   jax.ShapeDtypeStruct((nbat, N_HEADS, HEAD_DIM, HEAD_DIM), F32),
                   jax.ShapeDtypeStruct((nbat, SUBLANE, c3), F32)],
        compiler_params=_cparams(1, 40),
        name="gdn_short",
    )(proj, proj, proj, cst8, s0, w, alog_b, dtb_b, gn_b, expand)


def _gdn_kernel(qkv_ref, gab_ref, z_ref, cst_ref, s0_ref, w_ref, alog_ref, dtb_ref, gn_ref, exp_ref,
                y_ref, sf_ref, csto_ref, s_scr, c_scr, qkv_s, g_ref, beta_ref, *, n_chunks):
    tc = qkv_ref.shape[0]
    c3 = qkv_ref.shape[1]
    ga, gb = _head_inputs(gab_ref, exp_ref)
    c8 = _carry_in(pl.program_id(1), cst_ref, c_scr)
    u3 = qkv_ref[...].astype(F32).reshape(1, tc, c3)
    qkv_s[...] = _silu(_causal_conv(u3, c8, w_ref[...], GDN_K)).reshape(tc, c3)
    g_ref[...] = -jnp.exp(alog_ref[...]) * _softplus(ga + dtb_ref[...])
    beta_ref[...] = jax.nn.sigmoid(gb)
    hist = u3[:, tc - SUBLANE:, :]
    c_scr[...] = hist
    csto_ref[...] = hist
    q_ref = qkv_s.at[:, 0:BRANCH_W]
    k_ref = qkv_s.at[:, BRANCH_W:2 * BRANCH_W]
    v_ref = qkv_s.at[:, 2 * BRANCH_W:3 * BRANCH_W]

    cc = GDN_CHUNK
    n2 = 2 * cc
    npair = N_HEADS // 2
    zero_blk = jnp.zeros((HEAD_DIM, HEAD_DIM), F32)

    @pl.when(pl.program_id(1) == 0)
    def _():
        for pr in range(npair):
            top = jnp.concatenate([s0_ref[0, 2 * pr], zero_blk], axis=-1)
            bot = jnp.concatenate([zero_blk, s0_ref[0, 2 * pr + 1]], axis=-1)
            s_scr[pr] = jnp.concatenate([top, bot], axis=0)

    rows = lax.broadcasted_iota(jnp.int32, (n2, n2), 0)
    cols = lax.broadcasted_iota(jnp.int32, (n2, n2), 1)
    same = (rows >> 6) == (cols >> 6)
    tril = same & (rows >= cols)
    strict = same & (rows > cols)
    masks = _inverse_masks(n2)[:7]
    gn = gn_ref[...]
    probs = [(ci, pr) for ci in range(n_chunks) for pr in range(npair)]

    def rs(ci):
        return slice(ci * cc, (ci + 1) * cc)

    def ls(pr):
        return slice(pr * LANE, (pr + 1) * LANE)

    def stack(x):
        return jnp.where(same, jnp.concatenate([x, x], axis=0), 0.0)

    def twice(x):
        return jnp.concatenate([x, x], axis=0)

    gc_all = [_cumsum_rows(g_ref[rs(ci), :], cc) for ci in range(n_chunks)]
    gcm = [stack(gc_all[ci][:, ls(pr)]) for ci, pr in probs]
    beta = [twice(beta_ref[rs(ci), ls(pr)]) for ci, pr in probs]
    sol, intra, qe, k = _gdn_solve([stack(q_ref[rs(ci), ls(pr)]) for ci, pr in probs],
                                   [stack(k_ref[rs(ci), ls(pr)]) for ci, pr in probs],
                                   [stack(v_ref[rs(ci), ls(pr)]) for ci, pr in probs],
                                   gcm, beta, tril, strict, masks)
    g_last = [gc_all[ci][cc - 1:cc, ls(pr)] for ci, pr in probs]
    kdec = [k[p] * jnp.exp(g_last[p] - gcm[p]) for p in range(len(probs))]
    sdec = [jnp.exp(g_last[p]) for p in range(len(probs))]

    for ci in range(n_chunks):
        ps = [ci * npair + pr for pr in range(npair)]
        s_old = [s_scr[pr] for pr in range(npair)]
        v_new = [sol[p][:, :n2] - _bdot(sol[p][:, n2:], s_old[pr]) for pr, p in enumerate(ps)]
        o_st = [_bdot(qe[p], s_old[pr]) for pr, p in enumerate(ps)]
        o_in = [_bdot(intra[p], v_new[pr]) for pr, p in enumerate(ps)]
        s_up = [_bdot_tn(kdec[p], v_new[pr]) for pr, p in enumerate(ps)]
        for pr, p in enumerate(ps):
            s_scr[pr] = s_old[pr] * sdec[p] + s_up[pr]
            o = o_st[pr] + o_in[pr]
            o = o * lax.rsqrt(jnp.sum(o * o, axis=-1, keepdims=True) * (1.0 / HEAD_DIM) + EPS)
            o = o[:cc] + o[cc:]
            y_ref[rs(ci), ls(pr)] = o * gn[:, ls(pr)] * _silu(z_ref[rs(ci), ls(pr)].astype(F32))

    @pl.when(pl.program_id(1) == pl.num_programs(1) - 1)
    def _():
        for pr in range(npair):
            s_pair = s_scr[pr]
            sf_ref[0, 2 * pr] = s_pair[:HEAD_DIM, :HEAD_DIM]
            sf_ref[0, 2 * pr + 1] = s_pair[HEAD_DIM:, HEAD_DIM:]


def _gdn(proj, cst8, s0, w, alog_b, dtb_b, gn_b, expand, nbat, t, tc):
    tper = t // tc
    c = BRANCH_W
    c3 = 3 * c
    kern = functools.partial(_gdn_kernel, n_chunks=tc // GDN_CHUNK)
    s_spec = pl.BlockSpec((1, N_HEADS, HEAD_DIM, HEAD_DIM), lambda b, t_: (b, 0, 0, 0))
    cst_spec = pl.BlockSpec((1, SUBLANE, c3), lambda b, t_: (b, 0, 0))
    return pl.pallas_call(
        kern,
        grid=(nbat, tper),
        in_specs=[_tok_spec(tc, c3, OFF_GQKV // c3, tper), _tok_spec(tc, LANE, OFF_GAB // LANE, tper),
                  _tok_spec(tc, c, OFF_GZ // c, tper),
                  cst_spec, s_spec, _const_spec((GDN_K, c3)), _const_spec((1, c)), _const_spec((1, c)),
                  _const_spec((1, c)), _const_spec((LANE, 2 * c))],
        out_specs=[_tok_spec(tc, c, 0, tper), s_spec, cst_spec],
        out_shape=[jax.ShapeDtypeStruct((nbat * t, c), F32),
                   jax.ShapeDtypeStruct((nbat, N_HEADS, HEAD_DIM, HEAD_DIM), F32),
                   jax.ShapeDtypeStruct((nbat, SUBLANE, c3), F32)],
        scratch_shapes=[pltpu.VMEM((N_HEADS // 2, 2 * HEAD_DIM, 2 * HEAD_DIM), F32),
                        pltpu.VMEM((1, SUBLANE, c3), F32), pltpu.VMEM((tc, c3), F32),
                        pltpu.VMEM((tc, c), F32), pltpu.VMEM((tc, c), F32)],
        compiler_params=_cparams(2, 40),
        name="gdn_rule",
    )(proj, proj, proj, cst8, s0, w, alog_b, dtb_b, gn_b, expand)


def _merge_kernel(x_ref, mod_ref, g_ref, bg_ref, cg_ref, xt_ref, st_ref, wsc_ref, yb_ref, yc_ref, yd_ref,
                  wg_ref, wbo_ref, wmix_ref, o_ref, sto_ref, c_scr):
    x = x_ref[...]
    nb, tt, d = x.shape
    c = bg_ref.shape[-1]
    mod = mod_ref[...]
    h = _modulated_norm(x, g_ref[...], mod, 0, 1).reshape(nb * tt, d).astype(BF16)
    c8 = _carry_in(pl.program_id(1), st_ref, c_scr)
    u3 = (cg_ref[...].astype(F32) * xt_ref[...].astype(F32)).reshape(nb, tt, c)
    y_a = bg_ref[...].astype(F32) * _causal_conv(u3, c8, wsc_ref[...], SC_K).reshape(nb * tt, c)
    hist = u3[:, tt - SUBLANE:, :]
    c_scr[...] = hist
    sto_ref[...] = hist
    acc = None
    for n, y in enumerate((y_a, yb_ref, yc_ref, yd_ref)):
        gate = jax.nn.sigmoid(jnp.dot(h, wg_ref[:, n * d:(n + 1) * d], preferred_element_type=F32))
        term = gate * _bdot(y if n == 0 else y[...], wbo_ref[n])
        acc = term if acc is None else acc + term
    mix = _bdot(acc, wmix_ref[...])
    o_ref[...] = x + mod[:, 2:3, :] * mix.reshape(nb, tt, d)


def _merge(x3, mod, g, proj, st8, wsc, ys, wg, wbo, wmix, nb, tt):
    nbat, t, d = x3.shape
    tper = t // tt
    tm = nb * tt
    c = BRANCH_W
    x_spec = pl.BlockSpec((nb, tt, d), lambda b, t_: (b, t_, 0))
    return pl.pallas_call(
        _merge_kernel,
        grid=(nbat // nb, tper),
        in_specs=[x_spec, pl.BlockSpec((nb, 6, d), lambda b, t_: (b, 0, 0)), _const_spec((1, d)),
                  _tok_spec(tm, c, OFF_SC // c, tper), _tok_spec(tm, c, OFF_SC // c + 1, tper),
                  _tok_spec(tm, c, OFF_SC // c + 2, tper), _state_spec(nb, c), _const_spec((SC_K, c))]
                 + [_tok_spec(tm, c, 0, tper)] * 3
                 + [_const_spec(wg.shape), _const_spec(wbo.shape), _const_spec(wmix.shape)],
        out_specs=[x_spec, _state_spec(nb, c)],
        out_shape=[jax.ShapeDtypeStruct(x3.shape, F32), jax.ShapeDtypeStruct((nbat, SUBLANE, c), F32)],
        scratch_shapes=[pltpu.VMEM((nb, SUBLANE, c), F32)],
        compiler_params=_cparams(2, 48),
        name="merge",
    )(x3, mod, g, proj, proj, proj, st8, wsc, *ys, wg, wbo, wmix)


def _ffn_kernel(x_ref, mod_ref, g_ref, win_ref, wout_ref, gf_ref, o_ref, *, final):
    x = x_ref[...]
    nb, tt, d = x.shape
    mod = mod_ref[...]
    fh = wout_ref.shape[0]
    h = _modulated_norm(x, g_ref[...], mod, 3, 4).reshape(nb * tt, d).astype(BF16)
    gate = jnp.dot(h, win_ref[:, :fh], preferred_element_type=F32)
    up = jnp.dot(h, win_ref[:, fh:], preferred_element_type=F32)
    out = _bdot(_silu(gate) * up, wout_ref[...])
    xo = x + mod[:, 5:6, :] * out.reshape(nb, tt, d)
    o_ref[...] = _rms(xo, gf_ref[...]) if final else xo


def _ffn(x3, mod, g, win, wout, gf, nb, tt, final):
    nbat, t, d = x3.shape
    x_spec = pl.BlockSpec((nb, tt, d), lambda b, t_: (b, t_, 0))
    kern = functools.partial(_ffn_kernel, final=final)
    return pl.pallas_call(
        kern,
        grid=(nbat // nb, t // tt),
        in_specs=[x_spec, pl.BlockSpec((nb, 6, d), lambda b, t_: (b, 0, 0)), _const_spec((1, d)),
                  _const_spec(win.shape), _const_spec(wout.shape), _const_spec((1, d))],
        out_specs=x_spec,
        out_shape=jax.ShapeDtypeStruct(x3.shape, F32),
        compiler_params=_cparams(2, 56),
        name="ffn",
    )(x3, mod, g, win, wout, gf)


def _block_diag(blocks):
    h, r, c = blocks.shape
    eye = jnp.eye(h, dtype=blocks.dtype)
    return jnp.einsum('hrc,hg->hrgc', blocks, eye).reshape(h * r, h * c)


def _prep_layer_weights(w_in, w_qb, w_kvb, w_lru_gate_a, w_lru_gate_x):
    d = w_in.shape[0]
    c = BRANCH_W
    o = 0
    segs = {}
    for name, width in (('sc', 3 * c), ('qa', MLA_QLORA), ('ckv', MLA_KVLORA), ('kpe', MLA_ROPE), ('gqkv', 3 * c),
                        ('gz', c), ('ga', N_HEADS), ('gb', N_HEADS), ('lx', c), ('lg', c)):
        segs[name] = w_in[:, o:o + width]
        o += width
    half = MLA_ROPE // 2
    zpad = jnp.zeros((d, LANE - MLA_ROPE), F32)
    kpe = segs['kpe']
    kpa = jnp.concatenate([kpe, zpad], axis=1)
    kpb = jnp.concatenate([-kpe[:, half:], kpe[:, :half], zpad], axis=1)
    w_proj = jnp.concatenate(
        [segs['sc'], segs['gqkv'], segs['gz'], segs['lx'], segs['lg'], segs['qa'], segs['ckv'], kpa, kpb,
         segs['ga'], segs['gb'], jnp.zeros((d, LANE - 2 * N_HEADS), F32)],
        axis=1).astype(BF16)

    ql = w_qb.shape[0]
    wq = w_qb.reshape(ql, N_HEADS, HEAD_DIM + MLA_ROPE)
    wn = wq[:, :, :HEAD_DIM].reshape(ql, c)
    pe = wq[:, :, HEAD_DIM:]
    zq = jnp.zeros((ql, N_HEADS, LANE - MLA_ROPE), F32)
    wpa = jnp.concatenate([pe, zq], axis=2).reshape(ql, N_HEADS * LANE)
    wpb = jnp.concatenate([-pe[:, :, half:], pe[:, :, :half], zq], axis=2).reshape(ql, N_HEADS * LANE)
    w_kb = w_kvb[:, :, :HEAD_DIM]
    w_vb = w_kvb[:, :, HEAD_DIM:]
    bdk = _block_diag(jnp.transpose(w_kb, (1, 2, 0)))
    bdv = _block_diag(jnp.transpose(w_vb, (1, 0, 2)))
    wlg = jnp.concatenate([_block_diag(w_lru_gate_a), _block_diag(w_lru_gate_x)], axis=1)
    wvt = jnp.transpose(w_vb, (1, 2, 0))
    return dict(w_proj=w_proj, wn=wn.astype(BF16), wpa=wpa.astype(BF16), wpb=wpb.astype(BF16),
                bdk=bdk.astype(BF16), bdv=bdv.astype(BF16), wvt=wvt.astype(BF16), wlg=wlg.astype(BF16))


def _rope_tables(pos):
    half = MLA_ROPE // 2
    inv = ROPE_THETA ** (-jnp.arange(half, dtype=F32) / half)
    ang = pos.astype(F32)[:, None] * inv[None, :]
    pad = jnp.zeros((pos.shape[0], LANE - MLA_ROPE), F32)
    cos = jnp.concatenate([jnp.cos(ang), jnp.cos(ang), pad], axis=1)
    sin = jnp.concatenate([jnp.sin(ang), jnp.sin(ang), pad], axis=1)
    return cos, sin


def _pad_state(st, k_w):
    return jnp.pad(st, ((0, 0), (SUBLANE - (k_w - 1), 0), (0, 0)))


def _tile(t, pref):
    tt = min(t, pref)
    while t % tt:
        tt //= 2
    return tt


def _group_layer(x3, mod, lw, p, st, rope, attend, cfg):
    nbat, t, d = x3.shape
    nb, tt = cfg['nb'], cfg['tt']
    proj = _inproj(x3, mod, p['g_norm_mix'], lw['w_proj'], cfg['nb_proj'], cfg['tt_proj'])
    q_full, ckv, kpe, kf, *vt = _mla_pre(proj, rope[0], rope[1], p['g_q_norm'], p['g_kv_norm'], lw['wn'], lw['wpa'],
                                         lw['wpb'], lw['bdk'], nbat, t, nb, tt, cfg['q_dtype'], cfg['emit_vt'])
    y_b = attend(q_full, ckv, kpe, kf, vt, lw)
    gdn_w = (p['w_gdn_conv'], p['alog_b'], p['dtb_b'], p['gn_b'], p['head_expand'])
    if t < GDN_CHUNK:
        y_c, s_gdn, gc8 = _gdn_short(proj, st['gdn_conv'], st['gdn'], *gdn_w, nbat, t)
    else:
        y_c, s_gdn, gc8 = _gdn(proj, st['gdn_conv'], st['gdn'], *gdn_w, nbat, t, cfg['tc'])
    y_d, lc8, h_last = _lru(proj, st['lru_conv'], st['lru'], p['w_lru_conv'], p['b_lru_conv'], lw['wlg'],
                            p['b_lru_gates'], p['nsl'], nbat, t, cfg['nb_scan'], cfg['tt_scan'], cfg['pos0'])
    x1, sc8 = _merge(x3, mod, p['g_norm_mix'], proj, st['sconv'], p['w_sc_conv'], (y_b, y_c, y_d),
                     p['w_merge_gate'], p['w_branch_out'], p['w_mix_out'], cfg['nb_mm'], cfg['tt_mm'])
    x2 = _ffn(x1, mod, p['g_norm_ffn'], p['w_ffn_in'], p['w_ffn_out'], p['g_final'], cfg['nb_mm'], cfg['tt_mm'],
              cfg['final'])
    new_st = dict(ckv=ckv.reshape(nbat, t, -1), kpe=kpe.reshape(nbat, t, -1),
                  sconv=sc8[:, SUBLANE - (SC_K - 1):], gdn_conv=gc8[:, SUBLANE - (GDN_K - 1):], gdn=s_gdn,
                  lru_conv=lc8[:, SUBLANE - (LRU_K - 1):], lru=h_last[:, 0, :])
    return x2, new_st


STATE_KEYS = ('ckv', 'kpe', 'sconv', 'gdn_conv', 'gdn', 'lru_conv', 'lru')


def kernel(x_prompt, x_sample, c_prompt, c_sample, cache_mla_ckv, cache_mla_kpe, page_table, state_sconv, state_gdn_conv, state_gdn, state_lru_conv, state_lru, w_ada, b_ada, g_norm_mix, g_norm_ffn, w_in, w_sc_conv, g_q_norm, w_qb, g_kv_norm, w_kvb, w_gdn_conv, gdn_a_log, gdn_dt_bias, g_gdn_norm, w_lru_conv, b_lru_conv, w_lru_gate_a, b_lru_gate_a, w_lru_gate_x, b_lru_gate_x, lru_lambda, w_branch_out, w_merge_gate, w_mix_out, w_ffn_in, w_ffn_out, g_final):
    bp, tp, d = x_prompt.shape
    bs, ts, _ = x_sample.shape
    depth = w_in.shape[0]
    n_pages = page_table.shape[1]
    past_len = n_pages * cache_mla_ckv.shape[2]
    c = BRANCH_W
    assert ts == SUBLANE and tp % GDN_CHUNK == 0 and d % LANE == 0

    mod_all = _ada(jnp.concatenate([c_prompt, c_sample], axis=0), w_ada, b_ada).reshape(depth, bp + bs, 6, d)

    nb_s = _tile(bs, 32)
    cos_p, sin_p = _rope_tables(jnp.arange(tp, dtype=jnp.int32))
    cos_s, sin_s = _rope_tables(past_len + jnp.arange(ts, dtype=jnp.int32))
    rope_p = (cos_p, sin_p)
    rope_s = (jnp.tile(cos_s, (nb_s, 1)), jnp.tile(sin_s, (nb_s, 1)))

    tq = _tile(tp, 256)
    tk = _tile(tp, 256)
    cfg_p = dict(nb=1, tt=tk, nb_proj=1, tt_proj=_tile(tp, 2048), q_dtype=BF16, emit_vt=True,
                 tc=_tile(tp, 256), pos0=0, nb_scan=1, tt_scan=_tile(tp, 256), nb_mm=1, tt_mm=_tile(tp, 512))
    cfg_s = dict(nb=nb_s, tt=ts, nb_proj=_tile(bs, 128), tt_proj=ts, q_dtype=F32, emit_vt=False,
                 tc=GDN_CHUNK, pos0=past_len, nb_scan=nb_s, tt_scan=ts, nb_mm=_tile(bs, 64), tt_mm=ts)
    cache_kpe_t = jnp.swapaxes(cache_mla_kpe, 2, 3)
    head_expand = (jnp.arange(LANE)[:, None] == jnp.arange(2 * c)[None, :] // HEAD_DIM).astype(BF16)

    xp, xs = x_prompt, x_sample
    out_p = {k: [] for k in STATE_KEYS}
    out_s = {k: [] for k in STATE_KEYS}
    for l in range(depth):
        lw = _prep_layer_weights(w_in[l], w_qb[l], w_kvb[l], w_lru_gate_a[l], w_lru_gate_x[l])
        p = dict(
            g_norm_mix=g_norm_mix[l][None], g_norm_ffn=g_norm_ffn[l][None], g_final=g_final[None],
            w_sc_conv=w_sc_conv[l], g_q_norm=g_q_norm[l][None], g_kv_norm=g_kv_norm[l][None],
            w_gdn_conv=w_gdn_conv[l],
            alog_b=jnp.repeat(gdn_a_log[l], HEAD_DIM)[None], dtb_b=jnp.repeat(gdn_dt_bias[l], HEAD_DIM)[None],
            gn_b=jnp.tile(g_gdn_norm[l], N_HEADS)[None], head_expand=head_expand,
            w_lru_conv=w_lru_conv[l], b_lru_conv=b_lru_conv[l][None],
            b_lru_gates=jnp.concatenate([b_lru_gate_a[l], b_lru_gate_x[l]])[None],
            nsl=lru_lambda[l][None],
            w_merge_gate=w_merge_gate[l].astype(BF16), w_branch_out=w_branch_out[l].astype(BF16),
            w_mix_out=w_mix_out[l].astype(BF16), w_ffn_in=w_ffn_in[l].astype(BF16),
            w_ffn_out=w_ffn_out[l].astype(BF16))
        final = l == depth - 1

        st_p = dict(sconv=jnp.zeros((bp, SUBLANE, c), F32), gdn_conv=jnp.zeros((bp, SUBLANE, 3 * c), F32),
                    gdn=jnp.zeros((bp, N_HEADS, HEAD_DIM, HEAD_DIM), F32),
                    lru_conv=jnp.zeros((bp, SUBLANE, c), F32), lru=jnp.zeros((bp, 1, c), F32))

        def attend_p(q_full, ckv, kpe, kf, vt, lw_):
            return _attn_prompt(q_full, kf, vt[0], lw_['wvt'], bp, tp, tq, tk)

        xp, nst_p = _group_layer(xp, mod_all[l, :bp], lw, p, st_p, rope_p, attend_p, dict(cfg_p, final=final))

        st_s = dict(sconv=_pad_state(state_sconv[l], SC_K), gdn_conv=_pad_state(state_gdn_conv[l], GDN_K),
                    gdn=state_gdn[l], lru_conv=_pad_state(state_lru_conv[l], LRU_K), lru=state_lru[l][:, None, :])

        def attend_s(q_full, ckv, kpe, kf, vt, lw_, layer=l):
            return _attn_sample(q_full, cache_mla_ckv, cache_kpe_t, page_table, layer, ckv, kpe, lw_['bdv'], bs, ts)

        xs, nst_s = _group_layer(xs, mod_all[l, bp:], lw, p, st_s, rope_s, attend_s, dict(cfg_s, final=final))
        for k in STATE_KEYS:
            out_p[k].append(nst_p[k])
            out_s[k].append(nst_s[k])

    return ((xp, xs) + tuple(jnp.stack(out_p[k]) for k in STATE_KEYS)
            + tuple(jnp.stack(out_s[k]) for k in STATE_KEYS))
```

```python
import functools
import math

import jax
import jax.numpy as jnp
from jax import lax
from jax.experimental import pallas as pl
from jax.experimental.pallas import tpu as pltpu

F32 = jnp.float32
BF16 = jnp.bfloat16

HEAD_DIM = 64
N_HEADS = 8
BRANCH_W = N_HEADS * HEAD_DIM
MLA_ROPE = HEAD_DIM // 2
MLA_QLORA = 256
MLA_KVLORA = 128
ROPE_THETA = 10000.0
SC_K = 3
GDN_K = 4
LRU_K = 4
LRU_C = 8.0
GDN_CHUNK = 64
EPS = 1e-6
NEG_BIG = -1e30
LANE = 128
SUBLANE = 8
QK_W = 2 * LANE
ATTN_GROUP_W = 256

OFF_SC = 0
OFF_GQKV = 1536
OFF_GZ = 3072
OFF_LX = 3584
OFF_LG = 4096
OFF_QA = 4608
OFF_CKV = 4864
OFF_KPA = 4992
OFF_KPB = 5120
OFF_GAB = 5248
PROJ_W = 5376
PROJ_TN = 896


def _cparams(n_axes, vmem_mib):
    return pltpu.CompilerParams(dimension_semantics=("arbitrary",) * n_axes,
                                vmem_limit_bytes=vmem_mib * 1024 * 1024)


def _const_spec(shape):
    nd = len(shape)
    return pl.BlockSpec(shape, lambda *_: (0,) * nd, pipeline_mode=pl.Buffered(1))


def _bdot(a, b):
    return jnp.dot(a.astype(BF16), b.astype(BF16), preferred_element_type=F32)


def _bdot_nt(a, b):
    return lax.dot_general(a.astype(BF16), b.astype(BF16), (((1,), (1,)), ((), ())),
                           preferred_element_type=F32)


def _bdot_tn(a, b):
    return lax.dot_general(a.astype(BF16), b.astype(BF16), (((0,), (0,)), ((), ())),
                           preferred_element_type=F32)


def _silu(x):
    return x * jax.nn.sigmoid(x)


def _softplus(x):
    return jnp.maximum(x, 0.0) + jnp.log(1.0 + jnp.exp(-jnp.abs(x)))


def _gelu_tanh(x):
    return 0.5 * x * (1.0 + jnp.tanh(math.sqrt(2.0 / math.pi) * (x + 0.044715 * (x * x * x))))


def _rms(x, g):
    return x * lax.rsqrt(jnp.mean(x * x, axis=-1, keepdims=True) + EPS) * g


def _modulated_norm(x3, g, mod, shift_row, scale_row):
    y = _rms(x3, g)
    return y * (1.0 + mod[:, scale_row:scale_row + 1, :]) + mod[:, shift_row:shift_row + 1, :]


def _ada_kernel(c_ref, w_ref, b_ref, o_ref):
    o_ref[0] = _bdot(_silu(c_ref[...]), w_ref[0]) + b_ref[0]


def _ada(c_all, w_ada, b_ada):
    depth, d, n = w_ada.shape
    nb = c_all.shape[0]
    tn = n // 4
    return pl.pallas_call(
        _ada_kernel,
        grid=(depth, n // tn),
        in_specs=[pl.BlockSpec((nb, d), lambda l, j: (0, 0)),
                  pl.BlockSpec((1, d, tn), lambda l, j: (l, 0, j)),
                  pl.BlockSpec((1, 1, tn), lambda l, j: (l, 0, j))],
        out_specs=pl.BlockSpec((1, nb, tn), lambda l, j: (l, 0, j)),
        out_shape=jax.ShapeDtypeStruct((depth, nb, n), F32),
        compiler_params=_cparams(2, 40),
        name="ada_mod",
    )(c_all, w_ada, b_ada.reshape(depth, 1, n))


def _inproj_kernel(x_ref, mod_ref, g_ref, w_ref, o_ref, h_scr):
    @pl.when(pl.program_id(1) == 0)
    def _():
        h = _modulated_norm(x_ref[...], g_ref[...], mod_ref[...], 0, 1)
        h_scr[...] = h.reshape(h_scr.shape).astype(BF16)

    o_ref[...] = jnp.dot(h_scr[...], w_ref[...], preferred_element_type=F32).astype(o_ref.dtype)


def _inproj(x3, mod, g, w, nb, tt):
    nbat, t, d = x3.shape
    tper = t // tt
    tm = nb * tt
    n_m = (nbat // nb) * tper
    return pl.pallas_call(
        _inproj_kernel,
        grid=(n_m, PROJ_W // PROJ_TN),
        in_specs=[pl.BlockSpec((nb, tt, d), lambda i, j: (i // tper, i % tper, 0)),
                  pl.BlockSpec((nb, 6, d), lambda i, j: (i // tper, 0, 0)),
                  _const_spec((1, d)),
                  pl.BlockSpec((d, PROJ_TN), lambda i, j: (0, j))],
        out_specs=pl.BlockSpec((tm, PROJ_TN), lambda i, j: (i, j)),
        out_shape=jax.ShapeDtypeStruct((nbat * t, PROJ_W), BF16),
        scratch_shapes=[pltpu.VMEM((tm, d), BF16)],
        compiler_params=_cparams(2, 40),
        name="in_proj",
    )(x3, mod, g, w)


def _causal_conv(u3, c8, w, k_w):
    nb, tt, c = u3.shape
    rows = lax.broadcasted_iota(jnp.int32, (nb, SUBLANE, c), 1)
    acc = None
    for j in range(k_w):
        s = k_w - 1 - j
        if s == 0:
            sh = u3
        else:
            full = pltpu.roll(u3, s, axis=1)
            top = jnp.where(rows < s, pltpu.roll(c8, s, axis=1), full[:, 0:SUBLANE, :])
            sh = top if tt == SUBLANE else jnp.concatenate([top, full[:, SUBLANE:, :]], axis=1)
        term = sh * w[j:j + 1, :]
        acc = term if acc is None else acc + term
    return acc


def _carry_in(t_idx, st_ref, c_scr):
    @pl.when(t_idx == 0)
    def _():
        c_scr[...] = st_ref[...]

    return c_scr[...]


def _tok_spec(tm, width, col_block, tper):
    return pl.BlockSpec((tm, width), lambda b, t: (b * tper + t, col_block))


def _state_spec(nb, c):
    return pl.BlockSpec((nb, SUBLANE, c), lambda b, t: (b, 0, 0))


def _lru_branch(lx_ref, lg_ref, c8, h_in, w_ref, cb_ref, wg_ref, bg_ref, lam_ref, nb, tt, pos):
    c = lx_ref.shape[-1]
    u3 = lx_ref[...].astype(F32).reshape(nb, tt, c)
    u = (_causal_conv(u3, c8, w_ref[...], LRU_K) + cb_ref[...]).reshape(nb * tt, c)
    hist = u3[:, tt - SUBLANE:, :]
    gates = jax.nn.sigmoid(_bdot(u, wg_ref[...]) + bg_ref[...])
    r = gates[:, :c]
    ig = gates[:, c:]
    log_a = -LRU_C * r * _softplus(-lam_ref[...])
    rows = lax.broadcasted_iota(jnp.int32, (nb, tt, c), 1)
    mult = jnp.sqrt(1.0 - jnp.exp(2.0 * log_a)).reshape(nb, tt, c)
    mult = jnp.where(pos + rows == 0, 1.0, mult)
    grp = tt // SUBLANE
    a = jnp.exp(log_a).reshape(nb * grp, SUBLANE, c)
    b = (mult * (ig * u).reshape(nb, tt, c)).reshape(nb * grp, SUBLANE, c)
    tok = lax.broadcasted_iota(jnp.int32, (nb * grp, SUBLANE, c), 1)
    d = 1
    while d < SUBLANE:
        keep = tok >= d
        a_s = jnp.where(keep, pltpu.roll(a, d, axis=1), 1.0)
        b_s = jnp.where(keep, pltpu.roll(b, d, axis=1), 0.0)
        b = a * b_s + b
        a = a * a_s
        d *= 2
    a = a.reshape(nb, grp, SUBLANE, c)
    b = b.reshape(nb, grp, SUBLANE, c)
    parts = []
    for g in range(grp):
        part = b[:, g] + a[:, g] * h_in
        h_in = part[:, SUBLANE - 1:SUBLANE, :]
        parts.append(part)
    hs = parts[0] if grp == 1 else jnp.concatenate(parts, axis=1)
    return hs.reshape(nb * tt, c) * _gelu_tanh(lg_ref[...].astype(F32)), hist, h_in


def _mla_pre_kernel(qa_ref, ckv_ref, kpa_ref, kpb_ref, cos_ref, sin_ref, gq_ref, gkv_ref,
                    wn_ref, wpa_ref, wpb_ref, bdk_ref, q_out, ckv_out, kpe_out, kf_out, *rest, nb, tt):
    cos = cos_ref[...]
    sin = sin_ref[...]
    cq = _rms(qa_ref[...].astype(F32), gq_ref[...]).astype(BF16)
    qn = jnp.dot(cq, wn_ref[...], preferred_element_type=F32)
    qabs = _bdot(qn, bdk_ref[...])
    pa = jnp.dot(cq, wpa_ref[...], preferred_element_type=F32)
    pb = jnp.dot(cq, wpb_ref[...], preferred_element_type=F32)
    for h in range(N_HEADS):
        sl = slice(h * LANE, (h + 1) * LANE)
        q_pe = pa[:, sl] * cos + pb[:, sl] * sin
        q_out[:, h, :, 0:LANE] = qabs[:, sl].reshape(nb, tt, LANE).astype(q_out.dtype)
        q_out[:, h, :, LANE:QK_W] = q_pe.reshape(nb, tt, LANE).astype(q_out.dtype)
    ckv = _rms(ckv_ref[...].astype(F32), gkv_ref[...])
    kpe = kpa_ref[...].astype(F32) * cos + kpb_ref[...].astype(F32) * sin
    ckv_out[...] = ckv
    kpe_out[...] = kpe[:, :MLA_ROPE]
    kf_out[...] = jnp.concatenate([ckv, kpe], axis=-1).astype(BF16)
    if rest:
        rest[0][0] = ckv.T.astype(BF16)


def _mla_pre(proj, cos, sin, gq, gkv, wn, wpa, wpb, bdk, nbat, t, nb, tt, q_dtype, emit_vt):
    tper = t // tt
    tm = nb * tt
    hw = N_HEADS * LANE
    kern = functools.partial(_mla_pre_kernel, nb=nb, tt=tt)
    tab_spec = pl.BlockSpec((tm, LANE), lambda b, t_: (t_, 0))
    vt_specs, vt_shapes = [], []
    if emit_vt:
        assert nb == 1
        vt_specs = [pl.BlockSpec((1, MLA_KVLORA, tt), lambda b, t_: (b * tper + t_, 0, 0))]
        vt_shapes = [jax.ShapeDtypeStruct((nbat * tper, MLA_KVLORA, tt), BF16)]
    return pl.pallas_call(
        kern,
        grid=(nbat // nb, tper),
        in_specs=[_tok_spec(tm, MLA_QLORA, OFF_QA // MLA_QLORA, tper), _tok_spec(tm, LANE, OFF_CKV // LANE, tper),
                  _tok_spec(tm, LANE, OFF_KPA // LANE, tper), _tok_spec(tm, LANE, OFF_KPB // LANE, tper),
                  tab_spec, tab_spec, _const_spec((1, MLA_QLORA)), _const_spec((1, MLA_KVLORA)),
                  _const_spec((MLA_QLORA, BRANCH_W)), _const_spec((MLA_QLORA, hw)), _const_spec((MLA_QLORA, hw)),
                  _const_spec((BRANCH_W, hw))],
        out_specs=[pl.BlockSpec((nb, N_HEADS, tt, QK_W), lambda b, t_: (b, 0, t_, 0)),
                   _tok_spec(tm, MLA_KVLORA, 0, tper), _tok_spec(tm, MLA_ROPE, 0, tper), _tok_spec(tm, QK_W, 0, tper)]
                  + vt_specs,
        out_shape=[jax.ShapeDtypeStruct((nbat, N_HEADS, t, QK_W), q_dtype),
                   jax.ShapeDtypeStruct((nbat * t, MLA_KVLORA), F32),
                   jax.ShapeDtypeStruct((nbat * t, MLA_ROPE), F32),
                   jax.ShapeDtypeStruct((nbat * t, QK_W), BF16)] + vt_shapes,
        compiler_params=_cparams(2, 48),
        name="mla_pre",
    )(proj, proj, proj, proj, cos, sin, gq, gkv, wn, wpa, wpb, bdk)


def _heads_to_lanes(o, rows_per_head):
    return jnp.concatenate([o[h * rows_per_head:(h + 1) * rows_per_head] for h in range(N_HEADS)], axis=-1)


def _attn_prompt_kernel(q_ref, kf_ref, vt_ref, wvt_ref, y_ref, m_scr, l_scr, acc_scr, s_scr, *, tq, tk, scale):
    i = pl.program_id(1)
    n_full = (i * tq) // tk
    hpg = ATTN_GROUP_W // tq
    w2 = hpg * tq
    krow = lax.broadcasted_iota(jnp.int32, (tk, w2), 0)
    qcol = i * tq + (lax.broadcasted_iota(jnp.int32, (tk, w2), 1) & (tq - 1))
    heads = range(N_HEADS // hpg)
    m_scr[...] = jnp.full(m_scr.shape, NEG_BIG, F32)
    l_scr[...] = jnp.zeros(l_scr.shape, F32)
    acc_scr[...] = jnp.zeros(acc_scr.shape, F32)

    def scores(j):
        kblk = kf_ref[pl.ds(pl.multiple_of(j * tk, tk), tk), :]
        return [lax.dot_general(kblk, q_ref[0, hpg * h:hpg * (h + 1)].reshape(w2, QK_W), (((1,), (1,)), ((), ())),
                                preferred_element_type=F32)
                for h in heads]

    c2 = scale * math.log2(math.e)

    def step(j, masked):
        vt = vt_ref[j]
        s = [s_scr[h] for h in heads]
        if masked:
            keep = j * tk + krow <= qcol
            s = [jnp.where(keep, s[h], NEG_BIG) for h in heads]
        else:
            s_next = scores(j + 1)
            for h in heads:
                s_scr[h] = s_next[h]
        m_old = [m_scr[h] for h in heads]
        m_new = [jnp.maximum(m_old[h], jnp.max(s[h], axis=0, keepdims=True)) for h in heads]
        p = [jnp.exp2((s[h] - m_new[h]) * c2) for h in heads]
        alpha = [jnp.exp2((m_old[h] - m_new[h]) * c2) for h in heads]
        pv = [jnp.dot(vt, p[h].astype(BF16), preferred_element_type=F32) for h in heads]
        for h in heads:
            m_scr[h] = m_new[h]
            l_scr[h] = alpha[h] * l_scr[h] + jnp.sum(p[h], axis=0, keepdims=True)
            acc_scr[h] = alpha[h] * acc_scr[h] + pv[h]

    def body(j, carry):
        step(j, False)
        return carry

    s_first = scores(0)
    for h in heads:
        s_scr[h] = s_first[h]
    lax.fori_loop(0, n_full, body, 0)
    step(n_full, True)
    o = [acc_scr[h] / l_scr[h] for h in heads]
    outs = [_bdot(wvt_ref[hh], o[hh // hpg][:, (hh % hpg) * tq:(hh % hpg + 1) * tq]) for hh in range(N_HEADS)]
    y_ref[...] = jnp.concatenate(outs, axis=0).T


def _attn_prompt(q_full, kf, vt, wvt, nbat, t, tq, tk):
    tper = t // tq
    kper = t // tk
    assert ATTN_GROUP_W % tq == 0
    ngrp = N_HEADS * tq // ATTN_GROUP_W
    scale = (HEAD_DIM + MLA_ROPE) ** -0.5
    kern = functools.partial(_attn_prompt_kernel, tq=tq, tk=tk, scale=scale)
    return pl.pallas_call(
        kern,
        grid=(nbat, tper),
        in_specs=[pl.BlockSpec((1, N_HEADS, tq, QK_W), lambda b, i: (b, 0, i, 0)),
                  pl.BlockSpec((t, QK_W), lambda b, i: (b, 0)),
                  pl.BlockSpec((kper, MLA_KVLORA, tk), lambda b, i: (b, 0, 0)),
                  _const_spec(wvt.shape)],
        out_specs=pl.BlockSpec((tq, BRANCH_W), lambda b, i: (b * tper + i, 0)),
        out_shape=jax.ShapeDtypeStruct((nbat * t, BRANCH_W), F32),
        scratch_shapes=[pltpu.VMEM((ngrp, 1, ATTN_GROUP_W), F32), pltpu.VMEM((ngrp, 1, ATTN_GROUP_W), F32),
                        pltpu.VMEM((ngrp, MLA_KVLORA, ATTN_GROUP_W), F32),
                        pltpu.VMEM((ngrp, tk, ATTN_GROUP_W), F32)],
        compiler_params=_cparams(2, 40),
        name="attn_prompt",
    )(q_full, kf, vt, wvt)


def _attn_sample_kernel(pt_ref, q_ref, *refs, n_pages, pg, ts, page, scale):
    ckv_pages = refs[:n_pages]
    kpe_pages = refs[n_pages:2 * n_pages]
    ckvn_ref, kpen_ref, bdv_ref, y_ref = refs[2 * n_pages:]
    r = N_HEADS * ts
    groups = range(n_pages // pg)

    q = q_ref[0].reshape(r, QK_W)
    q_lat = q[:, :MLA_KVLORA].astype(BF16)
    q_pe = q[:, MLA_KVLORA:MLA_KVLORA + MLA_ROPE].astype(BF16)

    c_all = [jnp.concatenate([ckv_pages[g * pg + i][0, 0].astype(BF16) for i in range(pg)], axis=0)
             for g in groups]
    kpt_all = [jnp.concatenate([kpe_pages[g * pg + i][0, 0].astype(BF16) for i in range(pg)], axis=1)
               for g in groups]
    s = [(lax.dot_general(q_lat, c_all[g], (((1,), (1,)), ((), ())), preferred_element_type=F32)
          + jnp.dot(q_pe, kpt_all[g], preferred_element_type=F32)) * scale for g in groups]
    pad = jnp.zeros((page - ts, MLA_KVLORA), F32)
    cn = jnp.concatenate([ckvn_ref[...], pad], axis=0).astype(BF16)
    kn = jnp.concatenate([kpen_ref[...], pad[:, :MLA_ROPE]], axis=0).astype(BF16)
    sn = (lax.dot_general(q_lat, cn, (((1,), (1,)), ((), ())), preferred_element_type=F32)
          + lax.dot_general(q_pe, kn, (((1,), (1,)), ((), ())), preferred_element_type=F32)) * scale
    qpos = lax.broadcasted_iota(jnp.int32, (r, page), 0) & (ts - 1)
    kpos = lax.broadcasted_iota(jnp.int32, (r, page), 1)
    s.append(jnp.where(kpos <= qpos, sn, NEG_BIG))
    vals = c_all + [cn]
    parts = range(len(s))
    m = [jnp.max(s[g], axis=-1, keepdims=True) for g in parts]
    p = [jnp.exp(s[g] - m[g]) for g in parts]
    l = [jnp.sum(p[g], axis=-1, keepdims=True) for g in parts]
    acc = [jnp.dot(p[g].astype(BF16), vals[g], preferred_element_type=F32) for g in parts]
    m_all = m[0]
    for g in parts[1:]:
        m_all = jnp.maximum(m_all, m[g])
    w = [jnp.exp(m[g] - m_all) for g in parts]
    l_all = w[0] * l[0]
    acc_all = w[0] * acc[0]
    for g in parts[1:]:
        l_all = l_all + w[g] * l[g]
        acc_all = acc_all + w[g] * acc[g]
    o = _heads_to_lanes(acc_all / l_all, ts)
    y_ref[...] = _bdot(o, bdv_ref[...])


def _attn_sample(q_full, cache_ckv, cache_kpe, page_table, layer, ckv_new, kpe_new, bdv, nbat, ts):
    n_pages = page_table.shape[1]
    page = cache_ckv.shape[2]
    pg = math.gcd(n_pages, 8)
    scale = (HEAD_DIM + MLA_ROPE) ** -0.5
    kern = functools.partial(_attn_sample_kernel, n_pages=n_pages, pg=pg, ts=ts, page=page, scale=scale)

    def page_spec(i, rows, width):
        return pl.BlockSpec((1, 1, rows, width), lambda b, pt: (layer, pt[b * n_pages + i], 0, 0))

    in_specs = ([pl.BlockSpec((1, N_HEADS, ts, QK_W), lambda b, pt: (b, 0, 0, 0))]
                + [page_spec(i, page, MLA_KVLORA) for i in range(n_pages)]
                + [page_spec(i, MLA_ROPE, page) for i in range(n_pages)]
                + [pl.BlockSpec((ts, MLA_KVLORA), lambda b, pt: (b, 0)),
                   pl.BlockSpec((ts, MLA_ROPE), lambda b, pt: (b, 0)),
                   pl.BlockSpec((N_HEADS * LANE, BRANCH_W), lambda b, pt: (0, 0))])
    grid_spec = pltpu.PrefetchScalarGridSpec(
        num_scalar_prefetch=1,
        grid=(nbat,),
        in_specs=in_specs,
        out_specs=pl.BlockSpec((ts, BRANCH_W), lambda b, pt: (b, 0)),
    )
    return pl.pallas_call(
        kern,
        grid_spec=grid_spec,
        out_shape=jax.ShapeDtypeStruct((nbat * ts, BRANCH_W), F32),
        compiler_params=_cparams(1, 40),
        name="attn_sample",
    )(page_table.reshape(-1), q_full, *([cache_ckv] * n_pages), *([cache_kpe] * n_pages), ckv_new, kpe_new, bdv)


def _cumsum_rows(x, n):
    rows = lax.broadcasted_iota(jnp.int32, x.shape, 0)
    d = 1
    while d < n:
        x = x + jnp.where(rows >= d, pltpu.roll(x, d, axis=0), 0.0)
        d *= 2
    return x


def _inverse_masks(n):
    rows = lax.broadcasted_iota(jnp.int32, (n, n), 0)
    cols = lax.broadcasted_iota(jnp.int32, (n, n), 1)
    masks = [rows == cols, (rows >> 1) == (cols >> 1)]
    k = 1
    while (1 << k) < n:
        masks.append(((rows >> k) ^ (cols >> k)) == 1)
        k += 1
    return masks


def _gdn_solve(q_raw, k_raw, v_raw, gcm, beta, tril, strict, masks):
    n = range(len(q_raw))
    eye = jnp.where(masks[0], 1.0, 0.0)
    q = [x * lax.rsqrt(jnp.sum(x * x, axis=-1, keepdims=True) + EPS) * (HEAD_DIM ** -0.5) for x in q_raw]
    k = [x * lax.rsqrt(jnp.sum(x * x, axis=-1, keepdims=True) + EPS) for x in k_raw]
    kb = [k[p] * beta[p] for p in n]
    decay = []
    for p in n:
        diff = gcm[p] - gcm[p].T
        decay.append(jnp.where(tril, jnp.exp(jnp.where(tril, diff, 0.0)), 0.0))
    kk = [_bdot_nt(kb[p], k[p]) for p in n]
    a_mat = [jnp.where(strict, kk[p] * decay[p], 0.0) for p in n]
    t = [eye - jnp.where(masks[1], a_mat[p], 0.0) for p in n]
    for m in masks[2:]:
        tl = [_bdot(t[p], jnp.where(m, a_mat[p], 0.0)) for p in n]
        t = [t[p] - _bdot(tl[p], t[p]) for p in n]
    egc = [jnp.exp(gcm[p]) for p in n]
    sol = [_bdot(t[p], jnp.concatenate([v_raw[p] * beta[p], kb[p] * egc[p]], axis=-1)) for p in n]
    qk = [_bdot_nt(q[p], k[p]) for p in n]
    intra = [jnp.where(tril, qk[p] * decay[p], 0.0) for p in n]
    qe = [q[p] * egc[p] for p in n]
    return sol, intra, qe, k


def _head_inputs(gab_ref, exp_ref):
    ab = jnp.dot(gab_ref[...], exp_ref[...], preferred_element_type=F32)
    return ab[:, :BRANCH_W], ab[:, BRANCH_W:]


def _gdn_short_kernel(qkv_ref, gab_ref, z_ref, cst_ref, s0_ref, w_ref, alog_ref, dtb_ref, gn_ref, exp_ref,
                      y_ref, sf_ref, csto_ref, *, nseq, ts):
    cc = nseq * ts
    n2 = 2 * cc
    npair = N_HEADS // 2
    c3 = qkv_ref.shape[1]
    c = BRANCH_W
    u3 = qkv_ref[...].astype(F32).reshape(nseq, ts, c3)
    conv = _silu(_causal_conv(u3, cst_ref[...], w_ref[...], GDN_K)).reshape(cc, c3)
    csto_ref[...] = u3
    ga, gb = _head_inputs(gab_ref, exp_ref)
    g3 = (-jnp.exp(alog_ref[...]) * _softplus(ga + dtb_ref[...])).reshape(nseq, ts, c)
    b_tile = jax.nn.sigmoid(gb)
    tok = lax.broadcasted_iota(jnp.int32, (nseq, ts, c), 1)
    d = 1
    while d < ts:
        g3 = g3 + jnp.where(tok >= d, pltpu.roll(g3, d, axis=1), 0.0)
        d *= 2
    gc = g3.reshape(cc, c)
    g_tot = g3[:, ts - 1:ts, :]
    gl = jnp.broadcast_to(g_tot, (nseq, ts, c)).reshape(cc, c)

    rows = lax.broadcasted_iota(jnp.int32, (n2, n2), 0)
    cols = lax.broadcasted_iota(jnp.int32, (n2, n2), 1)
    same_head = (rows >> 6) == (cols >> 6)
    same_seq = (rows >> 3) == (cols >> 3)
    tril = same_seq & (rows >= cols)
    strict = same_seq & (rows > cols)
    masks = _inverse_masks(n2)[:4]
    gn = gn_ref[...]
    pairs = range(npair)

    def ls(pr):
        return slice(pr * LANE, (pr + 1) * LANE)

    def stack(x):
        return jnp.where(same_head, jnp.concatenate([x, x], axis=0), 0.0)

    def twice(x):
        return jnp.concatenate([x, x], axis=0)

    erow = (lax.broadcasted_iota(jnp.int32, (n2, nseq * n2), 0) >> 3) & (nseq - 1)
    eblk = lax.broadcasted_iota(jnp.int32, (n2, nseq * n2), 1) >> 7
    emask = erow == eblk

    def expand(x):
        return jnp.where(emask, jnp.concatenate([x] * nseq, axis=1), 0.0)

    gcm = [stack(gc[:, ls(pr)]) for pr in pairs]
    glm = [stack(gl[:, ls(pr)]) for pr in pairs]
    beta = [twice(b_tile[:, ls(pr)]) for pr in pairs]
    sol, intra, qe, k = _gdn_solve([stack(conv[:, ls(pr)]) for pr in pairs],
                                   [stack(conv[:, c + pr * LANE:c + (pr + 1) * LANE]) for pr in pairs],
                                   [stack(conv[:, 2 * c + pr * LANE:2 * c + (pr + 1) * LANE]) for pr in pairs],
                                   gcm, beta, tril, strict, masks)
    kdec = [k[pr] * jnp.exp(glm[pr] - gcm[pr]) for pr in pairs]
    zero_blk = jnp.zeros((HEAD_DIM, HEAD_DIM), F32)
    s_old = []
    sdec = []
    for pr in pairs:
        blocks = []
        decs = []
        for b in range(nseq):
            top = jnp.concatenate([s0_ref[b, 2 * pr], zero_blk], axis=-1)
            bot = jnp.concatenate([zero_blk, s0_ref[b, 2 * pr + 1]], axis=-1)
            blocks += [top, bot]
            decs.append(jnp.broadcast_to(jnp.exp(g_tot[b, :, ls(pr)]), (n2, LANE)))
        s_old.append(jnp.concatenate(blocks, axis=0))
        sdec.append(jnp.concatenate(decs, axis=0))
    v_new = [sol[pr][:, :n2] - _bdot(expand(sol[pr][:, n2:]), s_old[pr]) for pr in pairs]
    o_st = [_bdot(expand(qe[pr]), s_old[pr]) for pr in pairs]
    o_in = [_bdot(intra[pr], v_new[pr]) for pr in pairs]
    s_up = [_bdot_tn(expand(kdec[pr]), v_new[pr]) for pr in pairs]
    for pr in pairs:
        s_new = s_old[pr] * sdec[pr] + s_up[pr]
        for b in range(nseq):
            sf_ref[b, 2 * pr] = s_new[b * n2:b * n2 + HEAD_DIM, :HEAD_DIM]
            sf_ref[b, 2 * pr + 1] = s_new[b * n2 + HEAD_DIM:(b + 1) * n2, HEAD_DIM:]
        o = o_st[pr] + o_in[pr]
        o = o * lax.rsqrt(jnp.sum(o * o, axis=-1, keepdims=True) * (1.0 / HEAD_DIM) + EPS)
        o = o[:cc] + o[cc:]
        y_ref[:, ls(pr)] = o * gn[:, ls(pr)] * _silu(z_ref[:, ls(pr)].astype(F32))


def _gdn_short(proj, cst8, s0, w, alog_b, dtb_b, gn_b, expand, nbat, ts):
    nseq = GDN_CHUNK // ts
    assert ts == SUBLANE and nbat % nseq == 0
    c = BRANCH_W
    c3 = 3 * c
    tm = nseq * ts
    kern = functools.partial(_gdn_short_kernel, nseq=nseq, ts=ts)
    s_spec = pl.BlockSpec((nseq, N_HEADS, HEAD_DIM, HEAD_DIM), lambda i: (i, 0, 0, 0))
    cst_spec = pl.BlockSpec((nseq, SUBLANE, c3), lambda i: (i, 0, 0))

    def tok(width, off):
        return pl.BlockSpec((tm, width), lambda i: (i, off // width))

    return pl.pallas_call(
        kern,
        grid=(nbat // nseq,),
        in_specs=[tok(c3, OFF_GQKV), tok(LANE, OFF_GAB), tok(c, OFF_GZ), cst_spec, s_spec,
                  _const_spec((GDN_K, c3)), _const_spec((1, c)), _const_spec((1, c)), _const_spec((1, c)),
                  _const_spec((LANE, 2 * c))],
        out_specs=[pl.BlockSpec((tm, c), lambda i: (i, 0)), s_spec, cst_spec],
        out_shape=[jax.ShapeDtypeStruct((nbat * ts, c), F32),
                   jax.ShapeDtypeStruct((nbat, N_HEADS, HEAD_DIM, HEAD_DIM), F32),
                   jax.ShapeDtypeStruct((nbat, SUBLANE, c3), F32)],
        compiler_params=_cparams(1, 40),
        name="gdn_short",
    )(proj, proj, proj, cst8, s0, w, alog_b, dtb_b, gn_b, expand)


def _gdn_kernel(qkv_ref, gab_ref, z_ref, cst_ref, s0_ref, w_ref, alog_ref, dtb_ref, gn_ref, exp_ref,
                y_ref, sf_ref, csto_ref, s_scr, c_scr, qkv_s, g_ref, beta_ref, *, n_chunks):
    tc = qkv_ref.shape[0]
    c3 = qkv_ref.shape[1]
    ga, gb = _head_inputs(gab_ref, exp_ref)
    c8 = _carry_in(pl.program_id(1), cst_ref, c_scr)
    u3 = qkv_ref[...].astype(F32).reshape(1, tc, c3)
    qkv_s[...] = _silu(_causal_conv(u3, c8, w_ref[...], GDN_K)).reshape(tc, c3)
    g_ref[...] = -jnp.exp(alog_ref[...]) * _softplus(ga + dtb_ref[...])
    beta_ref[...] = jax.nn.sigmoid(gb)
    hist = u3[:, tc - SUBLANE:, :]
    c_scr[...] = hist
    csto_ref[...] = hist
    q_ref = qkv_s.at[:, 0:BRANCH_W]
    k_ref = qkv_s.at[:, BRANCH_W:2 * BRANCH_W]
    v_ref = qkv_s.at[:, 2 * BRANCH_W:3 * BRANCH_W]

    cc = GDN_CHUNK
    n2 = 2 * cc
    npair = N_HEADS // 2
    zero_blk = jnp.zeros((HEAD_DIM, HEAD_DIM), F32)

    @pl.when(pl.program_id(1) == 0)
    def _():
        for pr in range(npair):
            top = jnp.concatenate([s0_ref[0, 2 * pr], zero_blk], axis=-1)
            bot = jnp.concatenate([zero_blk, s0_ref[0, 2 * pr + 1]], axis=-1)
            s_scr[pr] = jnp.concatenate([top, bot], axis=0)

    rows = lax.broadcasted_iota(jnp.int32, (n2, n2), 0)
    cols = lax.broadcasted_iota(jnp.int32, (n2, n2), 1)
    same = (rows >> 6) == (cols >> 6)
    tril = same & (rows >= cols)
    strict = same & (rows > cols)
    masks = _inverse_masks(n2)[:7]
    gn = gn_ref[...]
    probs = [(ci, pr) for ci in range(n_chunks) for pr in range(npair)]

    def rs(ci):
        return slice(ci * cc, (ci + 1) * cc)

    def ls(pr):
        return slice(pr * LANE, (pr + 1) * LANE)

    def stack(x):
        return jnp.where(same, jnp.concatenate([x, x], axis=0), 0.0)

    def twice(x):
        return jnp.concatenate([x, x], axis=0)

    gc_all = [_cumsum_rows(g_ref[rs(ci), :], cc) for ci in range(n_chunks)]
    gcm = [stack(gc_all[ci][:, ls(pr)]) for ci, pr in probs]
    beta = [twice(beta_ref[rs(ci), ls(pr)]) for ci, pr in probs]
    sol, intra, qe, k = _gdn_solve([stack(q_ref[rs(ci), ls(pr)]) for ci, pr in probs],
                                   [stack(k_ref[rs(ci), ls(pr)]) for ci, pr in probs],
                                   [stack(v_ref[rs(ci), ls(pr)]) for ci, pr in probs],
                                   gcm, beta, tril, strict, masks)
    g_last = [gc_all[ci][cc - 1:cc, ls(pr)] for ci, pr in probs]
    kdec = [k[p] * jnp.exp(g_last[p] - gcm[p]) for p in range(len(probs))]
    sdec = [jnp.exp(g_last[p]) for p in range(len(probs))]

    for ci in range(n_chunks):
        ps = [ci * npair + pr for pr in range(npair)]
        s_old = [s_scr[pr] for pr in range(npair)]
        v_new = [sol[p][:, :n2] - _bdot(sol[p][:, n2:], s_old[pr]) for pr, p in enumerate(ps)]
        o_st = [_bdot(qe[p], s_old[pr]) for pr, p in enumerate(ps)]
        o_in = [_bdot(intra[p], v_new[pr]) for pr, p in enumerate(ps)]
        s_up = [_bdot_tn(kdec[p], v_new[pr]) for pr, p in enumerate(ps)]
        for pr, p in enumerate(ps):
            s_scr[pr] = s_old[pr] * sdec[p] + s_up[pr]
            o = o_st[pr] + o_in[pr]
            o = o * lax.rsqrt(jnp.sum(o * o, axis=-1, keepdims=True) * (1.0 / HEAD_DIM) + EPS)
            o = o[:cc] + o[cc:]
            y_ref[rs(ci), ls(pr)] = o * gn[:, ls(pr)] * _silu(z_ref[rs(ci), ls(pr)].astype(F32))

    @pl.when(pl.program_id(1) == pl.num_programs(1) - 1)
    def _():
        for pr in range(npair):
            s_pair = s_scr[pr]
            sf_ref[0, 2 * pr] = s_pair[:HEAD_DIM, :HEAD_DIM]
            sf_ref[0, 2 * pr + 1] = s_pair[HEAD_DIM:, HEAD_DIM:]


def _gdn(proj, cst8, s0, w, alog_b, dtb_b, gn_b, expand, nbat, t, tc):
    tper = t // tc
    c = BRANCH_W
    c3 = 3 * c
    kern = functools.partial(_gdn_kernel, n_chunks=tc // GDN_CHUNK)
    s_spec = pl.BlockSpec((1, N_HEADS, HEAD_DIM, HEAD_DIM), lambda b, t_: (b, 0, 0, 0))
    cst_spec = pl.BlockSpec((1, SUBLANE, c3), lambda b, t_: (b, 0, 0))
    return pl.pallas_call(
        kern,
        grid=(nbat, tper),
        in_specs=[_tok_spec(tc, c3, OFF_GQKV // c3, tper), _tok_spec(tc, LANE, OFF_GAB // LANE, tper),
                  _tok_spec(tc, c, OFF_GZ // c, tper),
                  cst_spec, s_spec, _const_spec((GDN_K, c3)), _const_spec((1, c)), _const_spec((1, c)),
                  _const_spec((1, c)), _const_spec((LANE, 2 * c))],
        out_specs=[_tok_spec(tc, c, 0, tper), s_spec, cst_spec],
        out_shape=[jax.ShapeDtypeStruct((nbat * t, c), F32),
                   jax.ShapeDtypeStruct((nbat, N_HEADS, HEAD_DIM, HEAD_DIM), F32),
                   jax.ShapeDtypeStruct((nbat, SUBLANE, c3), F32)],
        scratch_shapes=[pltpu.VMEM((N_HEADS // 2, 2 * HEAD_DIM, 2 * HEAD_DIM), F32),
                        pltpu.VMEM((1, SUBLANE, c3), F32), pltpu.VMEM((tc, c3), F32),
                        pltpu.VMEM((tc, c), F32), pltpu.VMEM((tc, c), F32)],
        compiler_params=_cparams(2, 40),
        name="gdn_rule",
    )(proj, proj, proj, cst8, s0, w, alog_b, dtb_b, gn_b, expand)


def _merge_kernel(x_ref, mod_ref, g_ref, bg_ref, cg_ref, xt_ref, st_ref, wsc_ref, yb_ref, yc_ref,
                  lx_ref, lg_ref, lst_ref, h0_ref, wl_ref, cbl_ref, wgl_ref, bgl_ref, lam_ref,
                  wg_ref, wbo_ref, wmix_ref, o_ref, sto_ref, lsto_ref, hl_ref, c_scr, lc_scr, h_scr, *, pos0):
    x = x_ref[...]
    nb, tt, d = x.shape
    c = bg_ref.shape[-1]
    t_idx = pl.program_id(1)
    mod = mod_ref[...]
    h = _modulated_norm(x, g_ref[...], mod, 0, 1).reshape(nb * tt, d).astype(BF16)
    c8 = _carry_in(t_idx, st_ref, c_scr)
    u3 = (cg_ref[...].astype(F32) * xt_ref[...].astype(F32)).reshape(nb, tt, c)
    y_a = bg_ref[...].astype(F32) * _causal_conv(u3, c8, wsc_ref[...], SC_K).reshape(nb * tt, c)
    hist = u3[:, tt - SUBLANE:, :]
    c_scr[...] = hist
    sto_ref[...] = hist
    lc8 = _carry_in(t_idx, lst_ref, lc_scr)

    @pl.when(t_idx == 0)
    def _():
        h_scr[...] = h0_ref[...]

    y_d, lhist, h_last = _lru_branch(lx_ref, lg_ref, lc8, h_scr[...], wl_ref, cbl_ref, wgl_ref, bgl_ref, lam_ref,
                                     nb, tt, pos0 + t_idx * tt)
    lc_scr[...] = lhist
    lsto_ref[...] = lhist
    h_scr[...] = h_last
    hl_ref[...] = h_last
    acc = None
    for n, y in enumerate((y_a, yb_ref[...], yc_ref[...], y_d)):
        gate = jax.nn.sigmoid(jnp.dot(h, wg_ref[:, n * d:(n + 1) * d], preferred_element_type=F32))
        term = gate * _bdot(y, wbo_ref[n])
        acc = term if acc is None else acc + term
    mix = _bdot(acc, wmix_ref[...])
    o_ref[...] = x + mod[:, 2:3, :] * mix.reshape(nb, tt, d)


def _merge(x3, mod, g, proj, st8, wsc, ys, lru_st8, lru_h0, lru_w, wg, wbo, wmix, nb, tt, pos0):
    nbat, t, d = x3.shape
    tper = t // tt
    tm = nb * tt
    c = BRANCH_W
    x_spec = pl.BlockSpec((nb, tt, d), lambda b, t_: (b, t_, 0))
    h_spec = pl.BlockSpec((nb, 1, c), lambda b, t_: (b, 0, 0))
    return pl.pallas_call(
        functools.partial(_merge_kernel, pos0=pos0),
        grid=(nbat // nb, tper),
        in_specs=[x_spec, pl.BlockSpec((nb, 6, d), lambda b, t_: (b, 0, 0)), _const_spec((1, d)),
                  _tok_spec(tm, c, OFF_SC // c, tper), _tok_spec(tm, c, OFF_SC // c + 1, tper),
                  _tok_spec(tm, c, OFF_SC // c + 2, tper), _state_spec(nb, c), _const_spec((SC_K, c))]
                 + [_tok_spec(tm, c, 0, tper)] * 2
                 + [_tok_spec(tm, c, OFF_LX // c, tper), _tok_spec(tm, c, OFF_LG // c, tper), _state_spec(nb, c),
                    h_spec, _const_spec((LRU_K, c)), _const_spec((1, c)), _const_spec((c, 2 * c)),
                    _const_spec((1, 2 * c)), _const_spec((1, c))]
                 + [_const_spec(wg.shape), _const_spec(wbo.shape), _const_spec(wmix.shape)],
        out_specs=[x_spec, _state_spec(nb, c), _state_spec(nb, c), h_spec],
        out_shape=[jax.ShapeDtypeStruct(x3.shape, F32), jax.ShapeDtypeStruct((nbat, SUBLANE, c), F32),
                   jax.ShapeDtypeStruct((nbat, SUBLANE, c), F32), jax.ShapeDtypeStruct((nbat, 1, c), F32)],
        scratch_shapes=[pltpu.VMEM((nb, SUBLANE, c), F32), pltpu.VMEM((nb, SUBLANE, c), F32),
                        pltpu.VMEM((nb, 1, c), F32)],
        compiler_params=_cparams(2, 48),
        name="merge",
    )(x3, mod, g, proj, proj, proj, st8, wsc, *ys, proj, proj, lru_st8, lru_h0, *lru_w, wg, wbo, wmix)


def _ffn_kernel(x_ref, mod_ref, g_ref, win_ref, wout_ref, gf_ref, o_ref, *, final):
    x = x_ref[...]
    nb, tt, d = x.shape
    mod = mod_ref[...]
    fh = wout_ref.shape[0]
    h = _modulated_norm(x, g_ref[...], mod, 3, 4).reshape(nb * tt, d).astype(BF16)
    gate = jnp.dot(h, win_ref[:, :fh], preferred_element_type=F32)
    up = jnp.dot(h, win_ref[:, fh:], preferred_element_type=F32)
    out = _bdot(_silu(gate) * up, wout_ref[...])
    xo = x + mod[:, 5:6, :] * out.reshape(nb, tt, d)
    o_ref[...] = _rms(xo, gf_ref[...]) if final else xo


def _ffn(x3, mod, g, win, wout, gf, nb, tt, final):
    nbat, t, d = x3.shape
    x_spec = pl.BlockSpec((nb, tt, d), lambda b, t_: (b, t_, 0))
    kern = functools.partial(_ffn_kernel, final=final)
    return pl.pallas_call(
        kern,
        grid=(nbat // nb, t // tt),
        in_specs=[x_spec, pl.BlockSpec((nb, 6, d), lambda b, t_: (b, 0, 0)), _const_spec((1, d)),
                  _const_spec(win.shape), _const_spec(wout.shape), _const_spec((1, d))],
        out_specs=x_spec,
        out_shape=jax.ShapeDtypeStruct(x3.shape, F32),
        compiler_params=_cparams(2, 56),
        name="ffn",
    )(x3, mod, g, win, wout, gf)


def _block_diag(blocks):
    h, r, c = blocks.shape
    eye = jnp.eye(h, dtype=blocks.dtype)
    return jnp.einsum('hrc,hg->hrgc', blocks, eye).reshape(h * r, h * c)


def _prep_layer_weights(w_in, w_qb, w_kvb, w_lru_gate_a, w_lru_gate_x):
    d = w_in.shape[0]
    c = BRANCH_W
    o = 0
    segs = {}
    for name, width in (('sc', 3 * c), ('qa', MLA_QLORA), ('ckv', MLA_KVLORA), ('kpe', MLA_ROPE), ('gqkv', 3 * c),
                        ('gz', c), ('ga', N_HEADS), ('gb', N_HEADS), ('lx', c), ('lg', c)):
        segs[name] = w_in[:, o:o + width]
        o += width
    half = MLA_ROPE // 2
    zpad = jnp.zeros((d, LANE - MLA_ROPE), F32)
    kpe = segs['kpe']
    kpa = jnp.concatenate([kpe, zpad], axis=1)
    kpb = jnp.concatenate([-kpe[:, half:], kpe[:, :half], zpad], axis=1)
    w_proj = jnp.concatenate(
        [segs['sc'], segs['gqkv'], segs['gz'], segs['lx'], segs['lg'], segs['qa'], segs['ckv'], kpa, kpb,
         segs['ga'], segs['gb'], jnp.zeros((d, LANE - 2 * N_HEADS), F32)],
        axis=1).astype(BF16)

    ql = w_qb.shape[0]
    wq = w_qb.reshape(ql, N_HEADS, HEAD_DIM + MLA_ROPE)
    wn = wq[:, :, :HEAD_DIM].reshape(ql, c)
    pe = wq[:, :, HEAD_DIM:]
    zq = jnp.zeros((ql, N_HEADS, LANE - MLA_ROPE), F32)
    wpa = jnp.concatenate([pe, zq], axis=2).reshape(ql, N_HEADS * LANE)
    wpb = jnp.concatenate([-pe[:, :, half:], pe[:, :, :half], zq], axis=2).reshape(ql, N_HEADS * LANE)
    w_kb = w_kvb[:, :, :HEAD_DIM]
    w_vb = w_kvb[:, :, HEAD_DIM:]
    bdk = _block_diag(jnp.transpose(w_kb, (1, 2, 0)))
    bdv = _block_diag(jnp.transpose(w_vb, (1, 0, 2)))
    wlg = jnp.concatenate([_block_diag(w_lru_gate_a), _block_diag(w_lru_gate_x)], axis=1)
    wvt = jnp.transpose(w_vb, (1, 2, 0))
    return dict(w_proj=w_proj, wn=wn.astype(BF16), wpa=wpa.astype(BF16), wpb=wpb.astype(BF16),
                bdk=bdk.astype(BF16), bdv=bdv.astype(BF16), wvt=wvt.astype(BF16), wlg=wlg.astype(BF16))


def _rope_tables(pos):
    half = MLA_ROPE // 2
    inv = ROPE_THETA ** (-jnp.arange(half, dtype=F32) / half)
    ang = pos.astype(F32)[:, None] * inv[None, :]
    pad = jnp.zeros((pos.shape[0], LANE - MLA_ROPE), F32)
    cos = jnp.concatenate([jnp.cos(ang), jnp.cos(ang), pad], axis=1)
    sin = jnp.concatenate([jnp.sin(ang), jnp.sin(ang), pad], axis=1)
    return cos, sin


def _pad_state(st, k_w):
    return jnp.pad(st, ((0, 0), (SUBLANE - (k_w - 1), 0), (0, 0)))


def _tile(t, pref):
    tt = min(t, pref)
    while t % tt:
        tt //= 2
    return tt


def _group_layer(x3, mod, lw, p, st, rope, attend, cfg):
    nbat, t, d = x3.shape
    nb, tt = cfg['nb'], cfg['tt']
    proj = _inproj(x3, mod, p['g_norm_mix'], lw['w_proj'], cfg['nb_proj'], cfg['tt_proj'])
    q_full, ckv, kpe, kf, *vt = _mla_pre(proj, rope[0], rope[1], p['g_q_norm'], p['g_kv_norm'], lw['wn'], lw['wpa'],
                                         lw['wpb'], lw['bdk'], nbat, t, nb, tt, cfg['q_dtype'], cfg['emit_vt'])
    y_b = attend(q_full, ckv, kpe, kf, vt, lw)
    gdn_w = (p['w_gdn_conv'], p['alog_b'], p['dtb_b'], p['gn_b'], p['head_expand'])
    if t < GDN_CHUNK:
        y_c, s_gdn, gc8 = _gdn_short(proj, st['gdn_conv'], st['gdn'], *gdn_w, nbat, t)
    else:
        y_c, s_gdn, gc8 = _gdn(proj, st['gdn_conv'], st['gdn'], *gdn_w, nbat, t, cfg['tc'])
    lru_w = (p['w_lru_conv'], p['b_lru_conv'], lw['wlg'], p['b_lru_gates'], p['nsl'])
    x1, sc8, lc8, h_last = _merge(x3, mod, p['g_norm_mix'], proj, st['sconv'], p['w_sc_conv'], (y_b, y_c),
                                  st['lru_conv'], st['lru'], lru_w, p['w_merge_gate'], p['w_branch_out'],
                                  p['w_mix_out'], cfg['nb_mm'], cfg['tt_mm'], cfg['pos0'])
    x2 = _ffn(x1, mod, p['g_norm_ffn'], p['w_ffn_in'], p['w_ffn_out'], p['g_final'], cfg['nb_mm'], cfg['tt_mm'],
              cfg['final'])
    new_st = dict(ckv=ckv.reshape(nbat, t, -1), kpe=kpe.reshape(nbat, t, -1),
                  sconv=sc8[:, SUBLANE - (SC_K - 1):], gdn_conv=gc8[:, SUBLANE - (GDN_K - 1):], gdn=s_gdn,
                  lru_conv=lc8[:, SUBLANE - (LRU_K - 1):], lru=h_last[:, 0, :])
    return x2, new_st


STATE_KEYS = ('ckv', 'kpe', 'sconv', 'gdn_conv', 'gdn', 'lru_conv', 'lru')


def kernel(x_prompt, x_sample, c_prompt, c_sample, cache_mla_ckv, cache_mla_kpe, page_table, state_sconv, state_gdn_conv, state_gdn, state_lru_conv, state_lru, w_ada, b_ada, g_norm_mix, g_norm_ffn, w_in, w_sc_conv, g_q_norm, w_qb, g_kv_norm, w_kvb, w_gdn_conv, gdn_a_log, gdn_dt_bias, g_gdn_norm, w_lru_conv, b_lru_conv, w_lru_gate_a, b_lru_gate_a, w_lru_gate_x, b_lru_gate_x, lru_lambda, w_branch_out, w_merge_gate, w_mix_out, w_ffn_in, w_ffn_out, g_final):
    bp, tp, d = x_prompt.shape
    bs, ts, _ = x_sample.shape
    depth = w_in.shape[0]
    n_pages = page_table.shape[1]
    past_len = n_pages * cache_mla_ckv.shape[2]
    c = BRANCH_W
    assert ts == SUBLANE and tp % GDN_CHUNK == 0 and d % LANE == 0

    mod_all = _ada(jnp.concatenate([c_prompt, c_sample], axis=0), w_ada, b_ada).reshape(depth, bp + bs, 6, d)

    nb_s = _tile(bs, 32)
    cos_p, sin_p = _rope_tables(jnp.arange(tp, dtype=jnp.int32))
    cos_s, sin_s = _rope_tables(past_len + jnp.arange(ts, dtype=jnp.int32))
    rope_p = (cos_p, sin_p)
    rope_s = (jnp.tile(cos_s, (nb_s, 1)), jnp.tile(sin_s, (nb_s, 1)))

    tq = _tile(tp, 256)
    tk = _tile(tp, 256)
    cfg_p = dict(nb=1, tt=tk, nb_proj=1, tt_proj=_tile(tp, 2048), q_dtype=BF16, emit_vt=True,
                 tc=_tile(tp, 256), pos0=0, nb_scan=1, tt_scan=_tile(tp, 256), nb_mm=1, tt_mm=_tile(tp, 512))
    cfg_s = dict(nb=nb_s, tt=ts, nb_proj=_tile(bs, 128), tt_proj=ts, q_dtype=F32, emit_vt=False,
                 tc=GDN_CHUNK, pos0=past_len, nb_scan=nb_s, tt_scan=ts, nb_mm=_tile(bs, 64), tt_mm=ts)
    cache_kpe_t = jnp.swapaxes(cache_mla_kpe, 2, 3)
    head_expand = (jnp.arange(LANE)[:, None] == jnp.arange(2 * c)[None, :] // HEAD_DIM).astype(BF16)

    xp, xs = x_prompt, x_sample
    out_p = {k: [] for k in STATE_KEYS}
    out_s = {k: [] for k in STATE_KEYS}
    for l in range(depth):
        lw = _prep_layer_weights(w_in[l], w_qb[l], w_kvb[l], w_lru_gate_a[l], w_lru_gate_x[l])
        p = dict(
            g_norm_mix=g_norm_mix[l][None], g_norm_ffn=g_norm_ffn[l][None], g_final=g_final[None],
            w_sc_conv=w_sc_conv[l], g_q_norm=g_q_norm[l][None], g_kv_norm=g_kv_norm[l][None],
            w_gdn_conv=w_gdn_conv[l],
            alog_b=jnp.repeat(gdn_a_log[l], HEAD_DIM)[None], dtb_b=jnp.repeat(gdn_dt_bias[l], HEAD_DIM)[None],
            gn_b=jnp.tile(g_gdn_norm[l], N_HEADS)[None], head_expand=head_expand,
            w_lru_conv=w_lru_conv[l], b_lru_conv=b_lru_conv[l][None],
            b_lru_gates=jnp.concatenate([b_lru_gate_a[l], b_lru_gate_x[l]])[None],
            nsl=lru_lambda[l][None],
            w_merge_gate=w_merge_gate[l].astype(BF16), w_branch_out=w_branch_out[l].astype(BF16),
            w_mix_out=w_mix_out[l].astype(BF16), w_ffn_in=w_ffn_in[l].astype(BF16),
            w_ffn_out=w_ffn_out[l].astype(BF16))
        final = l == depth - 1

        st_p = dict(sconv=jnp.zeros((bp, SUBLANE, c), F32), gdn_conv=jnp.zeros((bp, SUBLANE, 3 * c), F32),
                    gdn=jnp.zeros((bp, N_HEADS, HEAD_DIM, HEAD_DIM), F32),
                    lru_conv=jnp.zeros((bp, SUBLANE, c), F32), lru=jnp.zeros((bp, 1, c), F32))

        def attend_p(q_full, ckv, kpe, kf, vt, lw_):
            return _attn_prompt(q_full, kf, vt[0], lw_['wvt'], bp, tp, tq, tk)

        xp, nst_p = _group_layer(xp, mod_all[l, :bp], lw, p, st_p, rope_p, attend_p, dict(cfg_p, final=final))

        st_s = dict(sconv=_pad_state(state_sconv[l], SC_K), gdn_conv=_pad_state(state_gdn_conv[l], GDN_K),
                    gdn=state_gdn[l], lru_conv=_pad_state(state_lru_conv[l], LRU_K), lru=state_lru[l][:, None, :])

        def attend_s(q_full, ckv, kpe, kf, vt, lw_, layer=l):
            return _attn_sample(q_full, cache_mla_ckv, cache_kpe_t, page_table, layer, ckv, kpe, lw_['bdv'], bs, ts)

        xs, nst_s = _group_layer(xs, mod_all[l, bp:], lw, p, st_s, rope_s, attend_s, dict(cfg_s, final=final))
        for k in STATE_KEYS:
            out_p[k].append(nst_p[k])
            out_s[k].append(nst_s[k])

    return ((xp, xs) + tuple(jnp.stack(out_p[k]) for k in STATE_KEYS)
            + tuple(jnp.stack(out_s[k]) for k in STATE_KEYS))
```

```python
import functools
import math

import jax
import jax.numpy as jnp
from jax import lax
from jax.experimental import pallas as pl
from jax.experimental.pallas import tpu as pltpu

F32 = jnp.float32
BF16 = jnp.bfloat16

HEAD_DIM = 64
N_HEADS = 8
BRANCH_W = N_HEADS * HEAD_DIM
MLA_ROPE = HEAD_DIM // 2
MLA_QLORA = 256
MLA_KVLORA = 128
ROPE_THETA = 10000.0
SC_K = 3
GDN_K = 4
LRU_K = 4
LRU_C = 8.0
GDN_CHUNK = 64
EPS = 1e-6
NEG_BIG = -1e30
LANE = 128
SUBLANE = 8
QK_W = 2 * LANE
ATTN_GROUP_W = 256

OFF_SC = 0
OFF_GQKV = 1536
OFF_GZ = 3072
OFF_LX = 3584
OFF_LG = 4096
OFF_QA = 4608
OFF_CKV = 4864
OFF_KPA = 4992
OFF_KPB = 5120
OFF_GAB = 5248
PROJ_W = 5376
PROJ_TN = 896


def _cparams(n_axes, vmem_mib):
    return pltpu.CompilerParams(dimension_semantics=("arbitrary",) * n_axes,
                                vmem_limit_bytes=vmem_mib * 1024 * 1024)


def _const_spec(shape):
    nd = len(shape)
    return pl.BlockSpec(shape, lambda *_: (0,) * nd, pipeline_mode=pl.Buffered(1))


def _bdot(a, b):
    return jnp.dot(a.astype(BF16), b.astype(BF16), preferred_element_type=F32)


def _bdot_nt(a, b):
    return lax.dot_general(a.astype(BF16), b.astype(BF16), (((1,), (1,)), ((), ())),
                           preferred_element_type=F32)


def _bdot_tn(a, b):
    return lax.dot_general(a.astype(BF16), b.astype(BF16), (((0,), (0,)), ((), ())),
                           preferred_element_type=F32)


def _silu(x):
    return x * jax.nn.sigmoid(x)


def _softplus(x):
    return jnp.maximum(x, 0.0) + jnp.log(1.0 + jnp.exp(-jnp.abs(x)))


def _gelu_tanh(x):
    return 0.5 * x * (1.0 + jnp.tanh(math.sqrt(2.0 / math.pi) * (x + 0.044715 * (x * x * x))))


def _rms(x, g):
    return x * lax.rsqrt(jnp.mean(x * x, axis=-1, keepdims=True) + EPS) * g


def _modulated_norm(x3, g, mod, shift_row, scale_row):
    y = _rms(x3, g)
    return y * (1.0 + mod[:, scale_row:scale_row + 1, :]) + mod[:, shift_row:shift_row + 1, :]


def _ada_kernel(c_ref, w_ref, b_ref, o_ref):
    o_ref[0] = _bdot(_silu(c_ref[...]), w_ref[0]) + b_ref[0]


def _ada(c_all, w_ada, b_ada):
    depth, d, n = w_ada.shape
    nb = c_all.shape[0]
    tn = n // 4
    return pl.pallas_call(
        _ada_kernel,
        grid=(depth, n // tn),
        in_specs=[pl.BlockSpec((nb, d), lambda l, j: (0, 0)),
                  pl.BlockSpec((1, d, tn), lambda l, j: (l, 0, j)),
                  pl.BlockSpec((1, 1, tn), lambda l, j: (l, 0, j))],
        out_specs=pl.BlockSpec((1, nb, tn), lambda l, j: (l, 0, j)),
        out_shape=jax.ShapeDtypeStruct((depth, nb, n), F32),
        compiler_params=_cparams(2, 40),
        name="ada_mod",
    )(c_all, w_ada, b_ada.reshape(depth, 1, n))


def _inproj_kernel(x_ref, mod_ref, g_ref, w_ref, o_ref, h_scr):
    @pl.when(pl.program_id(1) == 0)
    def _():
        h = _modulated_norm(x_ref[...], g_ref[...], mod_ref[...], 0, 1)
        h_scr[...] = h.reshape(h_scr.shape).astype(BF16)

    o_ref[...] = jnp.dot(h_scr[...], w_ref[...], preferred_element_type=F32).astype(o_ref.dtype)


def _inproj(x3, mod, g, w, nb, tt):
    nbat, t, d = x3.shape
    tper = t // tt
    tm = nb * tt
    n_m = (nbat // nb) * tper
    return pl.pallas_call(
        _inproj_kernel,
        grid=(n_m, PROJ_W // PROJ_TN),
        in_specs=[pl.BlockSpec((nb, tt, d), lambda i, j: (i // tper, i % tper, 0)),
                  pl.BlockSpec((nb, 6, d), lambda i, j: (i // tper, 0, 0)),
                  _const_spec((1, d)),
                  pl.BlockSpec((d, PROJ_TN), lambda i, j: (0, j))],
        out_specs=pl.BlockSpec((tm, PROJ_TN), lambda i, j: (i, j)),
        out_shape=jax.ShapeDtypeStruct((nbat * t, PROJ_W), BF16),
        scratch_shapes=[pltpu.VMEM((tm, d), BF16)],
        compiler_params=_cparams(2, 40),
        name="in_proj",
    )(x3, mod, g, w)


def _causal_conv(u3, c8, w, k_w):
    nb, tt, c = u3.shape
    rows = lax.broadcasted_iota(jnp.int32, (nb, SUBLANE, c), 1)
    acc = None
    for j in range(k_w):
        s = k_w - 1 - j
        if s == 0:
            sh = u3
        else:
            full = pltpu.roll(u3, s, axis=1)
            top = jnp.where(rows < s, pltpu.roll(c8, s, axis=1), full[:, 0:SUBLANE, :])
            sh = top if tt == SUBLANE else jnp.concatenate([top, full[:, SUBLANE:, :]], axis=1)
        term = sh * w[j:j + 1, :]
        acc = term if acc is None else acc + term
    return acc


def _carry_in(t_idx, st_ref, c_scr):
    @pl.when(t_idx == 0)
    def _():
        c_scr[...] = st_ref[...]

    return c_scr[...]


def _tok_spec(tm, width, col_block, tper):
    return pl.BlockSpec((tm, width), lambda b, t: (b * tper + t, col_block))


def _state_spec(nb, c):
    return pl.BlockSpec((nb, SUBLANE, c), lambda b, t: (b, 0, 0))


def _lru_branch(lx_ref, lg_ref, c8, h_in, w_ref, cb_ref, wg_ref, bg_ref, lam_ref, nb, tt, pos):
    c = lx_ref.shape[-1]
    u3 = lx_ref[...].astype(F32).reshape(nb, tt, c)
    u = (_causal_conv(u3, c8, w_ref[...], LRU_K) + cb_ref[...]).reshape(nb * tt, c)
    hist = u3[:, tt - SUBLANE:, :]
    gates = jax.nn.sigmoid(_bdot(u, wg_ref[...]) + bg_ref[...])
    r = gates[:, :c]
    ig = gates[:, c:]
    log_a = -LRU_C * r * _softplus(-lam_ref[...])
    rows = lax.broadcasted_iota(jnp.int32, (nb, tt, c), 1)
    mult = jnp.sqrt(1.0 - jnp.exp(2.0 * log_a)).reshape(nb, tt, c)
    mult = jnp.where(pos + rows == 0, 1.0, mult)
    grp = tt // SUBLANE
    a = jnp.exp(log_a).reshape(nb * grp, SUBLANE, c)
    b = (mult * (ig * u).reshape(nb, tt, c)).reshape(nb * grp, SUBLANE, c)
    tok = lax.broadcasted_iota(jnp.int32, (nb * grp, SUBLANE, c), 1)
    d = 1
    while d < SUBLANE:
        keep = tok >= d
        a_s = jnp.where(keep, pltpu.roll(a, d, axis=1), 1.0)
        b_s = jnp.where(keep, pltpu.roll(b, d, axis=1), 0.0)
        b = a * b_s + b
        a = a * a_s
        d *= 2
    a = a.reshape(nb, grp, SUBLANE, c)
    b = b.reshape(nb, grp, SUBLANE, c)
    parts = []
    for g in range(grp):
        part = b[:, g] + a[:, g] * h_in
        h_in = part[:, SUBLANE - 1:SUBLANE, :]
        parts.append(part)
    hs = parts[0] if grp == 1 else jnp.concatenate(parts, axis=1)
    return hs.reshape(nb * tt, c) * _gelu_tanh(lg_ref[...].astype(F32)), hist, h_in


def _mla_pre_kernel(qa_ref, ckv_ref, kpa_ref, kpb_ref, cos_ref, sin_ref, gq_ref, gkv_ref,
                    wn_ref, wpa_ref, wpb_ref, bdk_ref, q_out, ckv_out, kpe_out, kf_out, *rest, nb, tt):
    cos = cos_ref[...]
    sin = sin_ref[...]
    cq = _rms(qa_ref[...].astype(F32), gq_ref[...]).astype(BF16)
    qn = jnp.dot(cq, wn_ref[...], preferred_element_type=F32)
    qabs = _bdot(qn, bdk_ref[...])
    pa = jnp.dot(cq, wpa_ref[...], preferred_element_type=F32)
    pb = jnp.dot(cq, wpb_ref[...], preferred_element_type=F32)
    for h in range(N_HEADS):
        sl = slice(h * LANE, (h + 1) * LANE)
        q_pe = pa[:, sl] * cos + pb[:, sl] * sin
        q_out[:, h, :, 0:LANE] = qabs[:, sl].reshape(nb, tt, LANE).astype(q_out.dtype)
        q_out[:, h, :, LANE:QK_W] = q_pe.reshape(nb, tt, LANE).astype(q_out.dtype)
    ckv = _rms(ckv_ref[...].astype(F32), gkv_ref[...])
    kpe = kpa_ref[...].astype(F32) * cos + kpb_ref[...].astype(F32) * sin
    ckv_out[...] = ckv
    kpe_out[...] = kpe[:, :MLA_ROPE]
    kf_out[...] = jnp.concatenate([ckv, kpe], axis=-1).astype(BF16)
    if rest:
        rest[0][0] = ckv.T.astype(BF16)


def _mla_pre(proj, cos, sin, gq, gkv, wn, wpa, wpb, bdk, nbat, t, nb, tt, q_dtype, emit_vt):
    tper = t // tt
    tm = nb * tt
    hw = N_HEADS * LANE
    kern = functools.partial(_mla_pre_kernel, nb=nb, tt=tt)
    tab_spec = pl.BlockSpec((tm, LANE), lambda b, t_: (t_, 0))
    vt_specs, vt_shapes = [], []
    if emit_vt:
        assert nb == 1
        vt_specs = [pl.BlockSpec((1, MLA_KVLORA, tt), lambda b, t_: (b * tper + t_, 0, 0))]
        vt_shapes = [jax.ShapeDtypeStruct((nbat * tper, MLA_KVLORA, tt), BF16)]
    return pl.pallas_call(
        kern,
        grid=(nbat // nb, tper),
        in_specs=[_tok_spec(tm, MLA_QLORA, OFF_QA // MLA_QLORA, tper), _tok_spec(tm, LANE, OFF_CKV // LANE, tper),
                  _tok_spec(tm, LANE, OFF_KPA // LANE, tper), _tok_spec(tm, LANE, OFF_KPB // LANE, tper),
                  tab_spec, tab_spec, _const_spec((1, MLA_QLORA)), _const_spec((1, MLA_KVLORA)),
                  _const_spec((MLA_QLORA, BRANCH_W)), _const_spec((MLA_QLORA, hw)), _const_spec((MLA_QLORA, hw)),
                  _const_spec((BRANCH_W, hw))],
        out_specs=[pl.BlockSpec((nb, N_HEADS, tt, QK_W), lambda b, t_: (b, 0, t_, 0)),
                   _tok_spec(tm, MLA_KVLORA, 0, tper), _tok_spec(tm, MLA_ROPE, 0, tper), _tok_spec(tm, QK_W, 0, tper)]
                  + vt_specs,
        out_shape=[jax.ShapeDtypeStruct((nbat, N_HEADS, t, QK_W), q_dtype),
                   jax.ShapeDtypeStruct((nbat * t, MLA_KVLORA), F32),
                   jax.ShapeDtypeStruct((nbat * t, MLA_ROPE), F32),
                   jax.ShapeDtypeStruct((nbat * t, QK_W), BF16)] + vt_shapes,
        compiler_params=_cparams(2, 48),
        name="mla_pre",
    )(proj, proj, proj, proj, cos, sin, gq, gkv, wn, wpa, wpb, bdk)


def _heads_to_lanes(o, rows_per_head):
    return jnp.concatenate([o[h * rows_per_head:(h + 1) * rows_per_head] for h in range(N_HEADS)], axis=-1)


def _attn_prompt_kernel(q_ref, kf_ref, vt_ref, wvt_ref, y_ref, m_scr, l_scr, acc_scr, s_scr, *, tq, tk, scale):
    i = pl.program_id(1)
    n_full = (i * tq) // tk
    hpg = ATTN_GROUP_W // tq
    w2 = hpg * tq
    krow = lax.broadcasted_iota(jnp.int32, (tk, w2), 0)
    qcol = i * tq + (lax.broadcasted_iota(jnp.int32, (tk, w2), 1) & (tq - 1))
    heads = range(N_HEADS // hpg)
    m_scr[...] = jnp.full(m_scr.shape, NEG_BIG, F32)
    l_scr[...] = jnp.zeros(l_scr.shape, F32)
    acc_scr[...] = jnp.zeros(acc_scr.shape, F32)

    def scores(j):
        kblk = kf_ref[pl.ds(pl.multiple_of(j * tk, tk), tk), :]
        return [lax.dot_general(kblk, q_ref[0, hpg * h:hpg * (h + 1)].reshape(w2, QK_W), (((1,), (1,)), ((), ())),
                                preferred_element_type=F32)
                for h in heads]

    c2 = scale * math.log2(math.e)

    def step(j, masked):
        vt = vt_ref[j]
        s = [s_scr[h] for h in heads]
        if masked:
            keep = j * tk + krow <= qcol
            s = [jnp.where(keep, s[h], NEG_BIG) for h in heads]
        else:
            s_next = scores(j + 1)
            for h in heads:
                s_scr[h] = s_next[h]
        m_old = [m_scr[h] for h in heads]
        m_new = [jnp.maximum(m_old[h], jnp.max(s[h], axis=0, keepdims=True)) for h in heads]
        p = [jnp.exp2((s[h] - m_new[h]) * c2) for h in heads]
        alpha = [jnp.exp2((m_old[h] - m_new[h]) * c2) for h in heads]
        pv = [jnp.dot(vt, p[h].astype(BF16), preferred_element_type=F32) for h in heads]
        for h in heads:
            m_scr[h] = m_new[h]
            l_scr[h] = alpha[h] * l_scr[h] + jnp.sum(p[h], axis=0, keepdims=True)
            acc_scr[h] = alpha[h] * acc_scr[h] + pv[h]

    def body(j, carry):
        step(j, False)
        return carry

    s_first = scores(0)
    for h in heads:
        s_scr[h] = s_first[h]
    lax.fori_loop(0, n_full, body, 0)
    step(n_full, True)
    o = [acc_scr[h] / l_scr[h] for h in heads]
    outs = [_bdot(wvt_ref[hh], o[hh // hpg][:, (hh % hpg) * tq:(hh % hpg + 1) * tq]) for hh in range(N_HEADS)]
    y_ref[...] = jnp.concatenate(outs, axis=0).T


def _attn_prompt(q_full, kf, vt, wvt, nbat, t, tq, tk):
    tper = t // tq
    kper = t // tk
    assert ATTN_GROUP_W % tq == 0
    ngrp = N_HEADS * tq // ATTN_GROUP_W
    scale = (HEAD_DIM + MLA_ROPE) ** -0.5
    kern = functools.partial(_attn_prompt_kernel, tq=tq, tk=tk, scale=scale)
    return pl.pallas_call(
        kern,
        grid=(nbat, tper),
        in_specs=[pl.BlockSpec((1, N_HEADS, tq, QK_W), lambda b, i: (b, 0, i, 0)),
                  pl.BlockSpec((t, QK_W), lambda b, i: (b, 0)),
                  pl.BlockSpec((kper, MLA_KVLORA, tk), lambda b, i: (b, 0, 0)),
                  _const_spec(wvt.shape)],
        out_specs=pl.BlockSpec((tq, BRANCH_W), lambda b, i: (b * tper + i, 0)),
        out_shape=jax.ShapeDtypeStruct((nbat * t, BRANCH_W), F32),
        scratch_shapes=[pltpu.VMEM((ngrp, 1, ATTN_GROUP_W), F32), pltpu.VMEM((ngrp, 1, ATTN_GROUP_W), F32),
                        pltpu.VMEM((ngrp, MLA_KVLORA, ATTN_GROUP_W), F32),
                        pltpu.VMEM((ngrp, tk, ATTN_GROUP_W), F32)],
        compiler_params=_cparams(2, 40),
        name="attn_prompt",
    )(q_full, kf, vt, wvt)


def _attn_sample_kernel(pt_ref, q_ref, *refs, n_pages, pg, ts, page, scale):
    ckv_pages = refs[:n_pages]
    kpe_pages = refs[n_pages:2 * n_pages]
    ckvn_ref, kpen_ref, bdv_ref, y_ref = refs[2 * n_pages:]
    r = N_HEADS * ts
    groups = range(n_pages // pg)

    q = q_ref[0].reshape(r, QK_W)
    q_lat = q[:, :MLA_KVLORA].astype(BF16)
    q_pe = q[:, MLA_KVLORA:MLA_KVLORA + MLA_ROPE].astype(BF16)

    c_all = [jnp.concatenate([ckv_pages[g * pg + i][0, 0].astype(BF16) for i in range(pg)], axis=0)
             for g in groups]
    kpt_all = [jnp.concatenate([kpe_pages[g * pg + i][0, 0].astype(BF16) for i in range(pg)], axis=1)
               for g in groups]
    s = [(lax.dot_general(q_lat, c_all[g], (((1,), (1,)), ((), ())), preferred_element_type=F32)
          + jnp.dot(q_pe, kpt_all[g], preferred_element_type=F32)) * scale for g in groups]
    pad = jnp.zeros((page - ts, MLA_KVLORA), F32)
    cn = jnp.concatenate([ckvn_ref[...], pad], axis=0).astype(BF16)
    kn = jnp.concatenate([kpen_ref[...], pad[:, :MLA_ROPE]], axis=0).astype(BF16)
    sn = (lax.dot_general(q_lat, cn, (((1,), (1,)), ((), ())), preferred_element_type=F32)
          + lax.dot_general(q_pe, kn, (((1,), (1,)), ((), ())), preferred_element_type=F32)) * scale
    qpos = lax.broadcasted_iota(jnp.int32, (r, page), 0) & (ts - 1)
    kpos = lax.broadcasted_iota(jnp.int32, (r, page), 1)
    s.append(jnp.where(kpos <= qpos, sn, NEG_BIG))
    vals = c_all + [cn]
    parts = range(len(s))
    m = [jnp.max(s[g], axis=-1, keepdims=True) for g in parts]
    p = [jnp.exp(s[g] - m[g]) for g in parts]
    l = [jnp.sum(p[g], axis=-1, keepdims=True) for g in parts]
    acc = [jnp.dot(p[g].astype(BF16), vals[g], preferred_element_type=F32) for g in parts]
    m_all = m[0]
    for g in parts[1:]:
        m_all = jnp.maximum(m_all, m[g])
    w = [jnp.exp(m[g] - m_all) for g in parts]
    l_all = w[0] * l[0]
    acc_all = w[0] * acc[0]
    for g in parts[1:]:
        l_all = l_all + w[g] * l[g]
        acc_all = acc_all + w[g] * acc[g]
    o = _heads_to_lanes(acc_all / l_all, ts)
    y_ref[...] = _bdot(o, bdv_ref[...])


def _attn_sample(q_full, cache_ckv, cache_kpe, page_table, layer, ckv_new, kpe_new, bdv, nbat, ts):
    n_pages = page_table.shape[1]
    page = cache_ckv.shape[2]
    pg = math.gcd(n_pages, 8)
    scale = (HEAD_DIM + MLA_ROPE) ** -0.5
    kern = functools.partial(_attn_sample_kernel, n_pages=n_pages, pg=pg, ts=ts, page=page, scale=scale)

    def page_spec(i, rows, width):
        return pl.BlockSpec((1, 1, rows, width), lambda b, pt: (layer, pt[b * n_pages + i], 0, 0))

    in_specs = ([pl.BlockSpec((1, N_HEADS, ts, QK_W), lambda b, pt: (b, 0, 0, 0))]
                + [page_spec(i, page, MLA_KVLORA) for i in range(n_pages)]
                + [page_spec(i, MLA_ROPE, page) for i in range(n_pages)]
                + [pl.BlockSpec((ts, MLA_KVLORA), lambda b, pt: (b, 0)),
                   pl.BlockSpec((ts, MLA_ROPE), lambda b, pt: (b, 0)),
                   pl.BlockSpec((N_HEADS * LANE, BRANCH_W), lambda b, pt: (0, 0))])
    grid_spec = pltpu.PrefetchScalarGridSpec(
        num_scalar_prefetch=1,
        grid=(nbat,),
        in_specs=in_specs,
        out_specs=pl.BlockSpec((ts, BRANCH_W), lambda b, pt: (b, 0)),
    )
    return pl.pallas_call(
        kern,
        grid_spec=grid_spec,
        out_shape=jax.ShapeDtypeStruct((nbat * ts, BRANCH_W), F32),
        compiler_params=_cparams(1, 40),
        name="attn_sample",
    )(page_table.reshape(-1), q_full, *([cache_ckv] * n_pages), *([cache_kpe] * n_pages), ckv_new, kpe_new, bdv)


def _cumsum_rows(x, n):
    rows = lax.broadcasted_iota(jnp.int32, x.shape, 0)
    d = 1
    while d < n:
        x = x + jnp.where(rows >= d, pltpu.roll(x, d, axis=0), 0.0)
        d *= 2
    return x


def _inverse_masks(n):
    rows = lax.broadcasted_iota(jnp.int32, (n, n), 0)
    cols = lax.broadcasted_iota(jnp.int32, (n, n), 1)
    masks = [rows == cols, (rows >> 1) == (cols >> 1)]
    k = 1
    while (1 << k) < n:
        masks.append(((rows >> k) ^ (cols >> k)) == 1)
        k += 1
    return masks


def _gdn_solve(q_raw, k_raw, v_raw, gcm, beta, tril, strict, masks):
    n = range(len(q_raw))
    eye = jnp.where(masks[0], 1.0, 0.0)
    q = [x * lax.rsqrt(jnp.sum(x * x, axis=-1, keepdims=True) + EPS) * (HEAD_DIM ** -0.5) for x in q_raw]
    k = [x * lax.rsqrt(jnp.sum(x * x, axis=-1, keepdims=True) + EPS) for x in k_raw]
    kb = [k[p] * beta[p] for p in n]
    decay = []
    for p in n:
        diff = gcm[p] - gcm[p].T
        decay.append(jnp.where(tril, jnp.exp(jnp.where(tril, diff, 0.0)), 0.0))
    kk = [_bdot_nt(kb[p], k[p]) for p in n]
    a_mat = [jnp.where(strict, kk[p] * decay[p], 0.0) for p in n]
    t = [eye - jnp.where(masks[1], a_mat[p], 0.0) for p in n]
    for m in masks[2:]:
        tl = [_bdot(t[p], jnp.where(m, a_mat[p], 0.0)) for p in n]
        t = [t[p] - _bdot(tl[p], t[p]) for p in n]
    egc = [jnp.exp(gcm[p]) for p in n]
    sol = [_bdot(t[p], jnp.concatenate([v_raw[p] * beta[p], kb[p] * egc[p]], axis=-1)) for p in n]
    qk = [_bdot_nt(q[p], k[p]) for p in n]
    intra = [jnp.where(tril, qk[p] * decay[p], 0.0) for p in n]
    qe = [q[p] * egc[p] for p in n]
    return sol, intra, qe, k


def _head_inputs(gab_ref, exp_ref):
    ab = jnp.dot(gab_ref[...], exp_ref[...], preferred_element_type=F32)
    return ab[:, :BRANCH_W], ab[:, BRANCH_W:]


def _gdn_short_kernel(qkv_ref, gab_ref, z_ref, cst_ref, s0_ref, w_ref, alog_ref, dtb_ref, gn_ref, exp_ref,
                      y_ref, sf_ref, csto_ref, *, nseq, ts):
    cc = nseq * ts
    n2 = 2 * cc
    npair = N_HEADS // 2
    c3 = qkv_ref.shape[1]
    c = BRANCH_W
    u3 = qkv_ref[...].astype(F32).reshape(nseq, ts, c3)
    conv = _silu(_causal_conv(u3, cst_ref[...], w_ref[...], GDN_K)).reshape(cc, c3)
    csto_ref[...] = u3
    ga, gb = _head_inputs(gab_ref, exp_ref)
    g3 = (-jnp.exp(alog_ref[...]) * _softplus(ga + dtb_ref[...])).reshape(nseq, ts, c)
    b_tile = jax.nn.sigmoid(gb)
    tok = lax.broadcasted_iota(jnp.int32, (nseq, ts, c), 1)
    d = 1
    while d < ts:
        g3 = g3 + jnp.where(tok >= d, pltpu.roll(g3, d, axis=1), 0.0)
        d *= 2
    gc = g3.reshape(cc, c)
    g_tot = g3[:, ts - 1:ts, :]
    gl = jnp.broadcast_to(g_tot, (nseq, ts, c)).reshape(cc, c)

    rows = lax.broadcasted_iota(jnp.int32, (n2, n2), 0)
    cols = lax.broadcasted_iota(jnp.int32, (n2, n2), 1)
    same_head = (rows >> 6) == (cols >> 6)
    same_seq = (rows >> 3) == (cols >> 3)
    tril = same_seq & (rows >= cols)
    strict = same_seq & (rows > cols)
    masks = _inverse_masks(n2)[:4]
    gn = gn_ref[...]
    pairs = range(npair)

    def ls(pr):
        return slice(pr * LANE, (pr + 1) * LANE)

    def stack(x):
        return jnp.where(same_head, jnp.concatenate([x, x], axis=0), 0.0)

    def twice(x):
        return jnp.concatenate([x, x], axis=0)

    erow = (lax.broadcasted_iota(jnp.int32, (n2, nseq * n2), 0) >> 3) & (nseq - 1)
    eblk = lax.broadcasted_iota(jnp.int32, (n2, nseq * n2), 1) >> 7
    emask = erow == eblk

    def expand(x):
        return jnp.where(emask, jnp.concatenate([x] * nseq, axis=1), 0.0)

    gcm = [stack(gc[:, ls(pr)]) for pr in pairs]
    glm = [stack(gl[:, ls(pr)]) for pr in pairs]
    beta = [twice(b_tile[:, ls(pr)]) for pr in pairs]
    sol, intra, qe, k = _gdn_solve([stack(conv[:, ls(pr)]) for pr in pairs],
                                   [stack(conv[:, c + pr * LANE:c + (pr + 1) * LANE]) for pr in pairs],
                                   [stack(conv[:, 2 * c + pr * LANE:2 * c + (pr + 1) * LANE]) for pr in pairs],
                                   gcm, beta, tril, strict, masks)
    kdec = [k[pr] * jnp.exp(glm[pr] - gcm[pr]) for pr in pairs]
    zero_blk = jnp.zeros((HEAD_DIM, HEAD_DIM), F32)
    s_old = []
    sdec = []
    for pr in pairs:
        blocks = []
        decs = []
        for b in range(nseq):
            top = jnp.concatenate([s0_ref[b, 2 * pr], zero_blk], axis=-1)
            bot = jnp.concatenate([zero_blk, s0_ref[b, 2 * pr + 1]], axis=-1)
            blocks += [top, bot]
            decs.append(jnp.broadcast_to(jnp.exp(g_tot[b, :, ls(pr)]), (n2, LANE)))
        s_old.append(jnp.concatenate(blocks, axis=0))
        sdec.append(jnp.concatenate(decs, axis=0))
    v_new = [sol[pr][:, :n2] - _bdot(expand(sol[pr][:, n2:]), s_old[pr]) for pr in pairs]
    o_st = [_bdot(expand(qe[pr]), s_old[pr]) for pr in pairs]
    o_in = [_bdot(intra[pr], v_new[pr]) for pr in pairs]
    s_up = [_bdot_tn(expand(kdec[pr]), v_new[pr]) for pr in pairs]
    for pr in pairs:
        s_new = s_old[pr] * sdec[pr] + s_up[pr]
        for b in range(nseq):
            sf_ref[b, 2 * pr] = s_new[b * n2:b * n2 + HEAD_DIM, :HEAD_DIM]
            sf_ref[b, 2 * pr + 1] = s_new[b * n2 + HEAD_DIM:(b + 1) * n2, HEAD_DIM:]
        o = o_st[pr] + o_in[pr]
        o = o * lax.rsqrt(jnp.sum(o * o, axis=-1, keepdims=True) * (1.0 / HEAD_DIM) + EPS)
        o = o[:cc] + o[cc:]
        y_ref[:, ls(pr)] = o * gn[:, ls(pr)] * _silu(z_ref[:, ls(pr)].astype(F32))


def _gdn_short(proj, cst8, s0, w, alog_b, dtb_b, gn_b, expand, nbat, ts):
    nseq = GDN_CHUNK // ts
    assert ts == SUBLANE and nbat % nseq == 0
    c = BRANCH_W
    c3 = 3 * c
    tm = nseq * ts
    kern = functools.partial(_gdn_short_kernel, nseq=nseq, ts=ts)
    s_spec = pl.BlockSpec((nseq, N_HEADS, HEAD_DIM, HEAD_DIM), lambda i: (i, 0, 0, 0))
    cst_spec = pl.BlockSpec((nseq, SUBLANE, c3), lambda i: (i, 0, 0))

    def tok(width, off):
        return pl.BlockSpec((tm, width), lambda i: (i, off // width))

    return pl.pallas_call(
        kern,
        grid=(nbat // nseq,),
        in_specs=[tok(c3, OFF_GQKV), tok(LANE, OFF_GAB), tok(c, OFF_GZ), cst_spec, s_spec,
                  _const_spec((GDN_K, c3)), _const_spec((1, c)), _const_spec((1, c)), _const_spec((1, c)),
                  _const_spec((LANE, 2 * c))],
        out_specs=[pl.BlockSpec((tm, c), lambda i: (i, 0)), s_spec, cst_spec],
        out_shape=[jax.ShapeDtypeStruct((nbat * ts, c), F32),
                   jax.ShapeDtypeStruct((nbat, N_HEADS, HEAD_DIM, HEAD_DIM), F32),
                   jax.ShapeDtypeStruct((nbat, SUBLANE, c3), F32)],
        compiler_params=_cparams(1, 40),
        name="gdn_short",
    )(proj, proj, proj, cst8, s0, w, alog_b, dtb_b, gn_b, expand)


def _gdn_kernel(qkv_ref, gab_ref, z_ref, cst_ref, s0_ref, w_ref, alog_ref, dtb_ref, gn_ref, exp_ref,
                y_ref, sf_ref, csto_ref, s_scr, c_scr, qkv_s, g_ref, beta_ref, *, n_chunks):
    tc = qkv_ref.shape[0]
    c3 = qkv_ref.shape[1]
    ga, gb = _head_inputs(gab_ref, exp_ref)
    c8 = _carry_in(pl.program_id(1), cst_ref, c_scr)
    u3 = qkv_ref[...].astype(F32).reshape(1, tc, c3)
    qkv_s[...] = _silu(_causal_conv(u3, c8, w_ref[...], GDN_K)).reshape(tc, c3)
    g_ref[...] = -jnp.exp(alog_ref[...]) * _softplus(ga + dtb_ref[...])
    beta_ref[...] = jax.nn.sigmoid(gb)
    hist = u3[:, tc - SUBLANE:, :]
    c_scr[...] = hist
    csto_ref[...] = hist
    q_ref = qkv_s.at[:, 0:BRANCH_W]
    k_ref = qkv_s.at[:, BRANCH_W:2 * BRANCH_W]
    v_ref = qkv_s.at[:, 2 * BRANCH_W:3 * BRANCH_W]

    cc = GDN_CHUNK
    n2 = 2 * cc
    npair = N_HEADS // 2
    zero_blk = jnp.zeros((HEAD_DIM, HEAD_DIM), F32)

    @pl.when(pl.program_id(1) == 0)
    def _():
        for pr in range(npair):
            top = jnp.concatenate([s0_ref[0, 2 * pr], zero_blk], axis=-1)
            bot = jnp.concatenate([zero_blk, s0_ref[0, 2 * pr + 1]], axis=-1)
            s_scr[pr] = jnp.concatenate([top, bot], axis=0)

    rows = lax.broadcasted_iota(jnp.int32, (n2, n2), 0)
    cols = lax.broadcasted_iota(jnp.int32, (n2, n2), 1)
    same = (rows >> 6) == (cols >> 6)
    tril = same & (rows >= cols)
    strict = same & (rows > cols)
    masks = _inverse_masks(n2)[:7]
    gn = gn_ref[...]
    probs = [(ci, pr) for ci in range(n_chunks) for pr in range(npair)]

    def rs(ci):
        return slice(ci * cc, (ci + 1) * cc)

    def ls(pr):
        return slice(pr * LANE, (pr + 1) * LANE)

    def stack(x):
        return jnp.where(same, jnp.concatenate([x, x], axis=0), 0.0)

    def twice(x):
        return jnp.concatenate([x, x], axis=0)

    gc_all = [_cumsum_rows(g_ref[rs(ci), :], cc) for ci in range(n_chunks)]
    gcm = [stack(gc_all[ci][:, ls(pr)]) for ci, pr in probs]
    beta = [twice(beta_ref[rs(ci), ls(pr)]) for ci, pr in probs]
    sol, intra, qe, k = _gdn_solve([stack(q_ref[rs(ci), ls(pr)]) for ci, pr in probs],
                                   [stack(k_ref[rs(ci), ls(pr)]) for ci, pr in probs],
                                   [stack(v_ref[rs(ci), ls(pr)]) for ci, pr in probs],
                                   gcm, beta, tril, strict, masks)
    g_last = [gc_all[ci][cc - 1:cc, ls(pr)] for ci, pr in probs]
    kdec = [k[p] * jnp.exp(g_last[p] - gcm[p]) for p in range(len(probs))]
    sdec = [jnp.exp(g_last[p]) for p in range(len(probs))]

    for ci in range(n_chunks):
        ps = [ci * npair + pr for pr in range(npair)]
        s_old = [s_scr[pr] for pr in range(npair)]
        v_new = [sol[p][:, :n2] - _bdot(sol[p][:, n2:], s_old[pr]) for pr, p in enumerate(ps)]
        o_st = [_bdot(qe[p], s_old[pr]) for pr, p in enumerate(ps)]
        o_in = [_bdot(intra[p], v_new[pr]) for pr, p in enumerate(ps)]
        s_up = [_bdot_tn(kdec[p], v_new[pr]) for pr, p in enumerate(ps)]
        for pr, p in enumerate(ps):
            s_scr[pr] = s_old[pr] * sdec[p] + s_up[pr]
            o = o_st[pr] + o_in[pr]
            o = o * lax.rsqrt(jnp.sum(o * o, axis=-1, keepdims=True) * (1.0 / HEAD_DIM) + EPS)
            o = o[:cc] + o[cc:]
            y_ref[rs(ci), ls(pr)] = o * gn[:, ls(pr)] * _silu(z_ref[rs(ci), ls(pr)].astype(F32))

    @pl.when(pl.program_id(1) == pl.num_programs(1) - 1)
    def _():
        for pr in range(npair):
            s_pair = s_scr[pr]
            sf_ref[0, 2 * pr] = s_pair[:HEAD_DIM, :HEAD_DIM]
            sf_ref[0, 2 * pr + 1] = s_pair[HEAD_DIM:, HEAD_DIM:]


def _gdn(proj, cst8, s0, w, alog_b, dtb_b, gn_b, expand, nbat, t, tc):
    tper = t // tc
    c = BRANCH_W
    c3 = 3 * c
    kern = functools.partial(_gdn_kernel, n_chunks=tc // GDN_CHUNK)
    s_spec = pl.BlockSpec((1, N_HEADS, HEAD_DIM, HEAD_DIM), lambda b, t_: (b, 0, 0, 0))
    cst_spec = pl.BlockSpec((1, SUBLANE, c3), lambda b, t_: (b, 0, 0))
    return pl.pallas_call(
        kern,
        grid=(nbat, tper),
        in_specs=[_tok_spec(tc, c3, OFF_GQKV // c3, tper), _tok_spec(tc, LANE, OFF_GAB // LANE, tper),
                  _tok_spec(tc, c, OFF_GZ // c, tper),
                  cst_spec, s_spec, _const_spec((GDN_K, c3)), _const_spec((1, c)), _const_spec((1, c)),
                  _const_spec((1, c)), _const_spec((LANE, 2 * c))],
        out_specs=[_tok_spec(tc, c, 0, tper), s_spec, cst_spec],
        out_shape=[jax.ShapeDtypeStruct((nbat * t, c), F32),
                   jax.ShapeDtypeStruct((nbat, N_HEADS, HEAD_DIM, HEAD_DIM), F32),
                   jax.ShapeDtypeStruct((nbat, SUBLANE, c3), F32)],
        scratch_shapes=[pltpu.VMEM((N_HEADS // 2, 2 * HEAD_DIM, 2 * HEAD_DIM), F32),
                        pltpu.VMEM((1, SUBLANE, c3), F32), pltpu.VMEM((tc, c3), F32),
                        pltpu.VMEM((tc, c), F32), pltpu.VMEM((tc, c), F32)],
        compiler_params=_cparams(2, 40),
        name="gdn_rule",
    )(proj, proj, proj, cst8, s0, w, alog_b, dtb_b, gn_b, expand)


def _merge_kernel(x_ref, mod_ref, g_ref, bg_ref, cg_ref, xt_ref, st_ref, wsc_ref, yb_ref, yc_ref,
                  lx_ref, lg_ref, lst_ref, h0_ref, wl_ref, cbl_ref, wgl_ref, bgl_ref, lam_ref,
                  wg_ref, wbo_ref, wmix_ref, o_ref, sto_ref, lsto_ref, hl_ref, c_scr, lc_scr, h_scr, *, pos0):
    x = x_ref[...]
    nb, tt, d = x.shape
    c = bg_ref.shape[-1]
    t_idx = pl.program_id(1)
    mod = mod_ref[...]
    h = _modulated_norm(x, g_ref[...], mod, 0, 1).reshape(nb * tt, d).astype(BF16)
    c8 = _carry_in(t_idx, st_ref, c_scr)
    u3 = (cg_ref[...].astype(F32) * xt_ref[...].astype(F32)).reshape(nb, tt, c)
    y_a = bg_ref[...].astype(F32) * _causal_conv(u3, c8, wsc_ref[...], SC_K).reshape(nb * tt, c)
    hist = u3[:, tt - SUBLANE:, :]
    c_scr[...] = hist
    sto_ref[...] = hist
    lc8 = _carry_in(t_idx, lst_ref, lc_scr)

    @pl.when(t_idx == 0)
    def _():
        h_scr[...] = h0_ref[...]

    y_d, lhist, h_last = _lru_branch(lx_ref, lg_ref, lc8, h_scr[...], wl_ref, cbl_ref, wgl_ref, bgl_ref, lam_ref,
                                     nb, tt, pos0 + t_idx * tt)
    lc_scr[...] = lhist
    lsto_ref[...] = lhist
    h_scr[...] = h_last
    hl_ref[...] = h_last
    acc = None
    for n, y in enumerate((y_a, yb_ref[...], yc_ref[...], y_d)):
        gate = jax.nn.sigmoid(jnp.dot(h, wg_ref[:, n * d:(n + 1) * d], preferred_element_type=F32))
        term = gate * _bdot(y, wbo_ref[n])
        acc = term if acc is None else acc + term
    mix = _bdot(acc, wmix_ref[...])
    o_ref[...] = x + mod[:, 2:3, :] * mix.reshape(nb, tt, d)


def _merge(x3, mod, g, proj, st8, wsc, ys, lru_st8, lru_h0, lru_w, wg, wbo, wmix, nb, tt, pos0):
    nbat, t, d = x3.shape
    tper = t // tt
    tm = nb * tt
    c = BRANCH_W
    x_spec = pl.BlockSpec((nb, tt, d), lambda b, t_: (b, t_, 0))
    h_spec = pl.BlockSpec((nb, 1, c), lambda b, t_: (b, 0, 0))
    return pl.pallas_call(
        functools.partial(_merge_kernel, pos0=pos0),
        grid=(nbat // nb, tper),
        in_specs=[x_spec, pl.BlockSpec((nb, 6, d), lambda b, t_: (b, 0, 0)), _const_spec((1, d)),
                  _tok_spec(tm, c, OFF_SC // c, tper), _tok_spec(tm, c, OFF_SC // c + 1, tper),
                  _tok_spec(tm, c, OFF_SC // c + 2, tper), _state_spec(nb, c), _const_spec((SC_K, c))]
                 + [_tok_spec(tm, c, 0, tper)] * 2
                 + [_tok_spec(tm, c, OFF_LX // c, tper), _tok_spec(tm, c, OFF_LG // c, tper), _state_spec(nb, c),
                    h_spec, _const_spec((LRU_K, c)), _const_spec((1, c)), _const_spec((c, 2 * c)),
                    _const_spec((1, 2 * c)), _const_spec((1, c))]
                 + [_const_spec(wg.shape), _const_spec(wbo.shape), _const_spec(wmix.shape)],
        out_specs=[x_spec, _state_spec(nb, c), _state_spec(nb, c), h_spec],
        out_shape=[jax.ShapeDtypeStruct(x3.shape, F32), jax.ShapeDtypeStruct((nbat, SUBLANE, c), F32),
                   jax.ShapeDtypeStruct((nbat, SUBLANE, c), F32), jax.ShapeDtypeStruct((nbat, 1, c), F32)],
        scratch_shapes=[pltpu.VMEM((nb, SUBLANE, c), F32), pltpu.VMEM((nb, SUBLANE, c), F32),
                        pltpu.VMEM((nb, 1, c), F32)],
        compiler_params=_cparams(2, 48),
        name="merge",
    )(x3, mod, g, proj, proj, proj, st8, wsc, *ys, proj, proj, lru_st8, lru_h0, *lru_w, wg, wbo, wmix)


def _ffn_kernel(x_ref, mod_ref, g_ref, win_ref, wout_ref, gf_ref, o_ref, *, final):
    x = x_ref[...]
    nb, tt, d = x.shape
    mod = mod_ref[...]
    fh = wout_ref.shape[0]
    h = _modulated_norm(x, g_ref[...], mod, 3, 4).reshape(nb * tt, d).astype(BF16)
    gate = jnp.dot(h, win_ref[:, :fh], preferred_element_type=F32)
    up = jnp.dot(h, win_ref[:, fh:], preferred_element_type=F32)
    out = _bdot(_silu(gate) * up, wout_ref[...])
    xo = x + mod[:, 5:6, :] * out.reshape(nb, tt, d)
    o_ref[...] = _rms(xo, gf_ref[...]) if final else xo


def _ffn(x3, mod, g, win, wout, gf, nb, tt, final):
    nbat, t, d = x3.shape
    x_spec = pl.BlockSpec((nb, tt, d), lambda b, t_: (b, t_, 0))
    kern = functools.partial(_ffn_kernel, final=final)
    return pl.pallas_call(
        kern,
        grid=(nbat // nb, t // tt),
        in_specs=[x_spec, pl.BlockSpec((nb, 6, d), lambda b, t_: (b, 0, 0)), _const_spec((1, d)),
                  _const_spec(win.shape), _const_spec(wout.shape), _const_spec((1, d))],
        out_specs=x_spec,
        out_shape=jax.ShapeDtypeStruct(x3.shape, F32),
        compiler_params=_cparams(2, 56),
        name="ffn",
    )(x3, mod, g, win, wout, gf)


def _block_diag(blocks):
    h, r, c = blocks.shape
    eye = jnp.eye(h, dtype=blocks.dtype)
    return jnp.einsum('hrc,hg->hrgc', blocks, eye).reshape(h * r, h * c)


def _prep_layer_weights(w_in, w_qb, w_kvb, w_lru_gate_a, w_lru_gate_x):
    d = w_in.shape[0]
    c = BRANCH_W
    o = 0
    segs = {}
    for name, width in (('sc', 3 * c), ('qa', MLA_QLORA), ('ckv', MLA_KVLORA), ('kpe', MLA_ROPE), ('gqkv', 3 * c),
                        ('gz', c), ('ga', N_HEADS), ('gb', N_HEADS), ('lx', c), ('lg', c)):
        segs[name] = w_in[:, o:o + width]
        o += width
    half = MLA_ROPE // 2
    zpad = jnp.zeros((d, LANE - MLA_ROPE), F32)
    kpe = segs['kpe']
    kpa = jnp.concatenate([kpe, zpad], axis=1)
    kpb = jnp.concatenate([-kpe[:, half:], kpe[:, :half], zpad], axis=1)
    w_proj = jnp.concatenate(
        [segs['sc'], segs['gqkv'], segs['gz'], segs['lx'], segs['lg'], segs['qa'], segs['ckv'], kpa, kpb,
         segs['ga'], segs['gb'], jnp.zeros((d, LANE - 2 * N_HEADS), F32)],
        axis=1).astype(BF16)

    ql = w_qb.shape[0]
    wq = w_qb.reshape(ql, N_HEADS, HEAD_DIM + MLA_ROPE)
    wn = wq[:, :, :HEAD_DIM].reshape(ql, c)
    pe = wq[:, :, HEAD_DIM:]
    zq = jnp.zeros((ql, N_HEADS, LANE - MLA_ROPE), F32)
    wpa = jnp.concatenate([pe, zq], axis=2).reshape(ql, N_HEADS * LANE)
    wpb = jnp.concatenate([-pe[:, :, half:], pe[:, :, :half], zq], axis=2).reshape(ql, N_HEADS * LANE)
    w_kb = w_kvb[:, :, :HEAD_DIM]
    w_vb = w_kvb[:, :, HEAD_DIM:]
    bdk = _block_diag(jnp.transpose(w_kb, (1, 2, 0)))
    bdv = _block_diag(jnp.transpose(w_vb, (1, 0, 2)))
    wlg = jnp.concatenate([_block_diag(w_lru_gate_a), _block_diag(w_lru_gate_x)], axis=1)
    wvt = jnp.transpose(w_vb, (1, 2, 0))
    return dict(w_proj=w_proj, wn=wn.astype(BF16), wpa=wpa.astype(BF16), wpb=wpb.astype(BF16),
                bdk=bdk.astype(BF16), bdv=bdv.astype(BF16), wvt=wvt.astype(BF16), wlg=wlg.astype(BF16))


def _rope_tables(pos):
    half = MLA_ROPE // 2
    inv = ROPE_THETA ** (-jnp.arange(half, dtype=F32) / half)
    ang = pos.astype(F32)[:, None] * inv[None, :]
    pad = jnp.zeros((pos.shape[0], LANE - MLA_ROPE), F32)
    cos = jnp.concatenate([jnp.cos(ang), jnp.cos(ang), pad], axis=1)
    sin = jnp.concatenate([jnp.sin(ang), jnp.sin(ang), pad], axis=1)
    return cos, sin


def _pad_state(st, k_w):
    return jnp.pad(st, ((0, 0), (SUBLANE - (k_w - 1), 0), (0, 0)))


def _tile(t, pref):
    tt = min(t, pref)
    while t % tt:
        tt //= 2
    return tt


def _group_layer(x3, mod, lw, p, st, rope, attend, cfg):
    nbat, t, d = x3.shape
    nb, tt = cfg['nb'], cfg['tt']
    proj = _inproj(x3, mod, p['g_norm_mix'], lw['w_proj'], cfg['nb_proj'], cfg['tt_proj'])
    q_full, ckv, kpe, kf, *vt = _mla_pre(proj, rope[0], rope[1], p['g_q_norm'], p['g_kv_norm'], lw['wn'], lw['wpa'],
                                         lw['wpb'], lw['bdk'], nbat, t, nb, tt, cfg['q_dtype'], cfg['emit_vt'])
    y_b = attend(q_full, ckv, kpe, kf, vt, lw)
    gdn_w = (p['w_gdn_conv'], p['alog_b'], p['dtb_b'], p['gn_b'], p['head_expand'])
    if t < GDN_CHUNK:
        y_c, s_gdn, gc8 = _gdn_short(proj, st['gdn_conv'], st['gdn'], *gdn_w, nbat, t)
    else:
        y_c, s_gdn, gc8 = _gdn(proj, st['gdn_conv'], st['gdn'], *gdn_w, nbat, t, cfg['tc'])
    lru_w = (p['w_lru_conv'], p['b_lru_conv'], lw['wlg'], p['b_lru_gates'], p['nsl'])
    x1, sc8, lc8, h_last = _merge(x3, mod, p['g_norm_mix'], proj, st['sconv'], p['w_sc_conv'], (y_b, y_c),
                                  st['lru_conv'], st['lru'], lru_w, p['w_merge_gate'], p['w_branch_out'],
                                  p['w_mix_out'], cfg['nb_mm'], cfg['tt_mm'], cfg['pos0'])
    x2 = _ffn(x1, mod, p['g_norm_ffn'], p['w_ffn_in'], p['w_ffn_out'], p['g_final'], cfg['nb_mm'], cfg['tt_mm'],
              cfg['final'])
    new_st = dict(ckv=ckv.reshape(nbat, t, -1), kpe=kpe.reshape(nbat, t, -1),
                  sconv=sc8[:, SUBLANE - (SC_K - 1):], gdn_conv=gc8[:, SUBLANE - (GDN_K - 1):], gdn=s_gdn,
                  lru_conv=lc8[:, SUBLANE - (LRU_K - 1):], lru=h_last[:, 0, :])
    return x2, new_st


STATE_KEYS = ('ckv', 'kpe', 'sconv', 'gdn_conv', 'gdn', 'lru_conv', 'lru')


def kernel(x_prompt, x_sample, c_prompt, c_sample, cache_mla_ckv, cache_mla_kpe, page_table, state_sconv, state_gdn_conv, state_gdn, state_lru_conv, state_lru, w_ada, b_ada, g_norm_mix, g_norm_ffn, w_in, w_sc_conv, g_q_norm, w_qb, g_kv_norm, w_kvb, w_gdn_conv, gdn_a_log, gdn_dt_bias, g_gdn_norm, w_lru_conv, b_lru_conv, w_lru_gate_a, b_lru_gate_a, w_lru_gate_x, b_lru_gate_x, lru_lambda, w_branch_out, w_merge_gate, w_mix_out, w_ffn_in, w_ffn_out, g_final):
    bp, tp, d = x_prompt.shape
    bs, ts, _ = x_sample.shape
    depth = w_in.shape[0]
    n_pages = page_table.shape[1]
    past_len = n_pages * cache_mla_ckv.shape[2]
    c = BRANCH_W
    assert ts == SUBLANE and tp % GDN_CHUNK == 0 and d % LANE == 0

    mod_all = _ada(jnp.concatenate([c_prompt, c_sample], axis=0), w_ada, b_ada).reshape(depth, bp + bs, 6, d)

    nb_s = _tile(bs, 32)
    cos_p, sin_p = _rope_tables(jnp.arange(tp, dtype=jnp.int32))
    cos_s, sin_s = _rope_tables(past_len + jnp.arange(ts, dtype=jnp.int32))
    rope_p = (cos_p, sin_p)
    rope_s = (jnp.tile(cos_s, (nb_s, 1)), jnp.tile(sin_s, (nb_s, 1)))

    tq = _tile(tp, 256)
    tk = _tile(tp, 256)
    cfg_p = dict(nb=1, tt=tk, nb_proj=1, tt_proj=_tile(tp, 2048), q_dtype=BF16, emit_vt=True,
                 tc=_tile(tp, 256), pos0=0, nb_mm=1, tt_mm=_tile(tp, 512))
    cfg_s = dict(nb=nb_s, tt=ts, nb_proj=_tile(bs, 128), tt_proj=ts, q_dtype=F32, emit_vt=False,
                 tc=GDN_CHUNK, pos0=past_len, nb_mm=_tile(bs, 64), tt_mm=ts)
    cache_kpe_t = jnp.swapaxes(cache_mla_kpe, 2, 3)
    head_expand = (jnp.arange(LANE)[:, None] == jnp.arange(2 * c)[None, :] // HEAD_DIM).astype(BF16)

    xp, xs = x_prompt, x_sample
    out_p = {k: [] for k in STATE_KEYS}
    out_s = {k: [] for k in STATE_KEYS}
    for l in range(depth):
        lw = _prep_layer_weights(w_in[l], w_qb[l], w_kvb[l], w_lru_gate_a[l], w_lru_gate_x[l])
        p = dict(
            g_norm_mix=g_norm_mix[l][None], g_norm_ffn=g_norm_ffn[l][None], g_final=g_final[None],
            w_sc_conv=w_sc_conv[l], g_q_norm=g_q_norm[l][None], g_kv_norm=g_kv_norm[l][None],
            w_gdn_conv=w_gdn_conv[l],
            alog_b=jnp.repeat(gdn_a_log[l], HEAD_DIM)[None], dtb_b=jnp.repeat(gdn_dt_bias[l], HEAD_DIM)[None],
            gn_b=jnp.tile(g_gdn_norm[l], N_HEADS)[None], head_expand=head_expand,
            w_lru_conv=w_lru_conv[l], b_lru_conv=b_lru_conv[l][None],
            b_lru_gates=jnp.concatenate([b_lru_gate_a[l], b_lru_gate_x[l]])[None],
            nsl=lru_lambda[l][None],
            w_merge_gate=w_merge_gate[l].astype(BF16), w_branch_out=w_branch_out[l].astype(BF16),
            w_mix_out=w_mix_out[l].astype(BF16), w_ffn_in=w_ffn_in[l].astype(BF16),
            w_ffn_out=w_ffn_out[l].astype(BF16))
        final = l == depth - 1

        st_p = dict(sconv=jnp.zeros((bp, SUBLANE, c), F32), gdn_conv=jnp.zeros((bp, SUBLANE, 3 * c), F32),
                    gdn=jnp.zeros((bp, N_HEADS, HEAD_DIM, HEAD_DIM), F32),
                    lru_conv=jnp.zeros((bp, SUBLANE, c), F32), lru=jnp.zeros((bp, 1, c), F32))

        def attend_p(q_full, ckv, kpe, kf, vt, lw_):
            return _attn_prompt(q_full, kf, vt[0], lw_['wvt'], bp, tp, tq, tk)

        xp, nst_p = _group_layer(xp, mod_all[l, :bp], lw, p, st_p, rope_p, attend_p, dict(cfg_p, final=final))

        st_s = dict(sconv=_pad_state(state_sconv[l], SC_K), gdn_conv=_pad_state(state_gdn_conv[l], GDN_K),
                    gdn=state_gdn[l], lru_conv=_pad_state(state_lru_conv[l], LRU_K), lru=state_lru[l][:, None, :])

        def attend_s(q_full, ckv, kpe, kf, vt, lw_, layer=l):
            return _attn_sample(q_full, cache_mla_ckv, cache_kpe_t, page_table, layer, ckv, kpe, lw_['bdv'], bs, ts)

        xs, nst_s = _group_layer(xs, mod_all[l, bp:], lw, p, st_s, rope_s, attend_s, dict(cfg_s, final=final))
        for k in STATE_KEYS:
            out_p[k].append(nst_p[k])
            out_s[k].append(nst_s[k])

    return ((xp, xs) + tuple(jnp.stack(out_p[k]) for k in STATE_KEYS)
            + tuple(jnp.stack(out_s[k]) for k in STATE_KEYS))
```
